```python
import math
import jax, jax.numpy as jnp
from jax import lax
import numpy as np

D_MODEL = 1024
BATCH = 8
SEQ = 2048
DEPTH = 1
DEC_BATCH = 128
DEC_SEQ = 8
PAST_LEN = 16384
PAGE_SIZE = 128

CHUNK = 128
A_GROUPS = 4
A_WIDTH = 1024
A_GDIM = A_WIDTH // A_GROUPS
DN_HEADS = 8
DN_KDIM = 128
DN_VDIM = 128
DN_QK = DN_HEADS * DN_KDIM
DN_V = DN_HEADS * DN_VDIM
DN_CONV_CH = 2 * DN_QK + DN_V
CONV_W = 4
DN_CHUNK = 64
N_GROUPS = 4
EXP_PER_GROUP = 8
N_EXPERTS = N_GROUPS * EXP_PER_GROUP
TOP_K = 2
D_EXPERT = 256
N_MOD = 6
EPS = 1e-6
IN_SIZES = (A_WIDTH, A_WIDTH, DN_CONV_CH, DN_V, DN_HEADS, DN_HEADS, D_MODEL, D_MODEL)
IN_WIDTH = A_WIDTH * 2 + DN_CONV_CH + DN_V + DN_HEADS * 2 + D_MODEL * 2

kernel_name = 'hybrid_gmlp_gdn_hmoe_step'


def rmsnorm(x, w):
    xf = x.astype(jnp.float32)
    y = xf * lax.rsqrt(jnp.mean(xf * xf, axis=-1, keepdims=True) + EPS)
    return (y * w.astype(jnp.float32)).astype(x.dtype)


def layernorm(x, w, b):
    xf = x.astype(jnp.float32)
    mu = jnp.mean(xf, axis=-1, keepdims=True)
    xc = xf - mu
    var = jnp.mean(xc * xc, axis=-1, keepdims=True)
    return (xc * lax.rsqrt(var + EPS) * w.astype(jnp.float32) + b.astype(jnp.float32)).astype(x.dtype)


def l2norm(x):
    xf = x.astype(jnp.float32)
    return xf * lax.rsqrt(jnp.sum(xf * xf, axis=-1, keepdims=True) + EPS)


def causal_conv(x, buf, w):
    T = x.shape[1]
    xp = jnp.concatenate([buf.astype(x.dtype), x], axis=1)
    y = xp[:, 0:T] * w[0]
    for j in range(1, CONV_W):
        y = y + xp[:, j:j + T] * w[j]
    return y, xp[:, -(CONV_W - 1):]


def gated_delta_rule(q, k, v, g, beta, s0, chunk):
    B, T, H, K = q.shape
    V = v.shape[-1]
    N = T // chunk
    f32 = jnp.float32

    def chunks(a):
        return jnp.swapaxes(a.reshape((B, N, chunk) + a.shape[2:]), 2, 3)

    qc = chunks(q.astype(f32) * (K ** -0.5))
    kc = chunks(k.astype(f32))
    vc = chunks(v.astype(f32))
    gc = jnp.cumsum(chunks(g.astype(f32)), axis=-1)
    bc = chunks(beta.astype(f32))
    tril = jnp.tril(jnp.ones((chunk, chunk), dtype=bool))
    strict = jnp.tril(jnp.ones((chunk, chunk), dtype=bool), -1)
    diff = gc[..., :, None] - gc[..., None, :]
    decay = jnp.where(tril, jnp.exp(jnp.minimum(diff, 0.0)), 0.0)
    kb = kc * bc[..., None]
    vb = vc * bc[..., None]
    lmat = jnp.where(strict, jnp.einsum('bnhck,bnhsk->bnhcs', kb, kc) * decay, 0.0)
    rhs = jnp.concatenate([vb, kb * jnp.exp(gc)[..., None]], axis=-1)
    sol = lax.linalg.triangular_solve(lmat, rhs, left_side=True, lower=True, unit_diagonal=True)
    uc, wc = sol[..., :V], sol[..., V:]
    attn = jnp.where(tril, jnp.einsum('bnhck,bnhsk->bnhcs', qc, kc) * decay, 0.0)

    def step(S, xs):
        q_i, k_i, u_i, w_i, a_i, g_i = xs
        v_new = u_i - jnp.einsum('bhck,bhkv->bhcv', w_i, S)
        o = jnp.einsum('bhck,bhkv->bhcv', q_i * jnp.exp(g_i)[..., None], S) + jnp.einsum('bhcs,bhsv->bhcv', a_i, v_new)
        g_last = g_i[..., -1]
        S = S * jnp.exp(g_last)[..., None, None] + jnp.einsum(
            'bhck,bhcv->bhkv', k_i * jnp.exp(g_last[..., None] - g_i)[..., None], v_new)
        return S, o

    xs = tuple(jnp.moveaxis(a, 1, 0) for a in (qc, kc, uc, wc, attn, gc))
    S, o = lax.scan(step, s0.astype(f32), xs)
    o = jnp.transpose(o, (1, 0, 3, 2, 4)).reshape(B, T, H, V)
    return o, S


def token_mixers(h, conv_buf, s0, w_in, conv_w, a_log, dt_bias, dn_norm, ln_v_w, ln_v_b, w_s, b_s, w_pa, w_pb, w_o):
    B, T, _ = h.shape
    z = h @ w_in
    zu, zv, zqkv, zz, zb, za, ga, gb = jnp.split(z, np.cumsum(IN_SIZES)[:-1].tolist(), axis=-1)
    zu = jax.nn.gelu(zu, approximate=False)
    vn = layernorm(jax.nn.gelu(zv, approximate=False), ln_v_w, ln_v_b)
    L = min(CHUNK, T)
    N = T // L
    wm = (w_s * jnp.tril(jnp.ones((CHUNK, CHUNK), w_s.dtype)))[:, :L, :L]
    sv = jnp.einsum('gts,bnsgc->bntgc', wm, vn.reshape(B, N, L, A_GROUPS, A_GDIM))
    sv = sv + jnp.transpose(b_s[:, :L])[None, None, :, :, None]
    a_out = zu * sv.reshape(B, T, A_WIDTH)
    qkv, conv_new = causal_conv(zqkv, conv_buf, conv_w)
    qkv = jax.nn.silu(qkv)
    q, k, v = jnp.split(qkv, [DN_QK, 2 * DN_QK], axis=-1)
    q = l2norm(q.reshape(B, T, DN_HEADS, DN_KDIM))
    k = l2norm(k.reshape(B, T, DN_HEADS, DN_KDIM))
    v = v.reshape(B, T, DN_HEADS, DN_VDIM)
    beta = jax.nn.sigmoid(zb.astype(jnp.float32))
    g = -jnp.exp(a_log.astype(jnp.float32)) * jax.nn.softplus(za.astype(jnp.float32) + dt_bias.astype(jnp.float32))
    o, s_new = gated_delta_rule(q, k, v, g, beta, s0, min(DN_CHUNK, T))
    o = rmsnorm(o.astype(h.dtype), dn_norm) * jax.nn.silu(zz.reshape(B, T, DN_HEADS, DN_VDIM))
    b_out = o.reshape(B, T, DN_V)
    m = jax.nn.sigmoid(ga) * (a_out @ w_pa) + jax.nn.sigmoid(gb) * (b_out @ w_pb)
    return m @ w_o, conv_new, s_new.astype(h.dtype), vn


def hier_moe(h, w_rg, b_rg, w_re, b_re, w_e_gate, w_e_up, w_e_down):
    f32 = jnp.float32
    glog = (h @ w_rg).astype(f32) + b_rg.astype(f32)
    gprob = jax.nn.softmax(glog, axis=-1)
    gtop, gsel = lax.top_k(glog, 1)
    gp = jnp.take_along_axis(gprob, gsel, axis=-1)
    elog = ((h @ w_re).astype(f32) + b_re.astype(f32)).reshape(h.shape[:-1] + (N_GROUPS, EXP_PER_GROUP))
    elog_g = jnp.take_along_axis(elog, gsel[..., None], axis=-2)[..., 0, :]
    etop, esel = lax.top_k(elog_g, TOP_K)
    ew = jax.nn.softmax(etop, axis=-1) * gp
    exp_w = jnp.sum(jax.nn.one_hot(esel, EXP_PER_GROUP, dtype=f32) * ew[..., None], axis=-2)
    combine = jax.nn.one_hot(gsel[..., 0], N_GROUPS, dtype=f32)[..., :, None] * exp_w[..., None, :]
    combine = combine.astype(h.dtype)
    out = jnp.zeros_like(h)
    for gi in range(N_GROUPS):
        a = jnp.einsum('btd,edf->btef', h, w_e_gate[gi])
        u = jnp.einsum('btd,edf->btef', h, w_e_up[gi])
        act = jax.nn.silu(a) * u * combine[..., gi, :, None]
        out = out + jnp.einsum('btef,efd->btd', act, w_e_down[gi])
    return out


def decoder_layer(x, c, conv_buf, s0, w_ada, b_ada, norm1, norm2, w_in, conv_w, a_log, dt_bias, dn_norm,
                  ln_v_w, ln_v_b, w_s, b_s, w_pa, w_pb, w_o, w_rg, b_rg, w_re, b_re, w_e_gate, w_e_up, w_e_down):
    B = x.shape[0]
    mod = (jax.nn.silu(c) @ w_ada + b_ada).reshape(B, N_MOD, 1, D_MODEL)
    sh1, sc1, g1, sh2, sc2, g2 = [mod[:, i] for i in range(N_MOD)]
    h = rmsnorm(x, norm1) * (1 + sc1) + sh1
    mix, conv_new, s_new, vn = token_mixers(h, conv_buf, s0, w_in, conv_w, a_log, dt_bias, dn_norm,
                                            ln_v_w, ln_v_b, w_s, b_s, w_pa, w_pb, w_o)
    x = x + g1 * mix
    h = rmsnorm(x, norm2) * (1 + sc2) + sh2
    x = x + g2 * hier_moe(h, w_rg, b_rg, w_re, b_re, w_e_gate, w_e_up, w_e_down)
    return x, conv_new, s_new, vn


def setup_inputs(seed: int = 0) -> dict:
    key = jax.random.key(seed)
    ks = jax.random.split(key, 32)
    f32 = jnp.float32

    def nrm(k, shape, scale):
        return jax.random.normal(k, shape, f32) * scale

    inp = {}
    inp['x_prompt'] = nrm(ks[0], (BATCH, SEQ, D_MODEL), 1.0)
    inp['x_sample'] = nrm(ks[1], (DEC_BATCH, DEC_SEQ, D_MODEL), 1.0)
    inp['c_prompt'] = nrm(ks[2], (BATCH, D_MODEL), 1.0)
    inp['c_sample'] = nrm(ks[3], (DEC_BATCH, D_MODEL), 1.0)
    inp['state_conv'] = nrm(ks[4], (DEPTH, DEC_BATCH, CONV_W - 1, DN_CONV_CH), 1.0)
    inp['state_delta'] = nrm(ks[5], (DEPTH, DEC_BATCH, DN_HEADS, DN_KDIM, DN_VDIM), 0.05)
    inp['w_ada'] = nrm(ks[6], (DEPTH, D_MODEL, N_MOD * D_MODEL), 0.5 * D_MODEL ** -0.5)
    inp['b_ada'] = nrm(ks[7], (DEPTH, N_MOD * D_MODEL), 0.01)
    inp['norm1'] = 1.0 + nrm(ks[8], (DEPTH, D_MODEL), 0.02)
    inp['norm2'] = 1.0 + nrm(ks[9], (DEPTH, D_MODEL), 0.02)
    inp['w_in'] = nrm(ks[10], (DEPTH, D_MODEL, IN_WIDTH), D_MODEL ** -0.5)
    inp['conv_w'] = nrm(ks[11], (DEPTH, CONV_W, DN_CONV_CH), CONV_W ** -0.5)
    inp['a_log'] = jnp.log(jax.random.uniform(ks[12], (DEPTH, DN_HEADS), f32, 1.0, 16.0))
    dt = jnp.exp(jax.random.uniform(ks[13], (DEPTH, DN_HEADS), f32, math.log(1e-3), math.log(1e-1)))
    inp['dt_bias'] = dt + jnp.log(-jnp.expm1(-dt))
    inp['dn_norm'] = 1.0 + nrm(ks[14], (DEPTH, DN_VDIM), 0.02)
    inp['ln_v_w'] = 1.0 + nrm(ks[15], (DEPTH, A_WIDTH), 0.02)
    inp['ln_v_b'] = nrm(ks[16], (DEPTH, A_WIDTH), 0.02)
    inp['w_s'] = nrm(ks[17], (DEPTH, A_GROUPS, CHUNK, CHUNK), CHUNK ** -0.5)
    inp['b_s'] = 1.0 + nrm(ks[18], (DEPTH, A_GROUPS, CHUNK), 0.1)
    inp['w_pa'] = nrm(ks[19], (DEPTH, A_WIDTH, D_MODEL), A_WIDTH ** -0.5)
    inp['w_pb'] = nrm(ks[20], (DEPTH, DN_V, D_MODEL), DN_V ** -0.5)
    inp['w_o'] = nrm(ks[21], (DEPTH, D_MODEL, D_MODEL), D_MODEL ** -0.5)
    inp['w_rg'] = nrm(ks[22], (DEPTH, D_MODEL, N_GROUPS), D_MODEL ** -0.5)
    inp['b_rg'] = nrm(ks[23], (DEPTH, N_GROUPS), 0.01)
    inp['w_re'] = nrm(ks[24], (DEPTH, D_MODEL, N_EXPERTS), D_MODEL ** -0.5)
    inp['b_re'] = nrm(ks[25], (DEPTH, N_EXPERTS), 0.01)
    inp['w_e_gate'] = nrm(ks[26], (DEPTH, N_GROUPS, EXP_PER_GROUP, D_MODEL, D_EXPERT), D_MODEL ** -0.5)
    inp['w_e_up'] = nrm(ks[27], (DEPTH, N_GROUPS, EXP_PER_GROUP, D_MODEL, D_EXPERT), D_MODEL ** -0.5)
    inp['w_e_down'] = nrm(ks[28], (DEPTH, N_GROUPS, EXP_PER_GROUP, D_EXPERT, D_MODEL), D_EXPERT ** -0.5)
    inp['final_norm'] = 1.0 + nrm(ks[29], (D_MODEL,), 0.02)
    return inp


def reference(x_prompt, x_sample, c_prompt, c_sample, state_conv, state_delta, w_ada, b_ada, norm1, norm2,
              w_in, conv_w, a_log, dt_bias, dn_norm, ln_v_w, ln_v_b, w_s, b_s, w_pa, w_pb, w_o,
              w_rg, b_rg, w_re, b_re, w_e_gate, w_e_up, w_e_down, final_norm):
    xp, xs = x_prompt, x_sample
    bp = x_prompt.shape[0]
    zero_conv = jnp.zeros((bp, CONV_W - 1, DN_CONV_CH), x_prompt.dtype)
    zero_state = jnp.zeros((bp, DN_HEADS, DN_KDIM, DN_VDIM), x_prompt.dtype)
    conv_p, conv_s, delta_p, delta_s, vrows_s = [], [], [], [], []
    for l in range(DEPTH):
        lw = [p[l] for p in (w_ada, b_ada, norm1, norm2, w_in, conv_w, a_log, dt_bias, dn_norm, ln_v_w, ln_v_b,
                             w_s, b_s, w_pa, w_pb, w_o, w_rg, b_rg, w_re, b_re, w_e_gate, w_e_up, w_e_down)]
        xp, cp, sp, _ = decoder_layer(xp, c_prompt, zero_conv, zero_state, *lw)
        xs, cs, ss, vs = decoder_layer(xs, c_sample, state_conv[l], state_delta[l], *lw)
        conv_p.append(cp)
        conv_s.append(cs)
        delta_p.append(sp)
        delta_s.append(ss)
        vrows_s.append(vs)
    y_prompt = rmsnorm(xp, final_norm)
    y_sample = rmsnorm(xs, final_norm)
    conv_prompt = jnp.stack(conv_p)
    conv_sample = jnp.stack(conv_s)
    delta_prompt = jnp.stack(delta_p)
    delta_sample = jnp.stack(delta_s)
    chunk_v_sample = jnp.stack(vrows_s)
    return (y_prompt, y_sample, conv_prompt, conv_sample, delta_prompt, delta_sample, chunk_v_sample)
```

```python
import functools

import jax
import jax.numpy as jnp
from jax import lax
from jax.experimental import pallas as pl
from jax.experimental.pallas import tpu as pltpu

F32 = jnp.float32
BF16 = jnp.bfloat16

D_MODEL = 1024
A_WIDTH = 1024
A_GROUPS = 4
A_GDIM = A_WIDTH // A_GROUPS
GMLP_CHUNK = 128
DN_HEADS = 8
DN_KDIM = 128
DN_VDIM = 128
DN_QK = DN_HEADS * DN_KDIM
DN_V = DN_HEADS * DN_VDIM
DN_CONV_CH = 2 * DN_QK + DN_V
CONV_W = 4
DN_CHUNK = 64
N_GROUPS = 4
EXP_PER_GROUP = 8
N_EXPERTS = N_GROUPS * EXP_PER_GROUP
D_EXPERT = 256
N_MOD = 6
EPS = 1e-6

LANES = 128
SUBLANES = 8
N_MAIN_BLOCKS = 8
ROUTER_LANE0 = N_GROUPS
VMEM_LIMIT = 56 * 1024 * 1024
ROW_TILE = 512


def _sigmoid(x):
    return 1.0 / (1.0 + jnp.exp(-x))


def _gelu(x):
    return 0.5 * x * (1.0 + lax.erf(x * 0.7071067811865476))


def _softplus(x):
    return jnp.maximum(x, 0.0) + jnp.log(1.0 + jnp.exp(-jnp.abs(x)))


def _dot(a, b):
    return jnp.dot(a.astype(BF16), b.astype(BF16), preferred_element_type=F32)


def _dot_nt(a, b):
    return lax.dot_general(a.astype(BF16), b.astype(BF16), (((1,), (1,)), ((), ())), preferred_element_type=F32)


def _dot_tn(a, b):
    return lax.dot_general(a.astype(BF16), b.astype(BF16), (((0,), (0,)), ((), ())), preferred_element_type=F32)


def _split3(x):
    hi = x.astype(BF16)
    r1 = x - hi.astype(F32)
    mid = r1.astype(BF16)
    lo = (r1 - mid.astype(F32)).astype(BF16)
    return hi, mid, lo


def _dot_exact_lhs(a_bf16, x):
    hi, mid, lo = _split3(x)
    f = lambda p: jnp.dot(a_bf16, p, preferred_element_type=F32)
    return f(hi) + f(mid) + f(lo)


_PASSES = {"invp": 3, "invn": 3, "inv_hi": 1, "inv_kmax": 8, "sol": 1, "kk": 1, "ws": 1, "attn": 1, "state": 1}


def _hilo(x):
    hi = x.astype(BF16)
    lo = (x - hi.astype(F32)).astype(BF16)
    return hi, lo


def _mm(site, a, b, dims=(((1,), (0,)), ((), ()))):
    f = lambda p, q: lax.dot_general(p, q, dims, preferred_element_type=F32)
    if _PASSES[site] == 1:
        return f(a.astype(BF16), b.astype(BF16))
    ah, al = _hilo(a)
    bh, bl = _hilo(b)
    return f(ah, bh) + (f(ah, bl) + f(al, bh))


_NT = (((1,), (1,)), ((), ()))
_TN = (((0,), (0,)), ((), ()))

def _params(sem):
    return pltpu.CompilerParams(dimension_semantics=sem, vmem_limit_bytes=VMEM_LIMIT)


def _ada_kernel(c_ref, w_ref, b_ref, o_ref):
    c = c_ref[...]
    o_ref[...] = _dot(c * _sigmoid(c), w_ref[...]) + b_ref[...]


def _ada(c_all, w_ada, b_ada):
    n = c_all.shape[0]
    width = w_ada.shape[1]
    bn = 512
    return pl.pallas_call(
        _ada_kernel,
        grid=(width // bn,),
        in_specs=[pl.BlockSpec((n, D_MODEL), lambda j: (0, 0)),
                  pl.BlockSpec((D_MODEL, bn), lambda j: (0, j)),
                  pl.BlockSpec((1, bn), lambda j: (0, j))],
        out_specs=pl.BlockSpec((n, bn), lambda j: (0, j)),
        out_shape=jax.ShapeDtypeStruct((n, width), F32),
        compiler_params=_params(("arbitrary",)),
        name="ada_mod",
    )(c_all, w_ada, b_ada.reshape(1, width))


def _inproj_kernel(x_ref, mod_ref, n1_ref, w_ref, wbg_ref, lnw_ref, lnb_ref, adt_ref, o_ref, bg_ref, h_s):
    j = pl.program_id(1)
    tm = h_s.shape[0]

    @pl.when(j == 0)
    def _():
        x = x_ref[...]
        y = x * lax.rsqrt(jnp.mean(x * x, axis=-1, keepdims=True) + EPS) * n1_ref[...]
        mod = mod_ref[...]
        h = y * (1.0 + mod[:, 1:2, :]) + mod[:, 0:1, :]
        hb = h.reshape(tm, D_MODEL).astype(BF16)
        h_s[...] = hb
        zbg = jnp.dot(hb, wbg_ref[...], preferred_element_type=F32)
        lane = lax.broadcasted_iota(jnp.int32, zbg.shape, 1)
        adt = adt_ref[...]
        g = -jnp.exp(adt[0:1, :]) * _softplus(zbg + adt[1:2, :])
        bg_ref[...] = jnp.where(lane < DN_HEADS, _sigmoid(zbg), g)

    z = jnp.dot(h_s[...], w_ref[...], preferred_element_type=F32)

    @pl.when(j == 0)
    def _():
        o_ref[...] = _gelu(z)

    @pl.when(j == 1)
    def _():
        a = _gelu(z)
        mu = jnp.mean(a, axis=-1, keepdims=True)
        ac = a - mu
        var = jnp.mean(ac * ac, axis=-1, keepdims=True)
        o_ref[...] = ac * lax.rsqrt(var + EPS) * lnw_ref[...] + lnb_ref[...]

    @pl.when((j >= 2) & (j <= 4))
    def _():
        o_ref[...] = z

    @pl.when(j == 5)
    def _():
        o_ref[...] = z * _sigmoid(z)

    @pl.when(j >= 6)
    def _():
        o_ref[...] = _sigmoid(z)


def _inproj(x, mod, norm1, w_main, w_bg, ln_w, ln_b, adt, bb, tt):
    B, T, _ = x.shape
    tm = bb * tt
    tpb = T // tt
    n_tiles = (B // bb) * tpb
    M = B * T
    xmap = lambda i, j: (i // tpb, i % tpb, 0)
    return pl.pallas_call(
        _inproj_kernel,
        grid=(n_tiles, N_MAIN_BLOCKS),
        in_specs=[pl.BlockSpec((bb, tt, D_MODEL), xmap),
                  pl.BlockSpec((bb, N_MOD, D_MODEL), lambda i, j: (i // tpb, 0, 0)),
                  pl.BlockSpec((1, D_MODEL), lambda i, j: (0, 0)),
                  pl.BlockSpec((D_MODEL, 1024), lambda i, j: (0, j)),
                  pl.BlockSpec((D_MODEL, LANES), lambda i, j: (0, 0)),
                  pl.BlockSpec((1, A_WIDTH), lambda i, j: (0, 0)),
                  pl.BlockSpec((1, A_WIDTH), lambda i, j: (0, 0)),
                  pl.BlockSpec((2, LANES), lambda i, j: (0, 0))],
        out_specs=[pl.BlockSpec((tm, 1024), lambda i, j: (i, j)),
                   pl.BlockSpec((tm, LANES), lambda i, j: (i, 0))],
        out_shape=[jax.ShapeDtypeStruct((M, N_MAIN_BLOCKS * 1024), F32),
                   jax.ShapeDtypeStruct((M, LANES), F32)],
        scratch_shapes=[pltpu.VMEM((tm, D_MODEL), BF16)],
        compiler_params=_params(("arbitrary", "arbitrary")),
        name="in_proj",
    )(x, mod, norm1.reshape(1, D_MODEL), w_main, w_bg, ln_w.reshape(1, A_WIDTH), ln_b.reshape(1, A_WIDTH), adt)


def _inv_unit_lower_minus_eye(lmat, nilpotent):
    n = -lmat
    p = lmat
    k = 2
    while k < nilpotent:
        early = k <= _PASSES["inv_kmax"]
        p = _mm("invp" if early else "inv_hi", p, p)
        n = n + p + _mm("invn" if early else "inv_hi", n, p)
        k *= 2
    return n


def _deltanet_kernel(q_ref, k_ref, v_ref, z_ref, bg_ref, cinit_ref, s0_ref, cw_ref, dnn_ref,
                     bo_ref, sn_ref, xp_s, s_s, *, nseq, C):
    R = nseq * C
    c = pl.program_id(1)
    nc = pl.num_programs(1)

    @pl.when(c == 0)
    def _():
        xp_s[:, 0:SUBLANES, :] = cinit_ref[...]
        s_s[...] = s0_ref[...]

    @pl.when(c > 0)
    def _():
        xp_s[:, 0:SUBLANES, :] = xp_s[:, C:C + SUBLANES, :]

    for s in range(nseq):
        xp_s[s, SUBLANES:SUBLANES + C, 0:DN_QK] = q_ref[s * C:(s + 1) * C, :]
        xp_s[s, SUBLANES:SUBLANES + C, DN_QK:2 * DN_QK] = k_ref[s * C:(s + 1) * C, :]
        xp_s[s, SUBLANES:SUBLANES + C, 2 * DN_QK:DN_CONV_CH] = v_ref[s * C:(s + 1) * C, :]

    cw = cw_ref[...]
    base = SUBLANES - (CONV_W - 1)
    acc = None
    for j in range(CONV_W):
        term = xp_s[:, base + j:base + j + C, :] * cw[j:j + 1, :]
        acc = term if acc is None else acc + term
    qkv = acc.reshape(R, DN_CONV_CH)
    qkv = qkv * _sigmoid(qkv)

    bg = bg_ref[...]
    row = lax.broadcasted_iota(jnp.int32, (R, R), 0)
    col = lax.broadcasted_iota(jnp.int32, (R, R), 1)
    if nseq > 1:
        shift = C.bit_length() - 1
        same = lax.shift_right_logical(row, shift) == lax.shift_right_logical(col, shift)
        tril = same & (col <= row)
        strict = same & (col < row)
    else:
        tril = col <= row
        strict = col < row
    gc = _dot_exact_lhs(jnp.where(tril, 1.0, 0.0).astype(BF16), bg)
    gct = gc.T
    dnn = dnn_ref[...]
    zg = z_ref[...]

    for h in range(DN_HEADS):
        hs = slice(h * DN_KDIM, (h + 1) * DN_KDIM)
        qh = qkv[:, h * DN_KDIM:(h + 1) * DN_KDIM]
        kh = qkv[:, DN_QK + h * DN_KDIM:DN_QK + (h + 1) * DN_KDIM]
        vh = qkv[:, 2 * DN_QK + h * DN_VDIM:2 * DN_QK + (h + 1) * DN_VDIM]
        qn = qh * lax.rsqrt(jnp.sum(qh * qh, axis=-1, keepdims=True) + EPS) * (DN_KDIM ** -0.5)
        kn = kh * lax.rsqrt(jnp.sum(kh * kh, axis=-1, keepdims=True) + EPS)
        beta = bg[:, h:h + 1]
        gcol = gc[:, DN_HEADS + h:DN_HEADS + h + 1]
        grow = gct[DN_HEADS + h:DN_HEADS + h + 1, :]
        decay = jnp.exp(jnp.minimum(gcol - grow, 0.0))
        eg = jnp.exp(gcol)
        kb = kn * beta
        vb = vh * beta
        a2 = _mm("kk", jnp.concatenate([kb, qn], axis=0), kn, _NT)
        lmat = jnp.where(strict, a2[:R] * decay, 0.0)
        attn = jnp.where(tril, a2[R:] * decay, 0.0)
        tn = _inv_unit_lower_minus_eye(lmat, C)
        rhs = jnp.concatenate([vb, kb * eg], axis=1)
        sol = rhs + _mm("sol", tn, rhs)
        u = sol[:, :DN_VDIM]
        w = sol[:, DN_VDIM:]
        qg = qn * eg
        vnew_parts = []
        qs_parts = []
        for s in range(nseq):
            rs = slice(s * C, (s + 1) * C)
            st = s_s[s, h]
            ws = _mm("ws", jnp.concatenate([w[rs], qg[rs]], axis=0), st)
            vnew_s = u[rs] - ws[:C]
            vnew_parts.append(vnew_s)
            qs_parts.append(ws[C:])
            glast = gcol[(s + 1) * C - 1:(s + 1) * C, :]
            kd = kn[rs] * jnp.exp(glast - gcol[rs])
            s_s[s, h] = st * jnp.exp(glast) + _mm("state", kd, vnew_s, _TN)
        vnew = vnew_parts[0] if nseq == 1 else jnp.concatenate(vnew_parts, axis=0)
        qs_all = qs_parts[0] if nseq == 1 else jnp.concatenate(qs_parts, axis=0)
        o = qs_all + _mm("attn", attn, vnew)
        on = o * lax.rsqrt(jnp.mean(o * o, axis=-1, keepdims=True) + EPS) * dnn
        bo_ref[:, hs] = on * zg[:, hs]

    @pl.when(c == nc - 1)
    def _():
        sn_ref[...] = s_s[...]


def _deltanet(main, bg, conv_init, s0, conv_w, dn_norm, nseq, C, nc):
    M = main.shape[0]
    R = nseq * C
    nb = M // (R * nc)
    rmap = lambda col: (lambda b, c: (b * nc + c, col))
    kern = functools.partial(_deltanet_kernel, nseq=nseq, C=C)
    return pl.pallas_call(
        kern,
        grid=(nb, nc),
        in_specs=[pl.BlockSpec((R, 1024), rmap(2)),
                  pl.BlockSpec((R, 1024), rmap(3)),
                  pl.BlockSpec((R, 1024), rmap(4)),
                  pl.BlockSpec((R, 1024), rmap(5)),
                  pl.BlockSpec((R, LANES), rmap(0)),
                  pl.BlockSpec((nseq, SUBLANES, DN_CONV_CH), lambda b, c: (b, 0, 0)),
                  pl.BlockSpec((nseq, DN_HEADS, DN_KDIM, DN_VDIM), lambda b, c: (b, 0, 0, 0)),
                  pl.BlockSpec((CONV_W, DN_CONV_CH), lambda b, c: (0, 0)),
                  pl.BlockSpec((1, DN_VDIM), lambda b, c: (0, 0))],
        out_specs=[pl.BlockSpec((R, DN_V), rmap(0)),
                   pl.BlockSpec((nseq, DN_HEADS, DN_KDIM, DN_VDIM), lambda b, c: (b, 0, 0, 0))],
        out_shape=[jax.ShapeDtypeStruct((M, DN_V), F32),
                   jax.ShapeDtypeStruct(s0.shape, F32)],
        scratch_shapes=[pltpu.VMEM((nseq, C + SUBLANES, DN_CONV_CH), F32),
                        pltpu.VMEM((nseq, DN_HEADS, DN_KDIM, DN_VDIM), F32)],
        compiler_params=_params(("arbitrary", "arbitrary")),
        name="deltanet",
    )(main, main, main, main, bg, conv_init, s0, conv_w, dn_norm.reshape(1, DN_VDIM))


def _outproj_kernel(u_ref, vn_ref, ga_ref, gb_ref, bo_ref, x_ref, mod_ref, ws_ref, bs_ref, wpa_ref, wpb_ref,
                    wo_ref, n2_ref, wr_ref, br_ref, x1_ref, h2_ref, cmb_ref):
    tm = u_ref.shape[0]
    vn = vn_ref[...].astype(BF16)
    parts = []
    for r in range(tm // GMLP_CHUNK):
        rs = slice(r * GMLP_CHUNK, (r + 1) * GMLP_CHUNK)
        row_parts = []
        for g in range(A_GROUPS):
            gs = slice(g * A_GDIM, (g + 1) * A_GDIM)
            row_parts.append(jnp.dot(ws_ref[g], vn[rs, gs], preferred_element_type=F32))
        parts.append(jnp.concatenate(row_parts, axis=1) + bs_ref[...])
    sv = parts[0] if len(parts) == 1 else jnp.concatenate(parts, axis=0)
    a_out = u_ref[...] * sv
    pa = _dot(a_out, wpa_ref[...])
    pb = _dot(bo_ref[...], wpb_ref[...])
    m = ga_ref[...] * pa + gb_ref[...] * pb
    mix = _dot(m, wo_ref[...])
    mod = mod_ref[...]
    x = x_ref[...]
    x1 = x + mod[:, 2:3, :] * mix.reshape(x.shape)
    x1_ref[...] = x1
    y = x1 * lax.rsqrt(jnp.mean(x1 * x1, axis=-1, keepdims=True) + EPS) * n2_ref[...]
    h2 = (y * (1.0 + mod[:, 4:5, :]) + mod[:, 3:4, :]).reshape(tm, D_MODEL)
    h2_ref[...] = h2.astype(BF16)

    logits = jnp.dot(h2, wr_ref[...], preferred_element_type=F32, precision=lax.Precision.HIGHEST) + br_ref[...]
    lane = lax.broadcasted_iota(jnp.int32, logits.shape, 1)
    lanef = lane.astype(F32)
    neg = jnp.float32(-jnp.inf)
    big = jnp.float32(1e9)
    gl = jnp.where(lane < N_GROUPS, logits, neg)
    gmax = jnp.max(gl, axis=-1, keepdims=True)
    gsel = jnp.min(jnp.where(gl == gmax, lanef, big), axis=-1, keepdims=True)
    gp = 1.0 / jnp.sum(jnp.exp(gl - gmax), axis=-1, keepdims=True)
    lo = ROUTER_LANE0 + EXP_PER_GROUP * gsel
    in_grp = (lanef >= lo) & (lanef < lo + EXP_PER_GROUP)
    el = jnp.where(in_grp, logits, neg)
    m1 = jnp.max(el, axis=-1, keepdims=True)
    i1 = jnp.min(jnp.where(el == m1, lanef, big), axis=-1, keepdims=True)
    el2 = jnp.where(lanef == i1, neg, el)
    m2 = jnp.max(el2, axis=-1, keepdims=True)
    i2 = jnp.min(jnp.where(el2 == m2, lanef, big), axis=-1, keepdims=True)
    e2 = jnp.exp(m2 - m1)
    w1 = gp / (1.0 + e2)
    w2 = gp * e2 / (1.0 + e2)
    cmb_ref[...] = jnp.where(lanef == i1, w1, jnp.where(lanef == i2, w2, 0.0))


def _outproj(main, bo, x, mod, ws_eff, bs_full, w_pa, w_pb, w_o, norm2, w_r, b_r, bb, tt):
    B, T, _ = x.shape
    tm = bb * tt
    tpb = T // tt
    n_tiles = (B // bb) * tpb
    M = B * T
    cmap = lambda col: (lambda i: (i, col))
    full2 = lambda i: (0, 0)
    return pl.pallas_call(
        _outproj_kernel,
        grid=(n_tiles,),
        in_specs=[pl.BlockSpec((tm, 1024), cmap(0)),
                  pl.BlockSpec((tm, 1024), cmap(1)),
                  pl.BlockSpec((tm, 1024), cmap(6)),
                  pl.BlockSpec((tm, 1024), cmap(7)),
                  pl.BlockSpec((tm, DN_V), cmap(0)),
                  pl.BlockSpec((bb, tt, D_MODEL), lambda i: (i // tpb, i % tpb, 0)),
                  pl.BlockSpec((bb, N_MOD, D_MODEL), lambda i: (i // tpb, 0, 0)),
                  pl.BlockSpec((A_GROUPS, GMLP_CHUNK, GMLP_CHUNK), lambda i: (0, 0, 0)),
                  pl.BlockSpec((GMLP_CHUNK, A_WIDTH), full2),
                  pl.BlockSpec((A_WIDTH, D_MODEL), full2),
                  pl.BlockSpec((DN_V, D_MODEL), full2),
                  pl.BlockSpec((D_MODEL, D_MODEL), full2),
                  pl.BlockSpec((1, D_MODEL), full2),
                  pl.BlockSpec((D_MODEL, LANES), full2),
                  pl.BlockSpec((1, LANES), full2)],
        out_specs=[pl.BlockSpec((bb, tt, D_MODEL), lambda i: (i // tpb, i % tpb, 0)),
                   pl.BlockSpec((tm, D_MODEL), cmap(0)),
                   pl.BlockSpec((tm, LANES), cmap(0))],
        out_shape=[jax.ShapeDtypeStruct((B, T, D_MODEL), F32),
                   jax.ShapeDtypeStruct((M, D_MODEL), BF16),
                   jax.ShapeDtypeStruct((M, LANES), F32)],
        compiler_params=_params(("arbitrary",)),
        name="out_proj",
    )(main, main, main, main, bo, x, mod, ws_eff, bs_full, w_pa, w_pb, w_o, norm2.reshape(1, D_MODEL), w_r, b_r)


def _moe_kernel(h_ref, cmb_ref, wg_ref, wu_ref, wd_ref, x1_ref, mod_ref, fn_ref, y_ref, acc_s):
    e = pl.program_id(1)
    ne = pl.num_programs(1)

    @pl.when(e == 0)
    def _():
        acc_s[...] = jnp.zeros_like(acc_s)

    h = h_ref[...]
    a = jnp.dot(h, wg_ref[0], preferred_element_type=F32)
    u = jnp.dot(h, wu_ref[0], preferred_element_type=F32)
    cmb = cmb_ref[...]
    lane = lax.broadcasted_iota(jnp.int32, cmb.shape, 1)
    ce = jnp.sum(jnp.where(lane == e + ROUTER_LANE0, cmb, 0.0), axis=-1, keepdims=True)
    act = a * _sigmoid(a) * u * ce
    acc_s[...] += _dot(act, wd_ref[0])

    @pl.when(e == ne - 1)
    def _():
        mod = mod_ref[...]
        x1 = x1_ref[...]
        x2 = x1 + mod[:, 5:6, :] * acc_s[...].reshape(x1.shape)
        y_ref[...] = x2 * lax.rsqrt(jnp.mean(x2 * x2, axis=-1, keepdims=True) + EPS) * fn_ref[...]


def _moe(h2, cmb, wg, wu, wd, x1, mod, final_norm, bb, tt):
    B, T, _ = x1.shape
    tm = bb * tt
    tpb = T // tt
    n_tiles = (B // bb) * tpb
    xmap = lambda i, e: (i // tpb, i % tpb, 0)
    return pl.pallas_call(
        _moe_kernel,
        grid=(n_tiles, N_EXPERTS),
        in_specs=[pl.BlockSpec((tm, D_MODEL), lambda i, e: (i, 0)),
                  pl.BlockSpec((tm, LANES), lambda i, e: (i, 0)),
                  pl.BlockSpec((1, D_MODEL, D_EXPERT), lambda i, e: (e, 0, 0)),
                  pl.BlockSpec((1, D_MODEL, D_EXPERT), lambda i, e: (e, 0, 0)),
                  pl.BlockSpec((1, D_EXPERT, D_MODEL), lambda i, e: (e, 0, 0)),
                  pl.BlockSpec((bb, tt, D_MODEL), xmap),
                  pl.BlockSpec((bb, N_MOD, D_MODEL), lambda i, e: (i // tpb, 0, 0)),
                  pl.BlockSpec((1, D_MODEL), lambda i, e: (0, 0))],
        out_specs=pl.BlockSpec((bb, tt, D_MODEL), xmap),
        out_shape=jax.ShapeDtypeStruct((B, T, D_MODEL), F32),
        scratch_shapes=[pltpu.VMEM((tm, D_MODEL), F32)],
        compiler_params=_params(("arbitrary", "arbitrary")),
        name="moe",
    )(h2, cmb, wg, wu, wd, x1, mod, final_norm.reshape(1, D_MODEL))


def _layer(x, mod, conv_init, s0, prm, final_norm, *, bb, tt, dn_nseq, dn_chunk):
    B, T, _ = x.shape
    main, bg = _inproj(x, mod, prm["norm1"], prm["w_main"], prm["w_bg"], prm["ln_v_w"], prm["ln_v_b"], prm["adt"],
                       bb, tt)
    nc = T // dn_chunk
    bo, s_new = _deltanet(main, bg, conv_init, s0, prm["conv_w"], prm["dn_norm"], dn_nseq, dn_chunk, nc)
    L = min(GMLP_CHUNK, T)
    reps = GMLP_CHUNK // L
    wm = prm["w_s_tril"][:, :L, :L]
    if reps > 1:
        blk = (jnp.arange(GMLP_CHUNK)[:, None] // L) == (jnp.arange(GMLP_CHUNK)[None, :] // L)
        wm = jnp.where(blk[None], jnp.tile(wm, (1, reps, reps)), 0.0)
    ws_eff = wm.astype(BF16)
    bs_rows = jnp.tile(jnp.transpose(prm["b_s"][:, :L]), (reps, 1))
    bs_full = jnp.repeat(bs_rows, A_GDIM, axis=1)
    x1, h2, cmb = _outproj(main, bo, x, mod, ws_eff, bs_full, prm["w_pa"], prm["w_pb"], prm["w_o"], prm["norm2"],
                           prm["w_r"], prm["b_r"], bb, tt)
    y = _moe(h2, cmb, prm["w_e_gate"], prm["w_e_up"], prm["w_e_down"], x1, mod, final_norm, bb, tt)
    zqkv = main[:, 2 * 1024:5 * 1024].reshape(B, T, DN_CONV_CH)
    conv_new = jnp.concatenate([conv_init[:, SUBLANES - (CONV_W - 1):, :], zqkv], axis=1)[:, -(CONV_W - 1):, :]
    vn = main[:, 1024:2048].reshape(B, T, A_WIDTH)
    return y, conv_new, s_new, vn


def kernel(x_prompt, x_sample, c_prompt, c_sample, state_conv, state_delta, w_ada, b_ada, norm1, norm2, w_in, conv_w, a_log, dt_bias, dn_norm, ln_v_w, ln_v_b, w_s, b_s, w_pa, w_pb, w_o, w_rg, b_rg, w_re, b_re, w_e_gate, w_e_up, w_e_down, final_norm):
    depth = w_ada.shape[0]
    assert depth == 1
    bp, tp, _ = x_prompt.shape
    bs_, ts, _ = x_sample.shape
    l = 0
    wi = w_in[l]
    o_qkv = 2 * A_WIDTH
    o_z = o_qkv + DN_CONV_CH
    o_b = o_z + DN_V
    o_ga = o_b + 2 * DN_HEADS
    w_main = jnp.concatenate([wi[:, :o_b], wi[:, o_ga:]], axis=1).astype(BF16)
    w_bg = jnp.pad(wi[:, o_b:o_ga], ((0, 0), (0, LANES - 2 * DN_HEADS))).astype(BF16)
    adt = jnp.stack([jnp.pad(a_log[l], (DN_HEADS, LANES - 2 * DN_HEADS)),
                     jnp.pad(dt_bias[l], (DN_HEADS, LANES - 2 * DN_HEADS))])
    w_r = jnp.pad(jnp.concatenate([w_rg[l], w_re[l]], axis=1), ((0, 0), (0, LANES - N_GROUPS - N_EXPERTS)))
    b_r = jnp.pad(jnp.concatenate([b_rg[l], b_re[l]]), (0, LANES - N_GROUPS - N_EXPERTS)).reshape(1, LANES)
    tri = jnp.tril(jnp.ones((GMLP_CHUNK, GMLP_CHUNK), F32))
    prm = dict(
        norm1=norm1[l], norm2=norm2[l], w_main=w_main, w_bg=w_bg, adt=adt, ln_v_w=ln_v_w[l], ln_v_b=ln_v_b[l],
        conv_w=conv_w[l], dn_norm=dn_norm[l], w_s_tril=w_s[l] * tri, b_s=b_s[l],
        w_pa=w_pa[l].astype(BF16), w_pb=w_pb[l].astype(BF16), w_o=w_o[l].astype(BF16), w_r=w_r, b_r=b_r,
        w_e_gate=w_e_gate[l].reshape(N_EXPERTS, D_MODEL, D_EXPERT).astype(BF16),
        w_e_up=w_e_up[l].reshape(N_EXPERTS, D_MODEL, D_EXPERT).astype(BF16),
        w_e_down=w_e_down[l].reshape(N_EXPERTS, D_EXPERT, D_MODEL).astype(BF16),
    )

    c_all = jnp.concatenate([c_prompt, c_sample], axis=0)
    mod = _ada(c_all, w_ada[l], b_ada[l]).reshape(bp + bs_, N_MOD, D_MODEL)
    mod_p, mod_s = mod[:bp], mod[bp:]

    pad_rows = SUBLANES - (CONV_W - 1)
    cinit_p = jnp.zeros((bp, SUBLANES, DN_CONV_CH), F32)
    cinit_s = jnp.pad(state_conv[l], ((0, 0), (pad_rows, 0), (0, 0)))
    s0_p = jnp.zeros((bp, DN_HEADS, DN_KDIM, DN_VDIM), F32)

    yp, conv_p, delta_p, _ = _layer(x_prompt, mod_p, cinit_p, s0_p, prm, final_norm,
                                    bb=1, tt=min(ROW_TILE, tp), dn_nseq=1, dn_chunk=min(DN_CHUNK, tp))
    ys, conv_s, delta_s, vn_s = _layer(x_sample, mod_s, cinit_s, state_delta[l], prm, final_norm,
                                       bb=min(bs_, ROW_TILE // ts), tt=ts,
                                       dn_nseq=min(bs_, DN_CHUNK // ts), dn_chunk=ts)
    return (yp, ys, conv_p[None], conv_s[None], delta_p[None], delta_s[None], vn_s[None])
```

```python
import functools

import jax
import jax.numpy as jnp
from jax import lax
from jax.experimental import pallas as pl
from jax.experimental.pallas import tpu as pltpu

F32 = jnp.float32
BF16 = jnp.bfloat16

D_MODEL = 1024
A_WIDTH = 1024
A_GROUPS = 4
A_GDIM = A_WIDTH // A_GROUPS
GMLP_CHUNK = 128
DN_HEADS = 8
DN_KDIM = 128
DN_VDIM = 128
DN_QK = DN_HEADS * DN_KDIM
DN_V = DN_HEADS * DN_VDIM
DN_CONV_CH = 2 * DN_QK + DN_V
CONV_W = 4
DN_CHUNK = 64
N_GROUPS = 4
EXP_PER_GROUP = 8
N_EXPERTS = N_GROUPS * EXP_PER_GROUP
D_EXPERT = 256
N_MOD = 6
EPS = 1e-6

LANES = 128
SUBLANES = 8
N_MAIN_BLOCKS = 8
ROUTER_LANE0 = N_GROUPS
VMEM_LIMIT = 56 * 1024 * 1024
ROW_TILE = 512


def _sigmoid(x):
    return 1.0 / (1.0 + jnp.exp(-x))


def _gelu(x):
    return 0.5 * x * (1.0 + lax.erf(x * 0.7071067811865476))


def _softplus(x):
    return jnp.maximum(x, 0.0) + jnp.log(1.0 + jnp.exp(-jnp.abs(x)))


def _dot(a, b):
    return jnp.dot(a.astype(BF16), b.astype(BF16), preferred_element_type=F32)


def _dot_nt(a, b):
    return lax.dot_general(a.astype(BF16), b.astype(BF16), (((1,), (1,)), ((), ())), preferred_element_type=F32)


def _dot_tn(a, b):
    return lax.dot_general(a.astype(BF16), b.astype(BF16), (((0,), (0,)), ((), ())), preferred_element_type=F32)


def _split3(x):
    hi = x.astype(BF16)
    r1 = x - hi.astype(F32)
    mid = r1.astype(BF16)
    lo = (r1 - mid.astype(F32)).astype(BF16)
    return hi, mid, lo


def _dot_exact_lhs(a_bf16, x):
    hi, mid, lo = _split3(x)
    f = lambda p: jnp.dot(a_bf16, p, preferred_element_type=F32)
    return f(hi) + f(mid) + f(lo)


_PASSES = {"invp": 3, "invn": 3, "inv_hi": 1, "inv_kmax": 8, "sol": 1, "kk": 1, "ws": 1, "attn": 1, "state": 1}


def _hilo(x):
    hi = x.astype(BF16)
    lo = (x - hi.astype(F32)).astype(BF16)
    return hi, lo


def _mm(site, a, b, dims=(((1,), (0,)), ((), ()))):
    f = lambda p, q: lax.dot_general(p, q, dims, preferred_element_type=F32)
    if _PASSES[site] == 1:
        return f(a.astype(BF16), b.astype(BF16))
    ah, al = _hilo(a)
    bh, bl = _hilo(b)
    return f(ah, bh) + (f(ah, bl) + f(al, bh))


_NT = (((1,), (1,)), ((), ()))
_TN = (((0,), (0,)), ((), ()))

def _params(sem):
    return pltpu.CompilerParams(dimension_semantics=sem, vmem_limit_bytes=VMEM_LIMIT)


def _ada_kernel(c_ref, w_ref, b_ref, o_ref):
    c = c_ref[...]
    o_ref[...] = _dot(c * _sigmoid(c), w_ref[...]) + b_ref[...]


def _ada(c_all, w_ada, b_ada):
    n = c_all.shape[0]
    width = w_ada.shape[1]
    bn = 512
    return pl.pallas_call(
        _ada_kernel,
        grid=(width // bn,),
        in_specs=[pl.BlockSpec((n, D_MODEL), lambda j: (0, 0)),
                  pl.BlockSpec((D_MODEL, bn), lambda j: (0, j)),
                  pl.BlockSpec((1, bn), lambda j: (0, j))],
        out_specs=pl.BlockSpec((n, bn), lambda j: (0, j)),
        out_shape=jax.ShapeDtypeStruct((n, width), F32),
        compiler_params=_params(("arbitrary",)),
        name="ada_mod",
    )(c_all, w_ada, b_ada.reshape(1, width))


def _inproj_kernel(x_ref, mod_ref, n1_ref, w_ref, wbg_ref, lnw_ref, lnb_ref, adt_ref, o_ref, bg_ref, h_s):
    j = pl.program_id(1)
    tm = h_s.shape[0]

    @pl.when(j == 0)
    def _():
        x = x_ref[...]
        y = x * lax.rsqrt(jnp.mean(x * x, axis=-1, keepdims=True) + EPS) * n1_ref[...]
        mod = mod_ref[...]
        h = y * (1.0 + mod[:, 1:2, :]) + mod[:, 0:1, :]
        hb = h.reshape(tm, D_MODEL).astype(BF16)
        h_s[...] = hb
        zbg = jnp.dot(hb, wbg_ref[...], preferred_element_type=F32)
        lane = lax.broadcasted_iota(jnp.int32, zbg.shape, 1)
        adt = adt_ref[...]
        g = -jnp.exp(adt[0:1, :]) * _softplus(zbg + adt[1:2, :])
        bg_ref[...] = jnp.where(lane < DN_HEADS, _sigmoid(zbg), g)

    z = jnp.dot(h_s[...], w_ref[...], preferred_element_type=F32)

    @pl.when(j == 0)
    def _():
        o_ref[...] = _gelu(z)

    @pl.when(j == 1)
    def _():
        a = _gelu(z)
        mu = jnp.mean(a, axis=-1, keepdims=True)
        ac = a - mu
        var = jnp.mean(ac * ac, axis=-1, keepdims=True)
        o_ref[...] = ac * lax.rsqrt(var + EPS) * lnw_ref[...] + lnb_ref[...]

    @pl.when((j >= 2) & (j <= 4))
    def _():
        o_ref[...] = z

    @pl.when(j == 5)
    def _():
        o_ref[...] = z * _sigmoid(z)

    @pl.when(j >= 6)
    def _():
        o_ref[...] = _sigmoid(z)


def _inproj(x, mod, norm1, w_main, w_bg, ln_w, ln_b, adt, bb, tt):
    B, T, _ = x.shape
    tm = bb * tt
    tpb = T // tt
    n_tiles = (B // bb) * tpb
    M = B * T
    xmap = lambda i, j: (i // tpb, i % tpb, 0)
    return pl.pallas_call(
        _inproj_kernel,
        grid=(n_tiles, N_MAIN_BLOCKS),
        in_specs=[pl.BlockSpec((bb, tt, D_MODEL), xmap),
                  pl.BlockSpec((bb, N_MOD, D_MODEL), lambda i, j: (i // tpb, 0, 0)),
                  pl.BlockSpec((1, D_MODEL), lambda i, j: (0, 0)),
                  pl.BlockSpec((D_MODEL, 1024), lambda i, j: (0, j)),
                  pl.BlockSpec((D_MODEL, LANES), lambda i, j: (0, 0)),
                  pl.BlockSpec((1, A_WIDTH), lambda i, j: (0, 0)),
                  pl.BlockSpec((1, A_WIDTH), lambda i, j: (0, 0)),
                  pl.BlockSpec((2, LANES), lambda i, j: (0, 0))],
        out_specs=[pl.BlockSpec((tm, 1024), lambda i, j: (i, j)),
                   pl.BlockSpec((tm, LANES), lambda i, j: (i, 0))],
        out_shape=[jax.ShapeDtypeStruct((M, N_MAIN_BLOCKS * 1024), F32),
                   jax.ShapeDtypeStruct((M, LANES), F32)],
        scratch_shapes=[pltpu.VMEM((tm, D_MODEL), BF16)],
        compiler_params=_params(("arbitrary", "arbitrary")),
        name="in_proj",
    )(x, mod, norm1.reshape(1, D_MODEL), w_main, w_bg, ln_w.reshape(1, A_WIDTH), ln_b.reshape(1, A_WIDTH), adt)


def _inv_unit_lower_minus_eye(lmats, nilpotent):
    ns = [-l for l in lmats]
    ps = list(lmats)
    k = 2
    while k < nilpotent:
        early = k <= _PASSES["inv_kmax"]
        ps = [_mm("invp" if early else "inv_hi", p, p) for p in ps]
        ns = [n + p + _mm("invn" if early else "inv_hi", n, p) for n, p in zip(ns, ps)]
        k *= 2
    return ns


def _deltanet_kernel(q_ref, k_ref, v_ref, z_ref, bg_ref, cinit_ref, s0_ref, cw_ref, dnn_ref,
                     bo_ref, sn_ref, xp_s, s_s, *, nseq, C):
    R = nseq * C
    c = pl.program_id(1)
    nc = pl.num_programs(1)

    @pl.when(c == 0)
    def _():
        xp_s[:, 0:SUBLANES, :] = cinit_ref[...]
        s_s[...] = s0_ref[...]

    @pl.when(c > 0)
    def _():
        xp_s[:, 0:SUBLANES, :] = xp_s[:, C:C + SUBLANES, :]

    for s in range(nseq):
        xp_s[s, SUBLANES:SUBLANES + C, 0:DN_QK] = q_ref[s * C:(s + 1) * C, :]
        xp_s[s, SUBLANES:SUBLANES + C, DN_QK:2 * DN_QK] = k_ref[s * C:(s + 1) * C, :]
        xp_s[s, SUBLANES:SUBLANES + C, 2 * DN_QK:DN_CONV_CH] = v_ref[s * C:(s + 1) * C, :]

    cw = cw_ref[...]
    base = SUBLANES - (CONV_W - 1)
    acc = None
    for j in range(CONV_W):
        term = xp_s[:, base + j:base + j + C, :] * cw[j:j + 1, :]
        acc = term if acc is None else acc + term
    qkv = acc.reshape(R, DN_CONV_CH)
    qkv = qkv * _sigmoid(qkv)

    bg = bg_ref[...]
    row = lax.broadcasted_iota(jnp.int32, (R, R), 0)
    col = lax.broadcasted_iota(jnp.int32, (R, R), 1)
    if nseq > 1:
        shift = C.bit_length() - 1
        same = lax.shift_right_logical(row, shift) == lax.shift_right_logical(col, shift)
        tril = same & (col <= row)
        strict = same & (col < row)
    else:
        tril = col <= row
        strict = col < row
    gc = _dot_exact_lhs(jnp.where(tril, 1.0, 0.0).astype(BF16), bg)
    gct = gc.T
    dnn = dnn_ref[...]
    zg = z_ref[...]

    HR = range(DN_HEADS)
    hsl = [slice(h * DN_KDIM, (h + 1) * DN_KDIM) for h in HR]
    qh = [qkv[:, h * DN_KDIM:(h + 1) * DN_KDIM] for h in HR]
    kh = [qkv[:, DN_QK + h * DN_KDIM:DN_QK + (h + 1) * DN_KDIM] for h in HR]
    vh = [qkv[:, 2 * DN_QK + h * DN_VDIM:2 * DN_QK + (h + 1) * DN_VDIM] for h in HR]
    qn = [x * lax.rsqrt(jnp.sum(x * x, axis=-1, keepdims=True) + EPS) * (DN_KDIM ** -0.5) for x in qh]
    kn = [x * lax.rsqrt(jnp.sum(x * x, axis=-1, keepdims=True) + EPS) for x in kh]
    beta = [bg[:, h:h + 1] for h in HR]
    gcol = [gc[:, DN_HEADS + h:DN_HEADS + h + 1] for h in HR]
    grow = [gct[DN_HEADS + h:DN_HEADS + h + 1, :] for h in HR]
    decay = [jnp.exp(jnp.minimum(gcol[h] - grow[h], 0.0)) for h in HR]
    eg = [jnp.exp(gcol[h]) for h in HR]
    kb = [kn[h] * beta[h] for h in HR]
    vb = [vh[h] * beta[h] for h in HR]
    a2 = [_mm("kk", jnp.concatenate([kb[h], qn[h]], axis=0), kn[h], _NT) for h in HR]
    lmat = [jnp.where(strict, a2[h][:R] * decay[h], 0.0) for h in HR]
    attn = [jnp.where(tril, a2[h][R:] * decay[h], 0.0) for h in HR]
    tn = _inv_unit_lower_minus_eye(lmat, C)
    rhs = [jnp.concatenate([vb[h], kb[h] * eg[h]], axis=1) for h in HR]
    sol = [rhs[h] + _mm("sol", tn[h], rhs[h]) for h in HR]
    u = [x[:, :DN_VDIM] for x in sol]
    w = [x[:, DN_VDIM:] for x in sol]
    qg = [qn[h] * eg[h] for h in HR]
    vnew = [[None] * nseq for _ in HR]
    qs = [[None] * nseq for _ in HR]
    for s in range(nseq):
        rs = slice(s * C, (s + 1) * C)
        st = [s_s[s, h] for h in HR]
        ws = [_mm("ws", jnp.concatenate([w[h][rs], qg[h][rs]], axis=0), st[h]) for h in HR]
        for h in HR:
            vnew[h][s] = u[h][rs] - ws[h][:C]
            qs[h][s] = ws[h][C:]
        glast = [gcol[h][(s + 1) * C - 1:(s + 1) * C, :] for h in HR]
        kd = [kn[h][rs] * jnp.exp(glast[h] - gcol[h][rs]) for h in HR]
        upd = [_mm("state", kd[h], vnew[h][s], _TN) for h in HR]
        for h in HR:
            s_s[s, h] = st[h] * jnp.exp(glast[h]) + upd[h]
    cat = lambda parts: parts[0] if nseq == 1 else jnp.concatenate(parts, axis=0)
    o = [cat(qs[h]) + _mm("attn", attn[h], cat(vnew[h])) for h in HR]
    for h in HR:
        on = o[h] * lax.rsqrt(jnp.mean(o[h] * o[h], axis=-1, keepdims=True) + EPS) * dnn
        bo_ref[:, hsl[h]] = on * zg[:, hsl[h]]

    @pl.when(c == nc - 1)
    def _():
        sn_ref[...] = s_s[...]


def _deltanet(main, bg, conv_init, s0, conv_w, dn_norm, nseq, C, nc):
    M = main.shape[0]
    R = nseq * C
    nb = M // (R * nc)
    rmap = lambda col: (lambda b, c: (b * nc + c, col))
    kern = functools.partial(_deltanet_kernel, nseq=nseq, C=C)
    return pl.pallas_call(
        kern,
        grid=(nb, nc),
        in_specs=[pl.BlockSpec((R, 1024), rmap(2)),
                  pl.BlockSpec((R, 1024), rmap(3)),
                  pl.BlockSpec((R, 1024), rmap(4)),
                  pl.BlockSpec((R, 1024), rmap(5)),
                  pl.BlockSpec((R, LANES), rmap(0)),
                  pl.BlockSpec((nseq, SUBLANES, DN_CONV_CH), lambda b, c: (b, 0, 0)),
                  pl.BlockSpec((nseq, DN_HEADS, DN_KDIM, DN_VDIM), lambda b, c: (b, 0, 0, 0)),
                  pl.BlockSpec((CONV_W, DN_CONV_CH), lambda b, c: (0, 0)),
                  pl.BlockSpec((1, DN_VDIM), lambda b, c: (0, 0))],
        out_specs=[pl.BlockSpec((R, DN_V), rmap(0)),
                   pl.BlockSpec((nseq, DN_HEADS, DN_KDIM, DN_VDIM), lambda b, c: (b, 0, 0, 0))],
        out_shape=[jax.ShapeDtypeStruct((M, DN_V), F32),
                   jax.ShapeDtypeStruct(s0.shape, F32)],
        scratch_shapes=[pltpu.VMEM((nseq, C + SUBLANES, DN_CONV_CH), F32),
                        pltpu.VMEM((nseq, DN_HEADS, DN_KDIM, DN_VDIM), F32)],
        compiler_params=_params(("arbitrary", "arbitrary")),
        name="deltanet",
    )(main, main, main, main, bg, conv_init, s0, conv_w, dn_norm.reshape(1, DN_VDIM))


def _outproj_kernel(u_ref, vn_ref, ga_ref, gb_ref, bo_ref, x_ref, mod_ref, ws_ref, bs_ref, wpa_ref, wpb_ref,
                    wo_ref, n2_ref, wr_ref, br_ref, x1_ref, h2_ref, cmb_ref):
    tm = u_ref.shape[0]
    vn = vn_ref[...].astype(BF16)
    parts = []
    for r in range(tm // GMLP_CHUNK):
        rs = slice(r * GMLP_CHUNK, (r + 1) * GMLP_CHUNK)
        row_parts = []
        for g in range(A_GROUPS):
            gs = slice(g * A_GDIM, (g + 1) * A_GDIM)
            row_parts.append(jnp.dot(ws_ref[g], vn[rs, gs], preferred_element_type=F32))
        parts.append(jnp.concatenate(row_parts, axis=1) + bs_ref[...])
    sv = parts[0] if len(parts) == 1 else jnp.concatenate(parts, axis=0)
    a_out = u_ref[...] * sv
    pa = _dot(a_out, wpa_ref[...])
    pb = _dot(bo_ref[...], wpb_ref[...])
    m = ga_ref[...] * pa + gb_ref[...] * pb
    mix = _dot(m, wo_ref[...])
    mod = mod_ref[...]
    x = x_ref[...]
    x1 = x + mod[:, 2:3, :] * mix.reshape(x.shape)
    x1_ref[...] = x1
    y = x1 * lax.rsqrt(jnp.mean(x1 * x1, axis=-1, keepdims=True) + EPS) * n2_ref[...]
    h2 = (y * (1.0 + mod[:, 4:5, :]) + mod[:, 3:4, :]).reshape(tm, D_MODEL)
    h2_ref[...] = h2.astype(BF16)

    logits = jnp.dot(h2, wr_ref[...], preferred_element_type=F32, precision=lax.Precision.HIGHEST) + br_ref[...]
    lane = lax.broadcasted_iota(jnp.int32, logits.shape, 1)
    lanef = lane.astype(F32)
    neg = jnp.float32(-jnp.inf)
    big = jnp.float32(1e9)
    gl = jnp.where(lane < N_GROUPS, logits, neg)
    gmax = jnp.max(gl, axis=-1, keepdims=True)
    gsel = jnp.min(jnp.where(gl == gmax, lanef, big), axis=-1, keepdims=True)
    gp = 1.0 / jnp.sum(jnp.exp(gl - gmax), axis=-1, keepdims=True)
    lo = ROUTER_LANE0 + EXP_PER_GROUP * gsel
    in_grp = (lanef >= lo) & (lanef < lo + EXP_PER_GROUP)
    el = jnp.where(in_grp, logits, neg)
    m1 = jnp.max(el, axis=-1, keepdims=True)
    i1 = jnp.min(jnp.where(el == m1, lanef, big), axis=-1, keepdims=True)
    el2 = jnp.where(lanef == i1, neg, el)
    m2 = jnp.max(el2, axis=-1, keepdims=True)
    i2 = jnp.min(jnp.where(el2 == m2, lanef, big), axis=-1, keepdims=True)
    e2 = jnp.exp(m2 - m1)
    w1 = gp / (1.0 + e2)
    w2 = gp * e2 / (1.0 + e2)
    cmb_ref[...] = jnp.where(lanef == i1, w1, jnp.where(lanef == i2, w2, 0.0))


def _outproj(main, bo, x, mod, ws_eff, bs_full, w_pa, w_pb, w_o, norm2, w_r, b_r, bb, tt):
    B, T, _ = x.shape
    tm = bb * tt
    tpb = T // tt
    n_tiles = (B // bb) * tpb
    M = B * T
    cmap = lambda col: (lambda i: (i, col))
    full2 = lambda i: (0, 0)
    return pl.pallas_call(
        _outproj_kernel,
        grid=(n_tiles,),
        in_specs=[pl.BlockSpec((tm, 1024), cmap(0)),
                  pl.BlockSpec((tm, 1024), cmap(1)),
                  pl.BlockSpec((tm, 1024), cmap(6)),
                  pl.BlockSpec((tm, 1024), cmap(7)),
                  pl.BlockSpec((tm, DN_V), cmap(0)),
                  pl.BlockSpec((bb, tt, D_MODEL), lambda i: (i // tpb, i % tpb, 0)),
                  pl.BlockSpec((bb, N_MOD, D_MODEL), lambda i: (i // tpb, 0, 0)),
                  pl.BlockSpec((A_GROUPS, GMLP_CHUNK, GMLP_CHUNK), lambda i: (0, 0, 0)),
                  pl.BlockSpec((GMLP_CHUNK, A_WIDTH), full2),
                  pl.BlockSpec((A_WIDTH, D_MODEL), full2),
                  pl.BlockSpec((DN_V, D_MODEL), full2),
                  pl.BlockSpec((D_MODEL, D_MODEL), full2),
                  pl.BlockSpec((1, D_MODEL), full2),
                  pl.BlockSpec((D_MODEL, LANES), full2),
                  pl.BlockSpec((1, LANES), full2)],
        out_specs=[pl.BlockSpec((bb, tt, D_MODEL), lambda i: (i // tpb, i % tpb, 0)),
                   pl.BlockSpec((tm, D_MODEL), cmap(0)),
                   pl.BlockSpec((tm, LANES), cmap(0))],
        out_shape=[jax.ShapeDtypeStruct((B, T, D_MODEL), F32),
                   jax.ShapeDtypeStruct((M, D_MODEL), BF16),
                   jax.ShapeDtypeStruct((M, LANES), F32)],
        compiler_params=_params(("arbitrary",)),
        name="out_proj",
    )(main, main, main, main, bo, x, mod, ws_eff, bs_full, w_pa, w_pb, w_o, norm2.reshape(1, D_MODEL), w_r, b_r)


def _moe_kernel(h_ref, cmb_ref, wg_ref, wu_ref, wd_ref, x1_ref, mod_ref, fn_ref, y_ref, acc_s):
    e = pl.program_id(1)
    ne = pl.num_programs(1)

    @pl.when(e == 0)
    def _():
        acc_s[...] = jnp.zeros_like(acc_s)

    h = h_ref[...]
    a = jnp.dot(h, wg_ref[0], preferred_element_type=F32)
    u = jnp.dot(h, wu_ref[0], preferred_element_type=F32)
    cmb = cmb_ref[...]
    lane = lax.broadcasted_iota(jnp.int32, cmb.shape, 1)
    ce = jnp.sum(jnp.where(lane == e + ROUTER_LANE0, cmb, 0.0), axis=-1, keepdims=True)
    act = a * _sigmoid(a) * u * ce
    acc_s[...] += _dot(act, wd_ref[0])

    @pl.when(e == ne - 1)
    def _():
        mod = mod_ref[...]
        x1 = x1_ref[...]
        x2 = x1 + mod[:, 5:6, :] * acc_s[...].reshape(x1.shape)
        y_ref[...] = x2 * lax.rsqrt(jnp.mean(x2 * x2, axis=-1, keepdims=True) + EPS) * fn_ref[...]


def _moe(h2, cmb, wg, wu, wd, x1, mod, final_norm, bb, tt):
    B, T, _ = x1.shape
    tm = bb * tt
    tpb = T // tt
    n_tiles = (B // bb) * tpb
    xmap = lambda i, e: (i // tpb, i % tpb, 0)
    return pl.pallas_call(
        _moe_kernel,
        grid=(n_tiles, N_EXPERTS),
        in_specs=[pl.BlockSpec((tm, D_MODEL), lambda i, e: (i, 0)),
                  pl.BlockSpec((tm, LANES), lambda i, e: (i, 0)),
                  pl.BlockSpec((1, D_MODEL, D_EXPERT), lambda i, e: (e, 0, 0)),
                  pl.BlockSpec((1, D_MODEL, D_EXPERT), lambda i, e: (e, 0, 0)),
                  pl.BlockSpec((1, D_EXPERT, D_MODEL), lambda i, e: (e, 0, 0)),
                  pl.BlockSpec((bb, tt, D_MODEL), xmap),
                  pl.BlockSpec((bb, N_MOD, D_MODEL), lambda i, e: (i // tpb, 0, 0)),
                  pl.BlockSpec((1, D_MODEL), lambda i, e: (0, 0))],
        out_specs=pl.BlockSpec((bb, tt, D_MODEL), xmap),
        out_shape=jax.ShapeDtypeStruct((B, T, D_MODEL), F32),
        scratch_shapes=[pltpu.VMEM((tm, D_MODEL), F32)],
        compiler_params=_params(("arbitrary", "arbitrary")),
        name="moe",
    )(h2, cmb, wg, wu, wd, x1, mod, final_norm.reshape(1, D_MODEL))


def _layer(x, mod, conv_init, s0, prm, final_norm, *, bb, tt, dn_nseq, dn_chunk):
    B, T, _ = x.shape
    main, bg = _inproj(x, mod, prm["norm1"], prm["w_main"], prm["w_bg"], prm["ln_v_w"], prm["ln_v_b"], prm["adt"],
                       bb, tt)
    nc = T // dn_chunk
    bo, s_new = _deltanet(main, bg, conv_init, s0, prm["conv_w"], prm["dn_norm"], dn_nseq, dn_chunk, nc)
    L = min(GMLP_CHUNK, T)
    reps = GMLP_CHUNK // L
    wm = prm["w_s_tril"][:, :L, :L]
    if reps > 1:
        blk = (jnp.arange(GMLP_CHUNK)[:, None] // L) == (jnp.arange(GMLP_CHUNK)[None, :] // L)
        wm = jnp.where(blk[None], jnp.tile(wm, (1, reps, reps)), 0.0)
    ws_eff = wm.astype(BF16)
    bs_rows = jnp.tile(jnp.transpose(prm["b_s"][:, :L]), (reps, 1))
    bs_full = jnp.repeat(bs_rows, A_GDIM, axis=1)
    x1, h2, cmb = _outproj(main, bo, x, mod, ws_eff, bs_full, prm["w_pa"], prm["w_pb"], prm["w_o"], prm["norm2"],
                           prm["w_r"], prm["b_r"], bb, tt)
    y = _moe(h2, cmb, prm["w_e_gate"], prm["w_e_up"], prm["w_e_down"], x1, mod, final_norm, bb, tt)
    main3 = main.reshape(B, T, N_MAIN_BLOCKS * 1024)
    tail = min(T, CONV_W - 1)
    zqkv_tail = main3[:, T - tail:, 2 * 1024:5 * 1024]
    conv_new = jnp.concatenate([conv_init[:, SUBLANES - (CONV_W - 1):, :], zqkv_tail], axis=1)[:, -(CONV_W - 1):, :]
    vn = main3[:, :, 1024:2048]
    return y, conv_new, s_new, vn


def kernel(x_prompt, x_sample, c_prompt, c_sample, state_conv, state_delta, w_ada, b_ada, norm1, norm2, w_in, conv_w, a_log, dt_bias, dn_norm, ln_v_w, ln_v_b, w_s, b_s, w_pa, w_pb, w_o, w_rg, b_rg, w_re, b_re, w_e_gate, w_e_up, w_e_down, final_norm):
    depth = w_ada.shape[0]
    assert depth == 1
    bp, tp, _ = x_prompt.shape
    bs_, ts, _ = x_sample.shape
    l = 0
    wi = w_in[l]
    o_qkv = 2 * A_WIDTH
    o_z = o_qkv + DN_CONV_CH
    o_b = o_z + DN_V
    o_ga = o_b + 2 * DN_HEADS
    w_main = jnp.concatenate([wi[:, :o_b], wi[:, o_ga:]], axis=1).astype(BF16)
    w_bg = jnp.pad(wi[:, o_b:o_ga], ((0, 0), (0, LANES - 2 * DN_HEADS))).astype(BF16)
    adt = jnp.stack([jnp.pad(a_log[l], (DN_HEADS, LANES - 2 * DN_HEADS)),
                     jnp.pad(dt_bias[l], (DN_HEADS, LANES - 2 * DN_HEADS))])
    w_r = jnp.pad(jnp.concatenate([w_rg[l], w_re[l]], axis=1), ((0, 0), (0, LANES - N_GROUPS - N_EXPERTS)))
    b_r = jnp.pad(jnp.concatenate([b_rg[l], b_re[l]]), (0, LANES - N_GROUPS - N_EXPERTS)).reshape(1, LANES)
    tri = jnp.tril(jnp.ones((GMLP_CHUNK, GMLP_CHUNK), F32))
    prm = dict(
        norm1=norm1[l], norm2=norm2[l], w_main=w_main, w_bg=w_bg, adt=adt, ln_v_w=ln_v_w[l], ln_v_b=ln_v_b[l],
        conv_w=conv_w[l], dn_norm=dn_norm[l], w_s_tril=w_s[l] * tri, b_s=b_s[l],
        w_pa=w_pa[l].astype(BF16), w_pb=w_pb[l].astype(BF16), w_o=w_o[l].astype(BF16), w_r=w_r, b_r=b_r,
        w_e_gate=w_e_gate[l].reshape(N_EXPERTS, D_MODEL, D_EXPERT).astype(BF16),
        w_e_up=w_e_up[l].reshape(N_EXPERTS, D_MODEL, D_EXPERT).astype(BF16),
        w_e_down=w_e_down[l].reshape(N_EXPERTS, D_EXPERT, D_MODEL).astype(BF16),
    )

    c_all = jnp.concatenate([c_prompt, c_sample], axis=0)
    mod = _ada(c_all, w_ada[l], b_ada[l]).reshape(bp + bs_, N_MOD, D_MODEL)
    mod_p, mod_s = mod[:bp], mod[bp:]

    pad_rows = SUBLANES - (CONV_W - 1)
    cinit_p = jnp.zeros((bp, SUBLANES, DN_CONV_CH), F32)
    cinit_s = jnp.pad(state_conv[l], ((0, 0), (pad_rows, 0), (0, 0)))
    s0_p = jnp.zeros((bp, DN_HEADS, DN_KDIM, DN_VDIM), F32)

    yp, conv_p, delta_p, _ = _layer(x_prompt, mod_p, cinit_p, s0_p, prm, final_norm,
                                    bb=1, tt=min(ROW_TILE, tp), dn_nseq=1, dn_chunk=min(DN_CHUNK, tp))
    ys, conv_s, delta_s, vn_s = _layer(x_sample, mod_s, cinit_s, state_delta[l], prm, final_norm,
                                       bb=min(bs_, ROW_TILE // ts), tt=ts,
                                       dn_nseq=min(bs_, DN_CHUNK // ts), dn_chunk=ts)
    return (yp, ys, conv_p[None], conv_s[None], delta_p[None], delta_s[None], vn_s[None])
```

```python
import functools

import jax
import jax.numpy as jnp
from jax import lax
from jax.experimental import pallas as pl
from jax.experimental.pallas import tpu as pltpu

F32 = jnp.float32
BF16 = jnp.bfloat16

D_MODEL = 1024
A_WIDTH = 1024
A_GROUPS = 4
A_GDIM = A_WIDTH // A_GROUPS
GMLP_CHUNK = 128
DN_HEADS = 8
DN_KDIM = 128
DN_VDIM = 128
DN_QK = DN_HEADS * DN_KDIM
DN_V = DN_HEADS * DN_VDIM
DN_CONV_CH = 2 * DN_QK + DN_V
CONV_W = 4
DN_CHUNK = 64
N_GROUPS = 4
EXP_PER_GROUP = 8
N_EXPERTS = N_GROUPS * EXP_PER_GROUP
D_EXPERT = 256
N_MOD = 6
EPS = 1e-6

LANES = 128
SUBLANES = 8
N_MAIN_BLOCKS = 8
ROUTER_LANE0 = N_GROUPS
VMEM_LIMIT = 56 * 1024 * 1024
ROW_TILE = 512
EXPERT_TILE = 256


def _sigmoid(x):
    return 1.0 / (1.0 + jnp.exp(-x))


def _gelu(x):
    return 0.5 * x * (1.0 + lax.erf(x * 0.7071067811865476))


def _softplus(x):
    return jnp.maximum(x, 0.0) + jnp.log(1.0 + jnp.exp(-jnp.abs(x)))


def _dot(a, b):
    return jnp.dot(a.astype(BF16), b.astype(BF16), preferred_element_type=F32)


def _dot_nt(a, b):
    return lax.dot_general(a.astype(BF16), b.astype(BF16), (((1,), (1,)), ((), ())), preferred_element_type=F32)


def _dot_tn(a, b):
    return lax.dot_general(a.astype(BF16), b.astype(BF16), (((0,), (0,)), ((), ())), preferred_element_type=F32)


def _split3(x):
    hi = x.astype(BF16)
    r1 = x - hi.astype(F32)
    mid = r1.astype(BF16)
    lo = (r1 - mid.astype(F32)).astype(BF16)
    return hi, mid, lo


def _dot_exact_lhs(a_bf16, x):
    hi, mid, lo = _split3(x)
    f = lambda p: jnp.dot(a_bf16, p, preferred_element_type=F32)
    return f(hi) + f(mid) + f(lo)


_PASSES = {"invp": 3, "invn": 3, "inv_hi": 1, "inv_kmax": 8, "sol": 1, "kk": 1, "ws": 1, "attn": 1, "state": 1}


def _hilo(x):
    hi = x.astype(BF16)
    lo = (x - hi.astype(F32)).astype(BF16)
    return hi, lo


def _mm(site, a, b, dims=(((1,), (0,)), ((), ()))):
    f = lambda p, q: lax.dot_general(p, q, dims, preferred_element_type=F32)
    if _PASSES[site] == 1:
        return f(a.astype(BF16), b.astype(BF16))
    ah, al = _hilo(a)
    bh, bl = _hilo(b)
    return f(ah, bh) + (f(ah, bl) + f(al, bh))


_NT = (((1,), (1,)), ((), ()))
_TN = (((0,), (0,)), ((), ()))

def _params(sem):
    return pltpu.CompilerParams(dimension_semantics=sem, vmem_limit_bytes=VMEM_LIMIT)


def _ada_kernel(c_ref, w_ref, b_ref, o_ref):
    c = c_ref[...]
    o_ref[...] = _dot(c * _sigmoid(c), w_ref[...]) + b_ref[...]


def _ada(c_all, w_ada, b_ada):
    n = c_all.shape[0]
    width = w_ada.shape[1]
    bn = 512
    return pl.pallas_call(
        _ada_kernel,
        grid=(width // bn,),
        in_specs=[pl.BlockSpec((n, D_MODEL), lambda j: (0, 0)),
                  pl.BlockSpec((D_MODEL, bn), lambda j: (0, j)),
                  pl.BlockSpec((1, bn), lambda j: (0, j))],
        out_specs=pl.BlockSpec((n, bn), lambda j: (0, j)),
        out_shape=jax.ShapeDtypeStruct((n, width), F32),
        compiler_params=_params(("arbitrary",)),
        name="ada_mod",
    )(c_all, w_ada, b_ada.reshape(1, width))


def _inproj_kernel(x_ref, mod_ref, n1_ref, w_ref, wbg_ref, lnw_ref, lnb_ref, adt_ref, o_ref, bg_ref, h_s):
    j = pl.program_id(1)
    tm = h_s.shape[0]

    @pl.when(j == 0)
    def _():
        x = x_ref[...]
        y = x * lax.rsqrt(jnp.mean(x * x, axis=-1, keepdims=True) + EPS) * n1_ref[...]
        mod = mod_ref[...]
        h = y * (1.0 + mod[:, 1:2, :]) + mod[:, 0:1, :]
        hb = h.reshape(tm, D_MODEL).astype(BF16)
        h_s[...] = hb
        zbg = jnp.dot(hb, wbg_ref[...], preferred_element_type=F32)
        lane = lax.broadcasted_iota(jnp.int32, zbg.shape, 1)
        adt = adt_ref[...]
        g = -jnp.exp(adt[0:1, :]) * _softplus(zbg + adt[1:2, :])
        bg_ref[...] = jnp.where(lane < DN_HEADS, _sigmoid(zbg), g)

    z = jnp.dot(h_s[...], w_ref[...], preferred_element_type=F32)

    @pl.when(j == 0)
    def _():
        o_ref[...] = _gelu(z)

    @pl.when(j == 1)
    def _():
        a = _gelu(z)
        mu = jnp.mean(a, axis=-1, keepdims=True)
        ac = a - mu
        var = jnp.mean(ac * ac, axis=-1, keepdims=True)
        o_ref[...] = ac * lax.rsqrt(var + EPS) * lnw_ref[...] + lnb_ref[...]

    @pl.when((j >= 2) & (j <= 4))
    def _():
        o_ref[...] = z

    @pl.when(j == 5)
    def _():
        o_ref[...] = z * _sigmoid(z)

    @pl.when(j >= 6)
    def _():
        o_ref[...] = _sigmoid(z)


def _inproj(x, mod, norm1, w_main, w_bg, ln_w, ln_b, adt, bb, tt):
    B, T, _ = x.shape
    tm = bb * tt
    tpb = T // tt
    n_tiles = (B // bb) * tpb
    M = B * T
    xmap = lambda i, j: (i // tpb, i % tpb, 0)
    return pl.pallas_call(
        _inproj_kernel,
        grid=(n_tiles, N_MAIN_BLOCKS),
        in_specs=[pl.BlockSpec((bb, tt, D_MODEL), xmap),
                  pl.BlockSpec((bb, N_MOD, D_MODEL), lambda i, j: (i // tpb, 0, 0)),
                  pl.BlockSpec((1, D_MODEL), lambda i, j: (0, 0)),
                  pl.BlockSpec((D_MODEL, 1024), lambda i, j: (0, j)),
                  pl.BlockSpec((D_MODEL, LANES), lambda i, j: (0, 0)),
                  pl.BlockSpec((1, A_WIDTH), lambda i, j: (0, 0)),
                  pl.BlockSpec((1, A_WIDTH), lambda i, j: (0, 0)),
                  pl.BlockSpec((2, LANES), lambda i, j: (0, 0))],
        out_specs=[pl.BlockSpec((tm, 1024), lambda i, j: (i, j)),
                   pl.BlockSpec((tm, LANES), lambda i, j: (i, 0))],
        out_shape=[jax.ShapeDtypeStruct((M, N_MAIN_BLOCKS * 1024), F32),
                   jax.ShapeDtypeStruct((M, LANES), F32)],
        scratch_shapes=[pltpu.VMEM((tm, D_MODEL), BF16)],
        compiler_params=_params(("arbitrary", "arbitrary")),
        name="in_proj",
    )(x, mod, norm1.reshape(1, D_MODEL), w_main, w_bg, ln_w.reshape(1, A_WIDTH), ln_b.reshape(1, A_WIDTH), adt)


def _inv_unit_lower_minus_eye(lmats, nilpotent):
    ns = [-l for l in lmats]
    ps = list(lmats)
    k = 2
    while k < nilpotent:
        early = k <= _PASSES["inv_kmax"]
        ps = [_mm("invp" if early else "inv_hi", p, p) for p in ps]
        ns = [n + p + _mm("invn" if early else "inv_hi", n, p) for n, p in zip(ns, ps)]
        k *= 2
    return ns


def _deltanet_kernel(q_ref, k_ref, v_ref, z_ref, bg_ref, cinit_ref, s0_ref, cw_ref, dnn_ref,
                     bo_ref, sn_ref, xp_s, s_s, *, nseq, C):
    R = nseq * C
    c = pl.program_id(1)
    nc = pl.num_programs(1)

    @pl.when(c == 0)
    def _():
        xp_s[:, 0:SUBLANES, :] = cinit_ref[...]
        s_s[...] = s0_ref[...]

    @pl.when(c > 0)
    def _():
        xp_s[:, 0:SUBLANES, :] = xp_s[:, C:C + SUBLANES, :]

    for s in range(nseq):
        xp_s[s, SUBLANES:SUBLANES + C, 0:DN_QK] = q_ref[s * C:(s + 1) * C, :]
        xp_s[s, SUBLANES:SUBLANES + C, DN_QK:2 * DN_QK] = k_ref[s * C:(s + 1) * C, :]
        xp_s[s, SUBLANES:SUBLANES + C, 2 * DN_QK:DN_CONV_CH] = v_ref[s * C:(s + 1) * C, :]

    cw = cw_ref[...]
    base = SUBLANES - (CONV_W - 1)
    acc = None
    for j in range(CONV_W):
        term = xp_s[:, base + j:base + j + C, :] * cw[j:j + 1, :]
        acc = term if acc is None else acc + term
    qkv = acc.reshape(R, DN_CONV_CH)
    qkv = qkv * _sigmoid(qkv)

    bg = bg_ref[...]
    row = lax.broadcasted_iota(jnp.int32, (R, R), 0)
    col = lax.broadcasted_iota(jnp.int32, (R, R), 1)
    if nseq > 1:
        shift = C.bit_length() - 1
        same = lax.shift_right_logical(row, shift) == lax.shift_right_logical(col, shift)
        tril = same & (col <= row)
        strict = same & (col < row)
    else:
        tril = col <= row
        strict = col < row
    gc = _dot_exact_lhs(jnp.where(tril, 1.0, 0.0).astype(BF16), bg)
    gct = gc.T
    dnn = dnn_ref[...]
    zg = z_ref[...]

    HR = range(DN_HEADS)
    hsl = [slice(h * DN_KDIM, (h + 1) * DN_KDIM) for h in HR]
    qh = [qkv[:, h * DN_KDIM:(h + 1) * DN_KDIM] for h in HR]
    kh = [qkv[:, DN_QK + h * DN_KDIM:DN_QK + (h + 1) * DN_KDIM] for h in HR]
    vh = [qkv[:, 2 * DN_QK + h * DN_VDIM:2 * DN_QK + (h + 1) * DN_VDIM] for h in HR]
    qn = [x * lax.rsqrt(jnp.sum(x * x, axis=-1, keepdims=True) + EPS) * (DN_KDIM ** -0.5) for x in qh]
    kn = [x * lax.rsqrt(jnp.sum(x * x, axis=-1, keepdims=True) + EPS) for x in kh]
    beta = [bg[:, h:h + 1] for h in HR]
    gcol = [gc[:, DN_HEADS + h:DN_HEADS + h + 1] for h in HR]
    grow = [gct[DN_HEADS + h:DN_HEADS + h + 1, :] for h in HR]
    decay = [jnp.exp(jnp.minimum(gcol[h] - grow[h], 0.0)) for h in HR]
    eg = [jnp.exp(gcol[h]) for h in HR]
    kb = [kn[h] * beta[h] for h in HR]
    vb = [vh[h] * beta[h] for h in HR]
    a2 = [_mm("kk", jnp.concatenate([kb[h], qn[h]], axis=0), kn[h], _NT) for h in HR]
    lmat = [jnp.where(strict, a2[h][:R] * decay[h], 0.0) for h in HR]
    attn = [jnp.where(tril, a2[h][R:] * decay[h], 0.0) for h in HR]
    tn = _inv_unit_lower_minus_eye(lmat, C)
    rhs = [jnp.concatenate([vb[h], kb[h] * eg[h]], axis=1) for h in HR]
    sol = [rhs[h] + _mm("sol", tn[h], rhs[h]) for h in HR]
    u = [x[:, :DN_VDIM] for x in sol]
    w = [x[:, DN_VDIM:] for x in sol]
    qg = [qn[h] * eg[h] for h in HR]
    vnew = [[None] * nseq for _ in HR]
    qs = [[None] * nseq for _ in HR]
    for s in range(nseq):
        rs = slice(s * C, (s + 1) * C)
        st = [s_s[s, h] for h in HR]
        ws = [_mm("ws", jnp.concatenate([w[h][rs], qg[h][rs]], axis=0), st[h]) for h in HR]
        for h in HR:
            vnew[h][s] = u[h][rs] - ws[h][:C]
            qs[h][s] = ws[h][C:]
        glast = [gcol[h][(s + 1) * C - 1:(s + 1) * C, :] for h in HR]
        kd = [kn[h][rs] * jnp.exp(glast[h] - gcol[h][rs]) for h in HR]
        upd = [_mm("state", kd[h], vnew[h][s], _TN) for h in HR]
        for h in HR:
            s_s[s, h] = st[h] * jnp.exp(glast[h]) + upd[h]
    cat = lambda parts: parts[0] if nseq == 1 else jnp.concatenate(parts, axis=0)
    o = [cat(qs[h]) + _mm("attn", attn[h], cat(vnew[h])) for h in HR]
    for h in HR:
        on = o[h] * lax.rsqrt(jnp.mean(o[h] * o[h], axis=-1, keepdims=True) + EPS) * dnn
        bo_ref[:, hsl[h]] = on * zg[:, hsl[h]]

    @pl.when(c == nc - 1)
    def _():
        sn_ref[...] = s_s[...]


def _deltanet(main, bg, conv_init, s0, conv_w, dn_norm, nseq, C, nc):
    M = main.shape[0]
    R = nseq * C
    nb = M // (R * nc)
    rmap = lambda col: (lambda b, c: (b * nc + c, col))
    kern = functools.partial(_deltanet_kernel, nseq=nseq, C=C)
    return pl.pallas_call(
        kern,
        grid=(nb, nc),
        in_specs=[pl.BlockSpec((R, 1024), rmap(2)),
                  pl.BlockSpec((R, 1024), rmap(3)),
                  pl.BlockSpec((R, 1024), rmap(4)),
                  pl.BlockSpec((R, 1024), rmap(5)),
                  pl.BlockSpec((R, LANES), rmap(0)),
                  pl.BlockSpec((nseq, SUBLANES, DN_CONV_CH), lambda b, c: (b, 0, 0)),
                  pl.BlockSpec((nseq, DN_HEADS, DN_KDIM, DN_VDIM), lambda b, c: (b, 0, 0, 0)),
                  pl.BlockSpec((CONV_W, DN_CONV_CH), lambda b, c: (0, 0)),
                  pl.BlockSpec((1, DN_VDIM), lambda b, c: (0, 0))],
        out_specs=[pl.BlockSpec((R, DN_V), rmap(0)),
                   pl.BlockSpec((nseq, DN_HEADS, DN_KDIM, DN_VDIM), lambda b, c: (b, 0, 0, 0))],
        out_shape=[jax.ShapeDtypeStruct((M, DN_V), F32),
                   jax.ShapeDtypeStruct(s0.shape, F32)],
        scratch_shapes=[pltpu.VMEM((nseq, C + SUBLANES, DN_CONV_CH), F32),
                        pltpu.VMEM((nseq, DN_HEADS, DN_KDIM, DN_VDIM), F32)],
        compiler_params=_params(("arbitrary", "arbitrary")),
        name="deltanet",
    )(main, main, main, main, bg, conv_init, s0, conv_w, dn_norm.reshape(1, DN_VDIM))


def _outproj_kernel(u_ref, vn_ref, ga_ref, gb_ref, bo_ref, x_ref, mod_ref, ws_ref, bs_ref, wpa_ref, wpb_ref,
                    wo_ref, n2_ref, wr_ref, br_ref, x1_ref, h2_ref, rt_ref, cnt_ref, cnt_s):
    tm = u_ref.shape[0]
    vn = vn_ref[...].astype(BF16)
    parts = []
    for r in range(tm // GMLP_CHUNK):
        rs = slice(r * GMLP_CHUNK, (r + 1) * GMLP_CHUNK)
        row_parts = []
        for g in range(A_GROUPS):
            gs = slice(g * A_GDIM, (g + 1) * A_GDIM)
            row_parts.append(jnp.dot(ws_ref[g], vn[rs, gs], preferred_element_type=F32))
        parts.append(jnp.concatenate(row_parts, axis=1) + bs_ref[...])
    sv = parts[0] if len(parts) == 1 else jnp.concatenate(parts, axis=0)
    a_out = u_ref[...] * sv
    pa = _dot(a_out, wpa_ref[...])
    pb = _dot(bo_ref[...], wpb_ref[...])
    m = ga_ref[...] * pa + gb_ref[...] * pb
    mix = _dot(m, wo_ref[...])
    mod = mod_ref[...]
    x = x_ref[...]
    x1 = x + mod[:, 2:3, :] * mix.reshape(x.shape)
    x1_ref[...] = x1
    y = x1 * lax.rsqrt(jnp.mean(x1 * x1, axis=-1, keepdims=True) + EPS) * n2_ref[...]
    h2 = (y * (1.0 + mod[:, 4:5, :]) + mod[:, 3:4, :]).reshape(tm, D_MODEL)
    h2_ref[...] = h2

    logits = jnp.dot(h2, wr_ref[...], preferred_element_type=F32, precision=lax.Precision.HIGHEST) + br_ref[...]
    lane = lax.broadcasted_iota(jnp.int32, logits.shape, 1)
    lanef = lane.astype(F32)
    neg = jnp.float32(-jnp.inf)
    big = jnp.float32(1e9)
    gl = jnp.where(lane < N_GROUPS, logits, neg)
    gmax = jnp.max(gl, axis=-1, keepdims=True)
    gsel = jnp.min(jnp.where(gl == gmax, lanef, big), axis=-1, keepdims=True)
    gp = 1.0 / jnp.sum(jnp.exp(gl - gmax), axis=-1, keepdims=True)
    lo = ROUTER_LANE0 + EXP_PER_GROUP * gsel
    in_grp = (lanef >= lo) & (lanef < lo + EXP_PER_GROUP)
    el = jnp.where(in_grp, logits, neg)
    m1 = jnp.max(el, axis=-1, keepdims=True)
    i1 = jnp.min(jnp.where(el == m1, lanef, big), axis=-1, keepdims=True)
    el2 = jnp.where(lanef == i1, neg, el)
    m2 = jnp.max(el2, axis=-1, keepdims=True)
    i2 = jnp.min(jnp.where(el2 == m2, lanef, big), axis=-1, keepdims=True)
    ex = jnp.exp(m2 - m1)
    w1 = gp / (1.0 + ex)
    w2 = gp * ex / (1.0 + ex)
    e1 = i1 - ROUTER_LANE0
    e2 = i2 - ROUTER_LANE0

    @pl.when(pl.program_id(0) == 0)
    def _():
        cnt_s[...] = jnp.zeros_like(cnt_s)

    oh1 = jnp.where(lanef == e1, 1.0, 0.0)
    oh2 = jnp.where(lanef == e2, 1.0, 0.0)
    oh = oh1 + oh2
    row = lax.broadcasted_iota(jnp.int32, (tm, tm), 0)
    col = lax.broadcasted_iota(jnp.int32, (tm, tm), 1)
    before = jnp.where(col < row, 1.0, 0.0).astype(BF16)
    seen = jnp.dot(before, oh.astype(BF16), preferred_element_type=F32) + cnt_s[...]
    r1 = jnp.sum(oh1 * seen, axis=-1, keepdims=True)
    r2 = jnp.sum(oh2 * seen, axis=-1, keepdims=True)
    cnt_s[...] += jnp.sum(oh, axis=0, keepdims=True)
    cnt_ref[...] = cnt_s[...]
    rec = jnp.zeros_like(logits)
    for k, val in enumerate((e1, e2, r1, r2, w1, w2)):
        rec = jnp.where(lane == k, val, rec)
    rt_ref[...] = rec


def _outproj(main, bo, x, mod, ws_eff, bs_full, w_pa, w_pb, w_o, norm2, w_r, b_r, bb, tt):
    B, T, _ = x.shape
    tm = bb * tt
    tpb = T // tt
    n_tiles = (B // bb) * tpb
    M = B * T
    cmap = lambda col: (lambda i: (i, col))
    full2 = lambda i: (0, 0)
    return pl.pallas_call(
        _outproj_kernel,
        grid=(n_tiles,),
        in_specs=[pl.BlockSpec((tm, 1024), cmap(0)),
                  pl.BlockSpec((tm, 1024), cmap(1)),
                  pl.BlockSpec((tm, 1024), cmap(6)),
                  pl.BlockSpec((tm, 1024), cmap(7)),
                  pl.BlockSpec((tm, DN_V), cmap(0)),
                  pl.BlockSpec((bb, tt, D_MODEL), lambda i: (i // tpb, i % tpb, 0)),
                  pl.BlockSpec((bb, N_MOD, D_MODEL), lambda i: (i // tpb, 0, 0)),
                  pl.BlockSpec((A_GROUPS, GMLP_CHUNK, GMLP_CHUNK), lambda i: (0, 0, 0)),
                  pl.BlockSpec((GMLP_CHUNK, A_WIDTH), full2),
                  pl.BlockSpec((A_WIDTH, D_MODEL), full2),
                  pl.BlockSpec((DN_V, D_MODEL), full2),
                  pl.BlockSpec((D_MODEL, D_MODEL), full2),
                  pl.BlockSpec((1, D_MODEL), full2),
                  pl.BlockSpec((D_MODEL, LANES), full2),
                  pl.BlockSpec((1, LANES), full2)],
        out_specs=[pl.BlockSpec((bb, tt, D_MODEL), lambda i: (i // tpb, i % tpb, 0)),
                   pl.BlockSpec((tm, D_MODEL), cmap(0)),
                   pl.BlockSpec((tm, LANES), cmap(0)),
                   pl.BlockSpec((1, LANES), full2)],
        out_shape=[jax.ShapeDtypeStruct((B, T, D_MODEL), F32),
                   jax.ShapeDtypeStruct((M, D_MODEL), F32),
                   jax.ShapeDtypeStruct((M, LANES), F32),
                   jax.ShapeDtypeStruct((1, LANES), F32)],
        scratch_shapes=[pltpu.VMEM((1, LANES), F32)],
        compiler_params=_params(("arbitrary",)),
        name="out_proj",
    )(main, main, main, main, bo, x, mod, ws_eff, bs_full, w_pa, w_pb, w_o, norm2.reshape(1, D_MODEL), w_r, b_r)


def _row_copy(src_ref, src_row, dst_ref, dst_row, sem):
    return pltpu.make_async_copy(src_ref.at[pl.ds(src_row, 1), :], dst_ref.at[pl.ds(dst_row, 1), :], sem)


def _dispatch_kernel(base_ref, ids_ref, h_ref, xs_in_ref, xs_ref, sem):
    del xs_in_ref
    tm = h_ref.shape[0]

    def issue(t, carry):
        for k in range(2):
            dst = base_ref[ids_ref[0, k, t]] + ids_ref[0, 2 + k, t]
            _row_copy(h_ref, t, xs_ref, dst, sem).start()
        return carry

    lax.fori_loop(0, tm, issue, 0, unroll=4)

    def drain(t, carry):
        for k in range(2):
            _row_copy(h_ref, 0, xs_ref, 0, sem).wait()
        return carry

    lax.fori_loop(0, tm, drain, 0, unroll=4)


def _dispatch(h2, ids, base, n_rows, tm):
    M = h2.shape[0]
    n_tiles = M // tm
    xs0 = jnp.zeros((n_rows, D_MODEL), F32)
    grid_spec = pltpu.PrefetchScalarGridSpec(
        num_scalar_prefetch=1,
        grid=(n_tiles,),
        in_specs=[pl.BlockSpec((1, 4, tm), lambda i, b: (i, 0, 0), memory_space=pltpu.SMEM),
                  pl.BlockSpec((tm, D_MODEL), lambda i, b: (i, 0)),
                  pl.BlockSpec(memory_space=pl.ANY)],
        out_specs=pl.BlockSpec(memory_space=pl.ANY),
        scratch_shapes=[pltpu.SemaphoreType.DMA(())],
    )
    return pl.pallas_call(
        _dispatch_kernel,
        grid_spec=grid_spec,
        out_shape=jax.ShapeDtypeStruct((n_rows, D_MODEL), F32),
        input_output_aliases={3: 0},
        compiler_params=_params(("arbitrary",)),
        name="moe_dispatch",
    )(base, ids, h2, xs0)


def _expert_kernel(te_ref, na_ref, x_ref, wg_ref, wu_ref, wd_ref, o_ref):
    del te_ref
    live = pl.program_id(0) < na_ref[0]

    @pl.when(live)
    def _():
        x = x_ref[...].astype(BF16)
        a = jnp.dot(x, wg_ref[0], preferred_element_type=F32)
        u = jnp.dot(x, wu_ref[0], preferred_element_type=F32)
        o_ref[...] = _dot(a * _sigmoid(a) * u, wd_ref[0])

    @pl.when(jnp.logical_not(live))
    def _():
        o_ref[...] = jnp.zeros_like(o_ref)


def _experts(xs, tile_expert, n_active, wg, wu, wd):
    n_rows = xs.shape[0]
    n_tiles = n_rows // EXPERT_TILE
    live = lambda i, na: jnp.minimum(i, na[0] - 1)
    wmap = lambda i, te, na: (te[live(i, na)], 0, 0)
    grid_spec = pltpu.PrefetchScalarGridSpec(
        num_scalar_prefetch=2,
        grid=(n_tiles,),
        in_specs=[pl.BlockSpec((EXPERT_TILE, D_MODEL), lambda i, te, na: (live(i, na), 0)),
                  pl.BlockSpec((1, D_MODEL, D_EXPERT), wmap),
                  pl.BlockSpec((1, D_MODEL, D_EXPERT), wmap),
                  pl.BlockSpec((1, D_EXPERT, D_MODEL), wmap)],
        out_specs=pl.BlockSpec((EXPERT_TILE, D_MODEL), lambda i, te, na: (i, 0)),
    )
    return pl.pallas_call(
        _expert_kernel,
        grid_spec=grid_spec,
        out_shape=jax.ShapeDtypeStruct((n_rows, D_MODEL), F32),
        compiler_params=_params(("arbitrary",)),
        name="moe_experts",
    )(tile_expert, n_active, xs, wg, wu, wd)


def _combine_kernel(base_ref, ids_ref, rt_ref, x1_ref, mod_ref, fn_ref, ys_ref, y_ref, g_s, sem):
    tm = rt_ref.shape[0]

    def issue(t, carry):
        for k in range(2):
            src = base_ref[ids_ref[0, k, t]] + ids_ref[0, 2 + k, t]
            _row_copy(ys_ref, src, g_s.at[k], t, sem).start()
        return carry

    lax.fori_loop(0, tm, issue, 0, unroll=4)

    def drain(t, carry):
        for k in range(2):
            _row_copy(ys_ref, 0, g_s.at[k], 0, sem).wait()
        return carry

    lax.fori_loop(0, tm, drain, 0, unroll=4)

    rt = rt_ref[...]
    moe = rt[:, 4:5] * g_s[0] + rt[:, 5:6] * g_s[1]
    mod = mod_ref[...]
    x1 = x1_ref[...]
    x2 = x1 + mod[:, 5:6, :] * moe.reshape(x1.shape)
    y_ref[...] = x2 * lax.rsqrt(jnp.mean(x2 * x2, axis=-1, keepdims=True) + EPS) * fn_ref[...]


def _combine(ys, ids, base, route, x1, mod, final_norm, bb, tt):
    B, T, _ = x1.shape
    tm = bb * tt
    tpb = T // tt
    n_tiles = (B // bb) * tpb
    xmap = lambda i, b: (i // tpb, i % tpb, 0)
    grid_spec = pltpu.PrefetchScalarGridSpec(
        num_scalar_prefetch=1,
        grid=(n_tiles,),
        in_specs=[pl.BlockSpec((1, 4, tm), lambda i, b: (i, 0, 0), memory_space=pltpu.SMEM),
                  pl.BlockSpec((tm, LANES), lambda i, b: (i, 0)),
                  pl.BlockSpec((bb, tt, D_MODEL), xmap),
                  pl.BlockSpec((bb, N_MOD, D_MODEL), lambda i, b: (i // tpb, 0, 0)),
                  pl.BlockSpec((1, D_MODEL), lambda i, b: (0, 0)),
                  pl.BlockSpec(memory_space=pl.ANY)],
        out_specs=pl.BlockSpec((bb, tt, D_MODEL), xmap),
        scratch_shapes=[pltpu.VMEM((2, tm, D_MODEL), F32), pltpu.SemaphoreType.DMA(())],
    )
    return pl.pallas_call(
        _combine_kernel,
        grid_spec=grid_spec,
        out_shape=jax.ShapeDtypeStruct((B, T, D_MODEL), F32),
        compiler_params=_params(("arbitrary",)),
        name="moe_combine",
    )(base, ids, route, x1, mod, final_norm.reshape(1, D_MODEL), ys)


def _moe(h2, route, counts, wg, wu, wd, x1, mod, final_norm, bb, tt):
    M = h2.shape[0]
    tm = bb * tt
    n_tiles_tok = M // tm
    max_tiles = (2 * M + N_EXPERTS * (EXPERT_TILE - 1)) // EXPERT_TILE
    cnt = counts[0, :N_EXPERTS].astype(jnp.int32)
    nt = (cnt + EXPERT_TILE - 1) // EXPERT_TILE
    ends = jnp.cumsum(nt)
    base = ((ends - nt) * EXPERT_TILE).astype(jnp.int32)
    n_active = ends[-1:].astype(jnp.int32)
    tile_expert = jnp.minimum(jnp.sum(jnp.arange(max_tiles)[:, None] >= ends[None, :], axis=1),
                              N_EXPERTS - 1).astype(jnp.int32)
    ids = route[:, :4].astype(jnp.int32)
    ids = jnp.transpose(ids.reshape(n_tiles_tok, tm, 4), (0, 2, 1))
    xs = _dispatch(h2, ids, base, max_tiles * EXPERT_TILE, tm)
    ys = _experts(xs, tile_expert, n_active, wg, wu, wd)
    return _combine(ys, ids, base, route, x1, mod, final_norm, bb, tt)


def _layer(x, mod, conv_init, s0, prm, final_norm, *, bb, tt, dn_nseq, dn_chunk):
    B, T, _ = x.shape
    main, bg = _inproj(x, mod, prm["norm1"], prm["w_main"], prm["w_bg"], prm["ln_v_w"], prm["ln_v_b"], prm["adt"],
                       bb, tt)
    nc = T // dn_chunk
    bo, s_new = _deltanet(main, bg, conv_init, s0, prm["conv_w"], prm["dn_norm"], dn_nseq, dn_chunk, nc)
    L = min(GMLP_CHUNK, T)
    reps = GMLP_CHUNK // L
    wm = prm["w_s_tril"][:, :L, :L]
    if reps > 1:
        blk = (jnp.arange(GMLP_CHUNK)[:, None] // L) == (jnp.arange(GMLP_CHUNK)[None, :] // L)
        wm = jnp.where(blk[None], jnp.tile(wm, (1, reps, reps)), 0.0)
    ws_eff = wm.astype(BF16)
    bs_rows = jnp.tile(jnp.transpose(prm["b_s"][:, :L]), (reps, 1))
    bs_full = jnp.repeat(bs_rows, A_GDIM, axis=1)
    x1, h2, route, counts = _outproj(main, bo, x, mod, ws_eff, bs_full, prm["w_pa"], prm["w_pb"], prm["w_o"],
                                     prm["norm2"], prm["w_r"], prm["b_r"], bb, tt)
    y = _moe(h2, route, counts, prm["w_e_gate"], prm["w_e_up"], prm["w_e_down"], x1, mod, final_norm, bb, tt)
    main3 = main.reshape(B, T, N_MAIN_BLOCKS * 1024)
    tail = min(T, CONV_W - 1)
    zqkv_tail = main3[:, T - tail:, 2 * 1024:5 * 1024]
    conv_new = jnp.concatenate([conv_init[:, SUBLANES - (CONV_W - 1):, :], zqkv_tail], axis=1)[:, -(CONV_W - 1):, :]
    vn = main3[:, :, 1024:2048]
    return y, conv_new, s_new, vn


def kernel(x_prompt, x_sample, c_prompt, c_sample, state_conv, state_delta, w_ada, b_ada, norm1, norm2, w_in, conv_w, a_log, dt_bias, dn_norm, ln_v_w, ln_v_b, w_s, b_s, w_pa, w_pb, w_o, w_rg, b_rg, w_re, b_re, w_e_gate, w_e_up, w_e_down, final_norm):
    depth = w_ada.shape[0]
    assert depth == 1
    bp, tp, _ = x_prompt.shape
    bs_, ts, _ = x_sample.shape
    l = 0
    wi = w_in[l]
    o_qkv = 2 * A_WIDTH
    o_z = o_qkv + DN_CONV_CH
    o_b = o_z + DN_V
    o_ga = o_b + 2 * DN_HEADS
    w_main = jnp.concatenate([wi[:, :o_b], wi[:, o_ga:]], axis=1).astype(BF16)
    w_bg = jnp.pad(wi[:, o_b:o_ga], ((0, 0), (0, LANES - 2 * DN_HEADS))).astype(BF16)
    adt = jnp.stack([jnp.pad(a_log[l], (DN_HEADS, LANES - 2 * DN_HEADS)),
                     jnp.pad(dt_bias[l], (DN_HEADS, LANES - 2 * DN_HEADS))])
    w_r = jnp.pad(jnp.concatenate([w_rg[l], w_re[l]], axis=1), ((0, 0), (0, LANES - N_GROUPS - N_EXPERTS)))
    b_r = jnp.pad(jnp.concatenate([b_rg[l], b_re[l]]), (0, LANES - N_GROUPS - N_EXPERTS)).reshape(1, LANES)
    tri = jnp.tril(jnp.ones((GMLP_CHUNK, GMLP_CHUNK), F32))
    prm = dict(
        norm1=norm1[l], norm2=norm2[l], w_main=w_main, w_bg=w_bg, adt=adt, ln_v_w=ln_v_w[l], ln_v_b=ln_v_b[l],
        conv_w=conv_w[l], dn_norm=dn_norm[l], w_s_tril=w_s[l] * tri, b_s=b_s[l],
        w_pa=w_pa[l].astype(BF16), w_pb=w_pb[l].astype(BF16), w_o=w_o[l].astype(BF16), w_r=w_r, b_r=b_r,
        w_e_gate=w_e_gate[l].reshape(N_EXPERTS, D_MODEL, D_EXPERT).astype(BF16),
        w_e_up=w_e_up[l].reshape(N_EXPERTS, D_MODEL, D_EXPERT).astype(BF16),
        w_e_down=w_e_down[l].reshape(N_EXPERTS, D_EXPERT, D_MODEL).astype(BF16),
    )

    c_all = jnp.concatenate([c_prompt, c_sample], axis=0)
    mod = _ada(c_all, w_ada[l], b_ada[l]).reshape(bp + bs_, N_MOD, D_MODEL)
    mod_p, mod_s = mod[:bp], mod[bp:]

    pad_rows = SUBLANES - (CONV_W - 1)
    cinit_p = jnp.zeros((bp, SUBLANES, DN_CONV_CH), F32)
    cinit_s = jnp.pad(state_conv[l], ((0, 0), (pad_rows, 0), (0, 0)))
    s0_p = jnp.zeros((bp, DN_HEADS, DN_KDIM, DN_VDIM), F32)

    yp, conv_p, delta_p, _ = _layer(x_prompt, mod_p, cinit_p, s0_p, prm, final_norm,
                                    bb=1, tt=min(ROW_TILE, tp), dn_nseq=1, dn_chunk=min(DN_CHUNK, tp))
    ys, conv_s, delta_s, vn_s = _layer(x_sample, mod_s, cinit_s, state_delta[l], prm, final_norm,
                                       bb=min(bs_, ROW_TILE // ts), tt=ts,
                                       dn_nseq=min(bs_, DN_CHUNK // ts), dn_chunk=ts)
    return (yp, ys, conv_p[None], conv_s[None], delta_p[None], delta_s[None], vn_s[None])
```

```python
import functools

import jax
import jax.numpy as jnp
from jax import lax
from jax.experimental import pallas as pl
from jax.experimental.pallas import tpu as pltpu

F32 = jnp.float32
BF16 = jnp.bfloat16

D_MODEL = 1024
A_WIDTH = 1024
A_GROUPS = 4
A_GDIM = A_WIDTH // A_GROUPS
GMLP_CHUNK = 128
DN_HEADS = 8
DN_KDIM = 128
DN_VDIM = 128
DN_QK = DN_HEADS * DN_KDIM
DN_V = DN_HEADS * DN_VDIM
DN_CONV_CH = 2 * DN_QK + DN_V
CONV_W = 4
DN_CHUNK = 64
N_GROUPS = 4
EXP_PER_GROUP = 8
N_EXPERTS = N_GROUPS * EXP_PER_GROUP
D_EXPERT = 256
N_MOD = 6
EPS = 1e-6

LANES = 128
SUBLANES = 8
N_MAIN_BLOCKS = 8
ROUTER_LANE0 = N_GROUPS
VMEM_LIMIT = 56 * 1024 * 1024
ROW_TILE = 512
EXPERT_TILE = 256


def _sigmoid(x):
    return 1.0 / (1.0 + jnp.exp(-x))


def _gelu(x):
    return 0.5 * x * (1.0 + lax.erf(x * 0.7071067811865476))


def _softplus(x):
    return jnp.maximum(x, 0.0) + jnp.log(1.0 + jnp.exp(-jnp.abs(x)))


def _dot(a, b):
    return jnp.dot(a.astype(BF16), b.astype(BF16), preferred_element_type=F32)


def _dot_nt(a, b):
    return lax.dot_general(a.astype(BF16), b.astype(BF16), (((1,), (1,)), ((), ())), preferred_element_type=F32)


def _dot_tn(a, b):
    return lax.dot_general(a.astype(BF16), b.astype(BF16), (((0,), (0,)), ((), ())), preferred_element_type=F32)


def _split3(x):
    hi = x.astype(BF16)
    r1 = x - hi.astype(F32)
    mid = r1.astype(BF16)
    lo = (r1 - mid.astype(F32)).astype(BF16)
    return hi, mid, lo


def _dot_exact_lhs(a_bf16, x):
    hi, mid, lo = _split3(x)
    f = lambda p: jnp.dot(a_bf16, p, preferred_element_type=F32)
    return f(hi) + f(mid) + f(lo)


_PASSES = {"invp": 3, "invn": 3, "inv_hi": 1, "inv_kmax": 8, "sol": 1, "kk": 1, "ws": 1, "attn": 1, "state": 1}


def _hilo(x):
    hi = x.astype(BF16)
    lo = (x - hi.astype(F32)).astype(BF16)
    return hi, lo


def _mm(site, a, b, dims=(((1,), (0,)), ((), ()))):
    f = lambda p, q: lax.dot_general(p, q, dims, preferred_element_type=F32)
    if _PASSES[site] == 1:
        return f(a.astype(BF16), b.astype(BF16))
    ah, al = _hilo(a)
    bh, bl = _hilo(b)
    return f(ah, bh) + (f(ah, bl) + f(al, bh))


_NT = (((1,), (1,)), ((), ()))
_TN = (((0,), (0,)), ((), ()))

def _params(sem):
    return pltpu.CompilerParams(dimension_semantics=sem, vmem_limit_bytes=VMEM_LIMIT)


def _ada_kernel(c_ref, w_ref, b_ref, o_ref):
    c = c_ref[...]
    o_ref[...] = _dot(c * _sigmoid(c), w_ref[...]) + b_ref[...]


def _ada(c_all, w_ada, b_ada):
    n = c_all.shape[0]
    width = w_ada.shape[1]
    bn = 512
    return pl.pallas_call(
        _ada_kernel,
        grid=(width // bn,),
        in_specs=[pl.BlockSpec((n, D_MODEL), lambda j: (0, 0)),
                  pl.BlockSpec((D_MODEL, bn), lambda j: (0, j)),
                  pl.BlockSpec((1, bn), lambda j: (0, j))],
        out_specs=pl.BlockSpec((n, bn), lambda j: (0, j)),
        out_shape=jax.ShapeDtypeStruct((n, width), F32),
        compiler_params=_params(("arbitrary",)),
        name="ada_mod",
    )(c_all, w_ada, b_ada.reshape(1, width))


def _inproj_kernel(x_ref, mod_ref, n1_ref, w_ref, wbg_ref, lnw_ref, lnb_ref, adt_ref,
                   u_ref, vn_ref, qkv_ref, sz_ref, ga_ref, gb_ref, bg_ref):
    tm = u_ref.shape[0]
    x = x_ref[...]
    y = x * lax.rsqrt(jnp.mean(x * x, axis=-1, keepdims=True) + EPS) * n1_ref[...]
    mod = mod_ref[...]
    h = y * (1.0 + mod[:, 1:2, :]) + mod[:, 0:1, :]
    hb = h.reshape(tm, D_MODEL).astype(BF16)
    zbg = jnp.dot(hb, wbg_ref[...], preferred_element_type=F32)
    lane = lax.broadcasted_iota(jnp.int32, zbg.shape, 1)
    adt = adt_ref[...]
    g = -jnp.exp(adt[0:1, :]) * _softplus(zbg + adt[1:2, :])
    bg_ref[...] = jnp.where(lane < DN_HEADS, _sigmoid(zbg), g)

    blk = lambda j: jnp.dot(hb, w_ref[:, j * 1024:(j + 1) * 1024], preferred_element_type=F32)
    u_ref[...] = _gelu(blk(0)).astype(u_ref.dtype)
    a = _gelu(blk(1))
    ac = a - jnp.mean(a, axis=-1, keepdims=True)
    var = jnp.mean(ac * ac, axis=-1, keepdims=True)
    vn_ref[...] = (ac * lax.rsqrt(var + EPS) * lnw_ref[...] + lnb_ref[...]).astype(vn_ref.dtype)
    for j in range(3):
        qkv_ref[:, j * 1024:(j + 1) * 1024] = blk(2 + j)
    z = blk(5)
    sz_ref[...] = (z * _sigmoid(z)).astype(sz_ref.dtype)
    ga_ref[...] = _sigmoid(blk(6)).astype(ga_ref.dtype)
    gb_ref[...] = _sigmoid(blk(7)).astype(gb_ref.dtype)


def _inproj(x, mod, norm1, w_main, w_bg, ln_w, ln_b, adt, bb, tt, vn_dtype):
    B, T, _ = x.shape
    tm = bb * tt
    tpb = T // tt
    n_tiles = (B // bb) * tpb
    M = B * T
    const = lambda i: (0, 0)
    rows = lambda i: (i, 0)
    resident = dict(pipeline_mode=pl.Buffered(1))
    return pl.pallas_call(
        _inproj_kernel,
        grid=(n_tiles,),
        in_specs=[pl.BlockSpec((bb, tt, D_MODEL), lambda i: (i // tpb, i % tpb, 0)),
                  pl.BlockSpec((bb, N_MOD, D_MODEL), lambda i: (i // tpb, 0, 0)),
                  pl.BlockSpec((1, D_MODEL), const),
                  pl.BlockSpec((D_MODEL, N_MAIN_BLOCKS * 1024), const, **resident),
                  pl.BlockSpec((D_MODEL, LANES), const, **resident),
                  pl.BlockSpec((1, A_WIDTH), const),
                  pl.BlockSpec((1, A_WIDTH), const),
                  pl.BlockSpec((2, LANES), const)],
        out_specs=[pl.BlockSpec((tm, A_WIDTH), rows),
                   pl.BlockSpec((tm, A_WIDTH), rows),
                   pl.BlockSpec((tm, DN_CONV_CH), rows),
                   pl.BlockSpec((tm, DN_V), rows),
                   pl.BlockSpec((tm, D_MODEL), rows),
                   pl.BlockSpec((tm, D_MODEL), rows),
                   pl.BlockSpec((tm, LANES), rows)],
        out_shape=[jax.ShapeDtypeStruct((M, A_WIDTH), BF16),
                   jax.ShapeDtypeStruct((M, A_WIDTH), vn_dtype),
                   jax.ShapeDtypeStruct((M, DN_CONV_CH), F32),
                   jax.ShapeDtypeStruct((M, DN_V), BF16),
                   jax.ShapeDtypeStruct((M, D_MODEL), BF16),
                   jax.ShapeDtypeStruct((M, D_MODEL), BF16),
                   jax.ShapeDtypeStruct((M, LANES), F32)],
        compiler_params=_params(("arbitrary",)),
        name="in_proj",
    )(x, mod, norm1.reshape(1, D_MODEL), w_main, w_bg, ln_w.reshape(1, A_WIDTH), ln_b.reshape(1, A_WIDTH), adt)


def _inv_unit_lower_minus_eye(lmats, nilpotent):
    ns = [-l for l in lmats]
    ps = list(lmats)
    k = 2
    while k < nilpotent:
        early = k <= _PASSES["inv_kmax"]
        ps = [_mm("invp" if early else "inv_hi", p, p) for p in ps]
        ns = [n + p + _mm("invn" if early else "inv_hi", n, p) for n, p in zip(ns, ps)]
        k *= 2
    return ns


def _deltanet_kernel(q_ref, k_ref, v_ref, z_ref, bg_ref, cinit_ref, s0_ref, cw_ref, dnn_ref,
                     bo_ref, sn_ref, xp_s, s_s, *, nseq, C):
    R = nseq * C
    c = pl.program_id(1)
    nc = pl.num_programs(1)

    @pl.when(c == 0)
    def _():
        xp_s[:, 0:SUBLANES, :] = cinit_ref[...]
        s_s[...] = s0_ref[...]

    @pl.when(c > 0)
    def _():
        xp_s[:, 0:SUBLANES, :] = xp_s[:, C:C + SUBLANES, :]

    for s in range(nseq):
        xp_s[s, SUBLANES:SUBLANES + C, 0:DN_QK] = q_ref[s * C:(s + 1) * C, :]
        xp_s[s, SUBLANES:SUBLANES + C, DN_QK:2 * DN_QK] = k_ref[s * C:(s + 1) * C, :]
        xp_s[s, SUBLANES:SUBLANES + C, 2 * DN_QK:DN_CONV_CH] = v_ref[s * C:(s + 1) * C, :]

    cw = cw_ref[...]
    base = SUBLANES - (CONV_W - 1)
    acc = None
    for j in range(CONV_W):
        term = xp_s[:, base + j:base + j + C, :] * cw[j:j + 1, :]
        acc = term if acc is None else acc + term
    qkv = acc.reshape(R, DN_CONV_CH)
    qkv = qkv * _sigmoid(qkv)

    bg = bg_ref[...]
    row = lax.broadcasted_iota(jnp.int32, (R, R), 0)
    col = lax.broadcasted_iota(jnp.int32, (R, R), 1)
    if nseq > 1:
        shift = C.bit_length() - 1
        same = lax.shift_right_logical(row, shift) == lax.shift_right_logical(col, shift)
        tril = same & (col <= row)
        strict = same & (col < row)
    else:
        tril = col <= row
        strict = col < row
    gc = _dot_exact_lhs(jnp.where(tril, 1.0, 0.0).astype(BF16), bg)
    gct = gc.T
    dnn = dnn_ref[...]
    zg = z_ref[...]

    HR = range(DN_HEADS)
    hsl = [slice(h * DN_KDIM, (h + 1) * DN_KDIM) for h in HR]
    qh = [qkv[:, h * DN_KDIM:(h + 1) * DN_KDIM] for h in HR]
    kh = [qkv[:, DN_QK + h * DN_KDIM:DN_QK + (h + 1) * DN_KDIM] for h in HR]
    vh = [qkv[:, 2 * DN_QK + h * DN_VDIM:2 * DN_QK + (h + 1) * DN_VDIM] for h in HR]
    qn = [x * lax.rsqrt(jnp.sum(x * x, axis=-1, keepdims=True) + EPS) * (DN_KDIM ** -0.5) for x in qh]
    kn = [x * lax.rsqrt(jnp.sum(x * x, axis=-1, keepdims=True) + EPS) for x in kh]
    beta = [bg[:, h:h + 1] for h in HR]
    gcol = [gc[:, DN_HEADS + h:DN_HEADS + h + 1] for h in HR]
    grow = [gct[DN_HEADS + h:DN_HEADS + h + 1, :] for h in HR]
    decay = [jnp.exp(jnp.minimum(gcol[h] - grow[h], 0.0)) for h in HR]
    eg = [jnp.exp(gcol[h]) for h in HR]
    kb = [kn[h] * beta[h] for h in HR]
    vb = [vh[h] * beta[h] for h in HR]
    a2 = [_mm("kk", jnp.concatenate([kb[h], qn[h]], axis=0), kn[h], _NT) for h in HR]
    lmat = [jnp.where(strict, a2[h][:R] * decay[h], 0.0) for h in HR]
    attn = [jnp.where(tril, a2[h][R:] * decay[h], 0.0) for h in HR]
    tn = _inv_unit_lower_minus_eye(lmat, C)
    rhs = [jnp.concatenate([vb[h], kb[h] * eg[h]], axis=1) for h in HR]
    sol = [rhs[h] + _mm("sol", tn[h], rhs[h]) for h in HR]
    u = [x[:, :DN_VDIM] for x in sol]
    w = [x[:, DN_VDIM:] for x in sol]
    qg = [qn[h] * eg[h] for h in HR]
    vnew = [[None] * nseq for _ in HR]
    qs = [[None] * nseq for _ in HR]
    for s in range(nseq):
        rs = slice(s * C, (s + 1) * C)
        st = [s_s[s, h] for h in HR]
        ws = [_mm("ws", jnp.concatenate([w[h][rs], qg[h][rs]], axis=0), st[h]) for h in HR]
        for h in HR:
            vnew[h][s] = u[h][rs] - ws[h][:C]
            qs[h][s] = ws[h][C:]
        glast = [gcol[h][(s + 1) * C - 1:(s + 1) * C, :] for h in HR]
        kd = [kn[h][rs] * jnp.exp(glast[h] - gcol[h][rs]) for h in HR]
        upd = [_mm("state", kd[h], vnew[h][s], _TN) for h in HR]
        for h in HR:
            s_s[s, h] = st[h] * jnp.exp(glast[h]) + upd[h]
    cat = lambda parts: parts[0] if nseq == 1 else jnp.concatenate(parts, axis=0)
    o = [cat(qs[h]) + _mm("attn", attn[h], cat(vnew[h])) for h in HR]
    for h in HR:
        on = o[h] * lax.rsqrt(jnp.mean(o[h] * o[h], axis=-1, keepdims=True) + EPS) * dnn
        bo_ref[:, hsl[h]] = (on * zg[:, hsl[h]].astype(F32)).astype(bo_ref.dtype)

    @pl.when(c == nc - 1)
    def _():
        sn_ref[...] = s_s[...]


def _deltanet(zqkv, sz, bg, conv_init, s0, conv_w, dn_norm, nseq, C, nc):
    M = zqkv.shape[0]
    R = nseq * C
    nb = M // (R * nc)
    rmap = lambda col: (lambda b, c: (b * nc + c, col))
    kern = functools.partial(_deltanet_kernel, nseq=nseq, C=C)
    return pl.pallas_call(
        kern,
        grid=(nb, nc),
        in_specs=[pl.BlockSpec((R, DN_QK), rmap(0)),
                  pl.BlockSpec((R, DN_QK), rmap(1)),
                  pl.BlockSpec((R, DN_V), rmap(2)),
                  pl.BlockSpec((R, DN_V), rmap(0)),
                  pl.BlockSpec((R, LANES), rmap(0)),
                  pl.BlockSpec((nseq, SUBLANES, DN_CONV_CH), lambda b, c: (b, 0, 0)),
                  pl.BlockSpec((nseq, DN_HEADS, DN_KDIM, DN_VDIM), lambda b, c: (b, 0, 0, 0)),
                  pl.BlockSpec((CONV_W, DN_CONV_CH), lambda b, c: (0, 0)),
                  pl.BlockSpec((1, DN_VDIM), lambda b, c: (0, 0))],
        out_specs=[pl.BlockSpec((R, DN_V), rmap(0)),
                   pl.BlockSpec((nseq, DN_HEADS, DN_KDIM, DN_VDIM), lambda b, c: (b, 0, 0, 0))],
        out_shape=[jax.ShapeDtypeStruct((M, DN_V), BF16),
                   jax.ShapeDtypeStruct(s0.shape, F32)],
        scratch_shapes=[pltpu.VMEM((nseq, C + SUBLANES, DN_CONV_CH), F32),
                        pltpu.VMEM((nseq, DN_HEADS, DN_KDIM, DN_VDIM), F32)],
        compiler_params=_params(("arbitrary", "arbitrary")),
        name="deltanet",
    )(zqkv, zqkv, zqkv, sz, bg, conv_init, s0, conv_w, dn_norm.reshape(1, DN_VDIM))


def _outproj_kernel(u_ref, vn_ref, ga_ref, gb_ref, bo_ref, x_ref, mod_ref, ws_ref, bs_ref, wpa_ref, wpb_ref,
                    wo_ref, n2_ref, wr_ref, br_ref, before_ref, x1_ref, h2_ref, rt_ref, cnt_ref, cnt_s):
    tm = u_ref.shape[0]
    vn = vn_ref[...].astype(BF16)
    parts = []
    for r in range(tm // GMLP_CHUNK):
        rs = slice(r * GMLP_CHUNK, (r + 1) * GMLP_CHUNK)
        row_parts = []
        for g in range(A_GROUPS):
            gs = slice(g * A_GDIM, (g + 1) * A_GDIM)
            row_parts.append(jnp.dot(ws_ref[g], vn[rs, gs], preferred_element_type=F32))
        parts.append(jnp.concatenate(row_parts, axis=1) + bs_ref[...])
    sv = parts[0] if len(parts) == 1 else jnp.concatenate(parts, axis=0)
    a_out = u_ref[...].astype(F32) * sv
    pa = _dot(a_out, wpa_ref[...])
    pb = _dot(bo_ref[...], wpb_ref[...])
    m = ga_ref[...].astype(F32) * pa + gb_ref[...].astype(F32) * pb
    mix = _dot(m, wo_ref[...])
    mod = mod_ref[...]
    x = x_ref[...]
    x1 = x + mod[:, 2:3, :] * mix.reshape(x.shape)
    x1_ref[...] = x1
    y = x1 * lax.rsqrt(jnp.mean(x1 * x1, axis=-1, keepdims=True) + EPS) * n2_ref[...]
    h2 = (y * (1.0 + mod[:, 4:5, :]) + mod[:, 3:4, :]).reshape(tm, D_MODEL)
    for j in range(SUBLANES):
        h2_ref[pl.ds(j, tm, stride=SUBLANES), :] = h2[:, j * LANES:(j + 1) * LANES]

    hh, hl = _hilo(h2)
    wr = wr_ref[...]
    p = jnp.dot(hh, wr, preferred_element_type=F32)
    logits = p[:, :LANES] + (p[:, LANES:] + jnp.dot(hl, wr[:, :LANES], preferred_element_type=F32)) + br_ref[...]
    lane = lax.broadcasted_iota(jnp.int32, logits.shape, 1)
    lanef = lane.astype(F32)
    neg = jnp.float32(-jnp.inf)
    big = jnp.float32(1e9)
    gl = jnp.where(lane < N_GROUPS, logits, neg)
    gmax = jnp.max(gl, axis=-1, keepdims=True)
    gsel = jnp.min(jnp.where(gl == gmax, lanef, big), axis=-1, keepdims=True)
    gp = 1.0 / jnp.sum(jnp.exp(gl - gmax), axis=-1, keepdims=True)
    lo = ROUTER_LANE0 + EXP_PER_GROUP * gsel
    in_grp = (lanef >= lo) & (lanef < lo + EXP_PER_GROUP)
    el = jnp.where(in_grp, logits, neg)
    m1 = jnp.max(el, axis=-1, keepdims=True)
    i1 = jnp.min(jnp.where(el == m1, lanef, big), axis=-1, keepdims=True)
    el2 = jnp.where(lanef == i1, neg, el)
    m2 = jnp.max(el2, axis=-1, keepdims=True)
    i2 = jnp.min(jnp.where(el2 == m2, lanef, big), axis=-1, keepdims=True)
    ex = jnp.exp(m2 - m1)
    w1 = gp / (1.0 + ex)
    w2 = gp * ex / (1.0 + ex)
    e1 = i1 - ROUTER_LANE0
    e2 = i2 - ROUTER_LANE0

    @pl.when(pl.program_id(0) == 0)
    def _():
        cnt_s[...] = jnp.zeros_like(cnt_s)

    oh1 = jnp.where(lanef == e1, 1.0, 0.0)
    oh2 = jnp.where(lanef == e2, 1.0, 0.0)
    oh = oh1 + oh2
    seen = jnp.dot(before_ref[...], oh.astype(BF16), preferred_element_type=F32) + cnt_s[...]
    r1 = jnp.sum(oh1 * seen, axis=-1, keepdims=True)
    r2 = jnp.sum(oh2 * seen, axis=-1, keepdims=True)
    cnt_s[...] += jnp.sum(oh, axis=0, keepdims=True)
    cnt_ref[...] = cnt_s[...]
    rec = jnp.zeros_like(logits)
    for k, val in enumerate((e1, e2, r1, r2, w1, w2)):
        rec = jnp.where(lane == k, val, rec)
    rt_ref[...] = rec


def _outproj(u, vn, ga, gb, bo, x, mod, ws_eff, bs_full, w_pa, w_pb, w_o, norm2, w_r, b_r, bb, tt):
    B, T, _ = x.shape
    tm = bb * tt
    tpb = T // tt
    n_tiles = (B // bb) * tpb
    M = B * T
    cmap = lambda col: (lambda i: (i, col))
    full2 = lambda i: (0, 0)
    return pl.pallas_call(
        _outproj_kernel,
        grid=(n_tiles,),
        in_specs=[pl.BlockSpec((tm, A_WIDTH), cmap(0)),
                  pl.BlockSpec((tm, A_WIDTH), cmap(0)),
                  pl.BlockSpec((tm, D_MODEL), cmap(0)),
                  pl.BlockSpec((tm, D_MODEL), cmap(0)),
                  pl.BlockSpec((tm, DN_V), cmap(0)),
                  pl.BlockSpec((bb, tt, D_MODEL), lambda i: (i // tpb, i % tpb, 0)),
                  pl.BlockSpec((bb, N_MOD, D_MODEL), lambda i: (i // tpb, 0, 0)),
                  pl.BlockSpec((A_GROUPS, GMLP_CHUNK, GMLP_CHUNK), lambda i: (0, 0, 0)),
                  pl.BlockSpec((GMLP_CHUNK, A_WIDTH), full2),
                  pl.BlockSpec((A_WIDTH, D_MODEL), full2),
                  pl.BlockSpec((DN_V, D_MODEL), full2),
                  pl.BlockSpec((D_MODEL, D_MODEL), full2),
                  pl.BlockSpec((1, D_MODEL), full2),
                  pl.BlockSpec((D_MODEL, 2 * LANES), full2),
                  pl.BlockSpec((1, LANES), full2),
                  pl.BlockSpec((tm, tm), full2)],
        out_specs=[pl.BlockSpec((bb, tt, D_MODEL), lambda i: (i // tpb, i % tpb, 0)),
                   pl.BlockSpec((tm * SUBLANES, LANES), cmap(0)),
                   pl.BlockSpec((tm, LANES), cmap(0)),
                   pl.BlockSpec((1, LANES), full2)],
        out_shape=[jax.ShapeDtypeStruct((B, T, D_MODEL), F32),
                   jax.ShapeDtypeStruct((M * SUBLANES, LANES), F32),
                   jax.ShapeDtypeStruct((M, LANES), F32),
                   jax.ShapeDtypeStruct((1, LANES), F32)],
        scratch_shapes=[pltpu.VMEM((1, LANES), F32)],
        compiler_params=_params(("arbitrary",)),
        name="out_proj",
    )(u, vn, ga, gb, bo, x, mod, ws_eff, bs_full, w_pa, w_pb, w_o, norm2.reshape(1, D_MODEL), w_r, b_r,
      jnp.tril(jnp.ones((tm, tm), BF16), -1))


def _tile_copy(src_ref, src_off, dst_ref, dst_off, sem):
    return pltpu.make_async_copy(src_ref.at[pl.ds(pl.multiple_of(src_off, SUBLANES), SUBLANES), :],
                                 dst_ref.at[pl.ds(pl.multiple_of(dst_off, SUBLANES), SUBLANES), :], sem)


def _to_matrix(ref, n):
    return jnp.concatenate([ref[pl.ds(j, n, stride=SUBLANES), :] for j in range(SUBLANES)], axis=1)


def _dispatch_kernel(pos_ref, h_ref, xs_in_ref, xs_ref, sem):
    del xs_in_ref
    tm = h_ref.shape[0] // SUBLANES

    def issue(t, carry):
        for k in range(2):
            _tile_copy(h_ref, t * SUBLANES, xs_ref, pos_ref[0, k, t], sem).start(priority=k)
        return carry

    lax.fori_loop(0, tm, issue, 0, unroll=8)

    def drain(t, carry):
        for k in range(2):
            _tile_copy(h_ref, 0, xs_ref, 0, sem).wait()
        return carry

    lax.fori_loop(0, tm, drain, 0, unroll=8)


def _dispatch(h2, pos, n_rows, tm):
    n_tiles = h2.shape[0] // (tm * SUBLANES)
    xs0 = jnp.zeros((n_rows * SUBLANES, LANES), F32)
    return pl.pallas_call(
        _dispatch_kernel,
        grid=(n_tiles,),
        in_specs=[pl.BlockSpec((1, 2, tm), lambda i: (i, 0, 0), memory_space=pltpu.SMEM),
                  pl.BlockSpec((tm * SUBLANES, LANES), lambda i: (i, 0)),
                  pl.BlockSpec(memory_space=pl.ANY)],
        out_specs=pl.BlockSpec(memory_space=pl.ANY),
        out_shape=jax.ShapeDtypeStruct((n_rows * SUBLANES, LANES), F32),
        scratch_shapes=[pltpu.SemaphoreType.DMA(())],
        input_output_aliases={2: 0},
        compiler_params=_params(("arbitrary",)),
        name="moe_dispatch",
    )(pos, h2, xs0)


def _expert_kernel(te_ref, na_ref, x_ref, wg_ref, wu_ref, wd_ref, o_ref):
    del te_ref
    live = pl.program_id(0) < na_ref[0]

    @pl.when(live)
    def _():
        x = _to_matrix(x_ref, EXPERT_TILE).astype(BF16)
        a = jnp.dot(x, wg_ref[0], preferred_element_type=F32)
        u = jnp.dot(x, wu_ref[0], preferred_element_type=F32)
        o = _dot(a * _sigmoid(a) * u, wd_ref[0])
        for j in range(SUBLANES):
            o_ref[pl.ds(j, EXPERT_TILE, stride=SUBLANES), :] = o[:, j * LANES:(j + 1) * LANES]

    @pl.when(jnp.logical_not(live))
    def _():
        o_ref[...] = jnp.zeros_like(o_ref)


def _experts(xs, tile_expert, n_active, wg, wu, wd):
    blk = EXPERT_TILE * SUBLANES
    n_tiles = xs.shape[0] // blk
    live = lambda i, na: jnp.minimum(i, jnp.maximum(na[0] - 1, 0))
    wmap = lambda i, te, na: (te[live(i, na)], 0, 0)
    grid_spec = pltpu.PrefetchScalarGridSpec(
        num_scalar_prefetch=2,
        grid=(n_tiles,),
        in_specs=[pl.BlockSpec((blk, LANES), lambda i, te, na: (live(i, na), 0)),
                  pl.BlockSpec((1, D_MODEL, D_EXPERT), wmap),
                  pl.BlockSpec((1, D_MODEL, D_EXPERT), wmap),
                  pl.BlockSpec((1, D_EXPERT, D_MODEL), wmap)],
        out_specs=pl.BlockSpec((blk, LANES), lambda i, te, na: (i, 0)),
    )
    return pl.pallas_call(
        _expert_kernel,
        grid_spec=grid_spec,
        out_shape=jax.ShapeDtypeStruct(xs.shape, F32),
        compiler_params=_params(("arbitrary",)),
        name="moe_experts",
    )(tile_expert, n_active, xs, wg, wu, wd)


def _combine_kernel(pos_ref, rt_ref, x1_ref, mod_ref, fn_ref, ys_ref, y_ref, g_s, sem):
    tm = rt_ref.shape[0]

    def issue(t, carry):
        for k in range(2):
            _tile_copy(ys_ref, pos_ref[0, k, t], g_s.at[k], t * SUBLANES, sem).start(priority=k)
        return carry

    lax.fori_loop(0, tm, issue, 0, unroll=8)

    def drain(t, carry):
        for k in range(2):
            _tile_copy(ys_ref, 0, g_s.at[k], 0, sem).wait()
        return carry

    lax.fori_loop(0, tm, drain, 0, unroll=8)

    rt = rt_ref[...]
    moe = rt[:, 4:5] * _to_matrix(g_s.at[0], tm) + rt[:, 5:6] * _to_matrix(g_s.at[1], tm)
    mod = mod_ref[...]
    x1 = x1_ref[...]
    x2 = x1 + mod[:, 5:6, :] * moe.reshape(x1.shape)
    y_ref[...] = x2 * lax.rsqrt(jnp.mean(x2 * x2, axis=-1, keepdims=True) + EPS) * fn_ref[...]


def _combine(ys, pos, route, x1, mod, final_norm, bb, tt):
    B, T, _ = x1.shape
    tm = bb * tt
    tpb = T // tt
    n_tiles = (B // bb) * tpb
    xmap = lambda i: (i // tpb, i % tpb, 0)
    return pl.pallas_call(
        _combine_kernel,
        grid=(n_tiles,),
        in_specs=[pl.BlockSpec((1, 2, tm), lambda i: (i, 0, 0), memory_space=pltpu.SMEM),
                  pl.BlockSpec((tm, LANES), lambda i: (i, 0)),
                  pl.BlockSpec((bb, tt, D_MODEL), xmap),
                  pl.BlockSpec((bb, N_MOD, D_MODEL), lambda i: (i // tpb, 0, 0)),
                  pl.BlockSpec((1, D_MODEL), lambda i: (0, 0)),
                  pl.BlockSpec(memory_space=pl.ANY)],
        out_specs=pl.BlockSpec((bb, tt, D_MODEL), xmap),
        out_shape=jax.ShapeDtypeStruct((B, T, D_MODEL), F32),
        scratch_shapes=[pltpu.VMEM((2, tm * SUBLANES, LANES), F32), pltpu.SemaphoreType.DMA(())],
        compiler_params=_params(("arbitrary",)),
        name="moe_combine",
    )(pos, route, x1, mod, final_norm.reshape(1, D_MODEL), ys)


def _moe(h2, route, counts, wg, wu, wd, x1, mod, final_norm, bb, tt):
    M = route.shape[0]
    tm = bb * tt
    n_tiles_tok = M // tm
    max_tiles = (2 * M + N_EXPERTS * (EXPERT_TILE - 1)) // EXPERT_TILE
    cnt = counts[0, :N_EXPERTS].astype(jnp.int32)
    nt = (cnt + EXPERT_TILE - 1) // EXPERT_TILE
    ends = jnp.cumsum(nt)
    base = ((ends - nt) * EXPERT_TILE).astype(jnp.int32)
    n_active = ends[-1:].astype(jnp.int32)
    tile_expert = jnp.minimum(jnp.sum(jnp.arange(max_tiles)[:, None] >= ends[None, :], axis=1),
                              N_EXPERTS - 1).astype(jnp.int32)
    eid = route[:, 0:2].astype(jnp.int32)
    rank = route[:, 2:4].astype(jnp.int32)
    pos = (base[eid] + rank) * SUBLANES
    pos = jnp.transpose(pos.reshape(n_tiles_tok, tm, 2), (0, 2, 1))
    xs = _dispatch(h2, pos, max_tiles * EXPERT_TILE, tm)
    ys = _experts(xs, tile_expert, n_active, wg, wu, wd)
    return _combine(ys, pos, route, x1, mod, final_norm, bb, tt)


def _layer(x, mod, conv_init, s0, prm, final_norm, *, bb, tt, dn_nseq, dn_chunk, vn_dtype):
    B, T, _ = x.shape
    u, vn, zqkv, sz, ga, gb, bg = _inproj(x, mod, prm["norm1"], prm["w_main"], prm["w_bg"], prm["ln_v_w"],
                                          prm["ln_v_b"], prm["adt"], bb, tt, vn_dtype)
    nc = T // dn_chunk
    bo, s_new = _deltanet(zqkv, sz, bg, conv_init, s0, prm["conv_w"], prm["dn_norm"], dn_nseq, dn_chunk, nc)
    L = min(GMLP_CHUNK, T)
    reps = GMLP_CHUNK // L
    wm = prm["w_s_tril"][:, :L, :L]
    if reps > 1:
        blk = (jnp.arange(GMLP_CHUNK)[:, None] // L) == (jnp.arange(GMLP_CHUNK)[None, :] // L)
        wm = jnp.where(blk[None], jnp.tile(wm, (1, reps, reps)), 0.0)
    ws_eff = wm.astype(BF16)
    bs_rows = jnp.tile(jnp.transpose(prm["b_s"][:, :L]), (reps, 1))
    bs_full = jnp.repeat(bs_rows, A_GDIM, axis=1)
    x1, h2, route, counts = _outproj(u, vn, ga, gb, bo, x, mod, ws_eff, bs_full, prm["w_pa"], prm["w_pb"],
                                     prm["w_o"], prm["norm2"], prm["w_r"], prm["b_r"], bb, tt)
    y = _moe(h2, route, counts, prm["w_e_gate"], prm["w_e_up"], prm["w_e_down"], x1, mod, final_norm, bb, tt)
    tail = min(T, CONV_W - 1)
    zqkv_tail = zqkv.reshape(B, T, DN_CONV_CH)[:, T - tail:, :]
    conv_new = jnp.concatenate([conv_init[:, SUBLANES - (CONV_W - 1):, :], zqkv_tail], axis=1)[:, -(CONV_W - 1):, :]
    return y, conv_new, s_new, vn.reshape(B, T, A_WIDTH)


def kernel(x_prompt, x_sample, c_prompt, c_sample, state_conv, state_delta, w_ada, b_ada, norm1, norm2, w_in, conv_w, a_log, dt_bias, dn_norm, ln_v_w, ln_v_b, w_s, b_s, w_pa, w_pb, w_o, w_rg, b_rg, w_re, b_re, w_e_gate, w_e_up, w_e_down, final_norm):
    depth = w_ada.shape[0]
    assert depth == 1
    bp, tp, _ = x_prompt.shape
    bs_, ts, _ = x_sample.shape
    l = 0
    wi = w_in[l]
    o_qkv = 2 * A_WIDTH
    o_z = o_qkv + DN_CONV_CH
    o_b = o_z + DN_V
    o_ga = o_b + 2 * DN_HEADS
    w_main = jnp.concatenate([wi[:, :o_b], wi[:, o_ga:]], axis=1).astype(BF16)
    w_bg = jnp.pad(wi[:, o_b:o_ga], ((0, 0), (0, LANES - 2 * DN_HEADS))).astype(BF16)
    adt = jnp.stack([jnp.pad(a_log[l], (DN_HEADS, LANES - 2 * DN_HEADS)),
                     jnp.pad(dt_bias[l], (DN_HEADS, LANES - 2 * DN_HEADS))])
    w_r = jnp.pad(jnp.concatenate([w_rg[l], w_re[l]], axis=1), ((0, 0), (0, LANES - N_GROUPS - N_EXPERTS)))
    w_r_hi = w_r.astype(BF16)
    w_r = jnp.concatenate([w_r_hi, (w_r - w_r_hi.astype(F32)).astype(BF16)], axis=1)
    b_r = jnp.pad(jnp.concatenate([b_rg[l], b_re[l]]), (0, LANES - N_GROUPS - N_EXPERTS)).reshape(1, LANES)
    tri = jnp.tril(jnp.ones((GMLP_CHUNK, GMLP_CHUNK), F32))
    prm = dict(
        norm1=norm1[l], norm2=norm2[l], w_main=w_main, w_bg=w_bg, adt=adt, ln_v_w=ln_v_w[l], ln_v_b=ln_v_b[l],
        conv_w=conv_w[l], dn_norm=dn_norm[l], w_s_tril=w_s[l] * tri, b_s=b_s[l],
        w_pa=w_pa[l].astype(BF16), w_pb=w_pb[l].astype(BF16), w_o=w_o[l].astype(BF16), w_r=w_r, b_r=b_r,
        w_e_gate=w_e_gate[l].reshape(N_EXPERTS, D_MODEL, D_EXPERT).astype(BF16),
        w_e_up=w_e_up[l].reshape(N_EXPERTS, D_MODEL, D_EXPERT).astype(BF16),
        w_e_down=w_e_down[l].reshape(N_EXPERTS, D_EXPERT, D_MODEL).astype(BF16),
    )

    c_all = jnp.concatenate([c_prompt, c_sample], axis=0)
    mod = _ada(c_all, w_ada[l], b_ada[l]).reshape(bp + bs_, N_MOD, D_MODEL)
    mod_p, mod_s = mod[:bp], mod[bp:]

    pad_rows = SUBLANES - (CONV_W - 1)
    cinit_p = jnp.zeros((bp, SUBLANES, DN_CONV_CH), F32)
    cinit_s = jnp.pad(state_conv[l], ((0, 0), (pad_rows, 0), (0, 0)))
    s0_p = jnp.zeros((bp, DN_HEADS, DN_KDIM, DN_VDIM), F32)

    yp, conv_p, delta_p, _ = _layer(x_prompt, mod_p, cinit_p, s0_p, prm, final_norm,
                                    bb=1, tt=min(ROW_TILE, tp), dn_nseq=1, dn_chunk=min(DN_CHUNK, tp),
                                    vn_dtype=BF16)
    ys, conv_s, delta_s, vn_s = _layer(x_sample, mod_s, cinit_s, state_delta[l], prm, final_norm,
                                       bb=min(bs_, ROW_TILE // ts), tt=ts,
                                       dn_nseq=min(bs_, DN_CHUNK // ts), dn_chunk=ts, vn_dtype=F32)
    return (yp, ys, conv_p[None], conv_s[None], delta_p[None], delta_s[None], vn_s[None])
```

```python
import functools

import jax
import jax.numpy as jnp
from jax import lax
from jax.experimental import pallas as pl
from jax.experimental.pallas import tpu as pltpu

F32 = jnp.float32
BF16 = jnp.bfloat16

D_MODEL = 1024
A_WIDTH = 1024
A_GROUPS = 4
A_GDIM = A_WIDTH // A_GROUPS
GMLP_CHUNK = 128
DN_HEADS = 8
DN_KDIM = 128
DN_VDIM = 128
DN_QK = DN_HEADS * DN_KDIM
DN_V = DN_HEADS * DN_VDIM
DN_CONV_CH = 2 * DN_QK + DN_V
CONV_W = 4
DN_CHUNK = 64
N_GROUPS = 4
EXP_PER_GROUP = 8
N_EXPERTS = N_GROUPS * EXP_PER_GROUP
D_EXPERT = 256
N_MOD = 6
EPS = 1e-6

LANES = 128
SUBLANES = 8
N_MAIN_BLOCKS = 8
ROUTER_LANE0 = N_GROUPS
VMEM_LIMIT = 56 * 1024 * 1024
ROW_TILE = 512
EXPERT_TILE = 256


def _sigmoid(x):
    return 1.0 / (1.0 + jnp.exp(-x))


def _gelu(x):
    return 0.5 * x * (1.0 + lax.erf(x * 0.7071067811865476))


def _softplus(x):
    return jnp.maximum(x, 0.0) + jnp.log(1.0 + jnp.exp(-jnp.abs(x)))


def _dot(a, b):
    return jnp.dot(a.astype(BF16), b.astype(BF16), preferred_element_type=F32)


def _dot_nt(a, b):
    return lax.dot_general(a.astype(BF16), b.astype(BF16), (((1,), (1,)), ((), ())), preferred_element_type=F32)


def _dot_tn(a, b):
    return lax.dot_general(a.astype(BF16), b.astype(BF16), (((0,), (0,)), ((), ())), preferred_element_type=F32)


def _split3(x):
    hi = x.astype(BF16)
    r1 = x - hi.astype(F32)
    mid = r1.astype(BF16)
    lo = (r1 - mid.astype(F32)).astype(BF16)
    return hi, mid, lo


def _dot_exact_lhs(a_bf16, x):
    hi, mid, lo = _split3(x)
    f = lambda p: jnp.dot(a_bf16, p, preferred_element_type=F32)
    return f(hi) + f(mid) + f(lo)


_PASSES = {"invp": 3, "invn": 3, "inv_hi": 1, "inv_kmax": 8, "sol": 1, "kk": 1, "ws": 1, "attn": 1, "state": 1}


def _hilo(x):
    hi = x.astype(BF16)
    lo = (x - hi.astype(F32)).astype(BF16)
    return hi, lo


def _mm(site, a, b, dims=(((1,), (0,)), ((), ()))):
    f = lambda p, q: lax.dot_general(p, q, dims, preferred_element_type=F32)
    if _PASSES[site] == 1:
        return f(a.astype(BF16), b.astype(BF16))
    ah, al = _hilo(a)
    bh, bl = _hilo(b)
    return f(ah, bh) + (f(ah, bl) + f(al, bh))


_NT = (((1,), (1,)), ((), ()))
_TN = (((0,), (0,)), ((), ()))

def _params(sem):
    return pltpu.CompilerParams(dimension_semantics=sem, vmem_limit_bytes=VMEM_LIMIT)


def _ada_kernel(c_ref, w_ref, b_ref, o_ref):
    c = c_ref[...]
    o_ref[...] = _dot(c * _sigmoid(c), w_ref[...]) + b_ref[...]


def _ada(c_all, w_ada, b_ada):
    n = c_all.shape[0]
    width = w_ada.shape[1]
    bn = 512
    return pl.pallas_call(
        _ada_kernel,
        grid=(width // bn,),
        in_specs=[pl.BlockSpec((n, D_MODEL), lambda j: (0, 0)),
                  pl.BlockSpec((D_MODEL, bn), lambda j: (0, j)),
                  pl.BlockSpec((1, bn), lambda j: (0, j))],
        out_specs=pl.BlockSpec((n, bn), lambda j: (0, j)),
        out_shape=jax.ShapeDtypeStruct((n, width), F32),
        compiler_params=_params(("arbitrary",)),
        name="ada_mod",
    )(c_all, w_ada, b_ada.reshape(1, width))


def _inproj_kernel(x_ref, mod_ref, n1_ref, w_ref, wbg_ref, lnw_ref, lnb_ref, adt_ref,
                   u_ref, vn_ref, qkv_ref, sz_ref, ga_ref, gb_ref, bg_ref):
    tm = u_ref.shape[0]
    x = x_ref[...]
    y = x * lax.rsqrt(jnp.mean(x * x, axis=-1, keepdims=True) + EPS) * n1_ref[...]
    mod = mod_ref[...]
    h = y * (1.0 + mod[:, 1:2, :]) + mod[:, 0:1, :]
    hb = h.reshape(tm, D_MODEL).astype(BF16)
    zbg = jnp.dot(hb, wbg_ref[...], preferred_element_type=F32)
    lane = lax.broadcasted_iota(jnp.int32, zbg.shape, 1)
    adt = adt_ref[...]
    g = -jnp.exp(adt[0:1, :]) * _softplus(zbg + adt[1:2, :])
    bg_ref[...] = jnp.where(lane < DN_HEADS, _sigmoid(zbg), g)

    blk = lambda j: jnp.dot(hb, w_ref[:, j * 1024:(j + 1) * 1024], preferred_element_type=F32)
    u_ref[...] = _gelu(blk(0)).astype(u_ref.dtype)
    a = _gelu(blk(1))
    ac = a - jnp.mean(a, axis=-1, keepdims=True)
    var = jnp.mean(ac * ac, axis=-1, keepdims=True)
    vn_ref[...] = (ac * lax.rsqrt(var + EPS) * lnw_ref[...] + lnb_ref[...]).astype(vn_ref.dtype)
    for j in range(3):
        qkv_ref[:, j * 1024:(j + 1) * 1024] = blk(2 + j)
    z = blk(5)
    sz_ref[...] = (z * _sigmoid(z)).astype(sz_ref.dtype)
    ga_ref[...] = _sigmoid(blk(6)).astype(ga_ref.dtype)
    gb_ref[...] = _sigmoid(blk(7)).astype(gb_ref.dtype)


def _inproj(x, mod, norm1, w_main, w_bg, ln_w, ln_b, adt, bb, tt, vn_dtype):
    B, T, _ = x.shape
    tm = bb * tt
    tpb = T // tt
    n_tiles = (B // bb) * tpb
    M = B * T
    const = lambda i: (0, 0)
    rows = lambda i: (i, 0)
    resident = dict(pipeline_mode=pl.Buffered(1))
    in_specs = [pl.BlockSpec((bb, tt, D_MODEL), lambda i: (i // tpb, i % tpb, 0)),
                pl.BlockSpec((bb, N_MOD, D_MODEL), lambda i: (i // tpb, 0, 0)),
                pl.BlockSpec((1, D_MODEL), const),
                pl.BlockSpec((D_MODEL, N_MAIN_BLOCKS * 1024), const, **resident),
                pl.BlockSpec((D_MODEL, LANES), const, **resident),
                pl.BlockSpec((1, A_WIDTH), const),
                pl.BlockSpec((1, A_WIDTH), const),
                pl.BlockSpec((2, LANES), const)]
    args = [x, mod, norm1.reshape(1, D_MODEL), w_main, w_bg, ln_w.reshape(1, A_WIDTH), ln_b.reshape(1, A_WIDTH), adt]
    out_specs = [pl.BlockSpec((tm, A_WIDTH), rows),
                 pl.BlockSpec((tm, A_WIDTH), rows),
                 pl.BlockSpec((tm, DN_CONV_CH), rows),
                 pl.BlockSpec((tm, DN_V), rows),
                 pl.BlockSpec((tm, D_MODEL), rows),
                 pl.BlockSpec((tm, D_MODEL), rows),
                 pl.BlockSpec((tm, LANES), rows)]
    out_shape = [jax.ShapeDtypeStruct((M, A_WIDTH), BF16),
                 jax.ShapeDtypeStruct((M, A_WIDTH), vn_dtype),
                 jax.ShapeDtypeStruct((M, DN_CONV_CH), F32),
                 jax.ShapeDtypeStruct((M, DN_V), BF16),
                 jax.ShapeDtypeStruct((M, D_MODEL), BF16),
                 jax.ShapeDtypeStruct((M, D_MODEL), BF16),
                 jax.ShapeDtypeStruct((M, LANES), F32)]
    return pl.pallas_call(
        _inproj_kernel,
        grid=(n_tiles,),
        in_specs=in_specs,
        out_specs=out_specs,
        out_shape=out_shape,
        compiler_params=_params(("arbitrary",)),
        name="in_proj",
    )(*args)


def _inv_unit_lower_minus_eye(lmats, nilpotent):
    ns = [-l for l in lmats]
    ps = list(lmats)
    k = 2
    while k < nilpotent:
        early = k <= _PASSES["inv_kmax"]
        ps = [_mm("invp" if early else "inv_hi", p, p) for p in ps]
        ns = [n + p + _mm("invn" if early else "inv_hi", n, p) for n, p in zip(ns, ps)]
        k *= 2
    return ns


def _head_sumsq(x):
    pair = 2 * DN_KDIM
    r = lax.broadcasted_iota(jnp.int32, (pair, pair), 0) >= DN_KDIM
    c = lax.broadcasted_iota(jnp.int32, (pair, pair), 1) >= DN_KDIM
    ones2 = jnp.where(r == c, 1.0, 0.0).astype(BF16)
    sq = (x * x).astype(BF16)
    return jnp.concatenate([jnp.dot(sq[:, p * pair:(p + 1) * pair], ones2, preferred_element_type=F32)
                            for p in range(x.shape[1] // pair)], axis=1)


def _deltanet_kernel(q_ref, k_ref, v_ref, z_ref, bg_ref, cinit_ref, s0_ref, cw_ref, dnn_ref,
                     bo_ref, sn_ref, xp_s, s_s, *, nprob, nseq, C):
    R = nseq * C
    c = pl.program_id(1)
    nc = pl.num_programs(1)

    @pl.when(c == 0)
    def _():
        xp_s[:, 0:SUBLANES, :] = cinit_ref[...]
        s_s[...] = s0_ref[...]

    @pl.when(c > 0)
    def _():
        xp_s[:, 0:SUBLANES, :] = xp_s[:, C:C + SUBLANES, :]

    for p in range(nprob):
        for s in range(nseq):
            i = p * nseq + s
            xp_s[i, SUBLANES:SUBLANES + C, 0:DN_QK] = q_ref[p, s * C:(s + 1) * C, :]
            xp_s[i, SUBLANES:SUBLANES + C, DN_QK:2 * DN_QK] = k_ref[p, s * C:(s + 1) * C, :]
            xp_s[i, SUBLANES:SUBLANES + C, 2 * DN_QK:DN_CONV_CH] = v_ref[p, s * C:(s + 1) * C, :]

    cw = cw_ref[...]
    base = SUBLANES - (CONV_W - 1)
    acc = None
    for j in range(CONV_W):
        term = xp_s[:, base + j:base + j + C, :] * cw[j:j + 1, :]
        acc = term if acc is None else acc + term
    qkv = acc.reshape(nprob * R, DN_CONV_CH)
    qkv = qkv * _sigmoid(qkv)
    q_all = qkv[:, 0:DN_QK]
    k_all = qkv[:, DN_QK:2 * DN_QK]
    v_all = qkv[:, 2 * DN_QK:DN_CONV_CH]
    qn_all = q_all * lax.rsqrt(_head_sumsq(q_all) + EPS) * (DN_KDIM ** -0.5)
    kn_all = k_all * lax.rsqrt(_head_sumsq(k_all) + EPS)

    row = lax.broadcasted_iota(jnp.int32, (R, R), 0)
    col = lax.broadcasted_iota(jnp.int32, (R, R), 1)
    if nseq > 1:
        shift = C.bit_length() - 1
        same = lax.shift_right_logical(row, shift) == lax.shift_right_logical(col, shift)
        tril = same & (col <= row)
        strict = same & (col < row)
    else:
        tril = col <= row
        strict = col < row
    tril_b = jnp.where(tril, 1.0, 0.0).astype(BF16)
    bg = [bg_ref[p] for p in range(nprob)]
    gc = [_dot_exact_lhs(tril_b, bg[p]) for p in range(nprob)]
    gct = [g.T for g in gc]
    dnn = dnn_ref[...]

    IT = [(p, h) for p in range(nprob) for h in range(DN_HEADS)]
    NI = range(len(IT))
    hsl = [slice(h * DN_KDIM, (h + 1) * DN_KDIM) for h in range(DN_HEADS)]
    rows = [slice(p * R, (p + 1) * R) for p in range(nprob)]
    qn = [qn_all[rows[p], hsl[h]] for p, h in IT]
    kn = [kn_all[rows[p], hsl[h]] for p, h in IT]
    vh = [v_all[rows[p], hsl[h]] for p, h in IT]
    beta = [bg[p][:, h:h + 1] for p, h in IT]
    gcol = [gc[p][:, DN_HEADS + h:DN_HEADS + h + 1] for p, h in IT]
    grow = [gct[p][DN_HEADS + h:DN_HEADS + h + 1, :] for p, h in IT]
    decay = [jnp.exp(jnp.minimum(gcol[i] - grow[i], 0.0)) for i in NI]
    eg = [jnp.exp(gcol[i]) for i in NI]
    kb = [kn[i] * beta[i] for i in NI]
    vb = [vh[i] * beta[i] for i in NI]
    a2 = [_mm("kk", jnp.concatenate([kb[i], qn[i]], axis=0), kn[i], _NT) for i in NI]
    lmat = [jnp.where(strict, a2[i][:R] * decay[i], 0.0) for i in NI]
    attn = [jnp.where(tril, a2[i][R:] * decay[i], 0.0) for i in NI]
    tn = _inv_unit_lower_minus_eye(lmat, C)
    rhs = [jnp.concatenate([vb[i], kb[i] * eg[i]], axis=1) for i in NI]
    sol = [rhs[i] + _mm("sol", tn[i], rhs[i]) for i in NI]
    u = [x[:, :DN_VDIM] for x in sol]
    w = [x[:, DN_VDIM:] for x in sol]
    qg = [qn[i] * eg[i] for i in NI]
    vnew = [[None] * nseq for _ in NI]
    qs = [[None] * nseq for _ in NI]
    for s in range(nseq):
        rs = slice(s * C, (s + 1) * C)
        st = [s_s[p * nseq + s, h] for p, h in IT]
        ws = [_mm("ws", jnp.concatenate([w[i][rs], qg[i][rs]], axis=0), st[i]) for i in NI]
        for i in NI:
            vnew[i][s] = u[i][rs] - ws[i][:C]
            qs[i][s] = ws[i][C:]
        glast = [gcol[i][(s + 1) * C - 1:(s + 1) * C, :] for i in NI]
        kd = [kn[i][rs] * jnp.exp(glast[i] - gcol[i][rs]) for i in NI]
        upd = [_mm("state", kd[i], vnew[i][s], _TN) for i in NI]
        for i, (p, h) in enumerate(IT):
            s_s[p * nseq + s, h] = st[i] * jnp.exp(glast[i]) + upd[i]
    cat = lambda parts: parts[0] if nseq == 1 else jnp.concatenate(parts, axis=0)
    o = [cat(qs[i]) + _mm("attn", attn[i], cat(vnew[i])) for i in NI]
    o_all = jnp.concatenate([jnp.concatenate(o[p * DN_HEADS:(p + 1) * DN_HEADS], axis=1) for p in range(nprob)],
                            axis=0)
    dnn_all = jnp.concatenate([dnn] * DN_HEADS, axis=1)
    on = o_all * lax.rsqrt(_head_sumsq(o_all) * (1.0 / DN_VDIM) + EPS) * dnn_all
    for p in range(nprob):
        bo_ref[p] = (on[rows[p]] * z_ref[p].astype(F32)).astype(bo_ref.dtype)

    @pl.when(c == nc - 1)
    def _():
        sn_ref[...] = s_s[...]


def _deltanet(zqkv, sz, bg, conv_init, s0, conv_w, dn_norm, nprob, nseq, C, nc):
    M = zqkv.shape[0]
    R = nseq * C
    G = M // (R * nc)
    view = lambda a: a.reshape(G, R * nc, a.shape[-1])
    rmap = lambda col: (lambda g, c: (g, c, col))
    nsq = nprob * nseq
    state_spec = pl.BlockSpec((nsq, DN_HEADS, DN_KDIM, DN_VDIM), lambda g, c: (g, 0, 0, 0))
    bo, s_new = pl.pallas_call(
        functools.partial(_deltanet_kernel, nprob=nprob, nseq=nseq, C=C),
        grid=(G // nprob, nc),
        in_specs=[pl.BlockSpec((nprob, R, DN_QK), rmap(0)),
                  pl.BlockSpec((nprob, R, DN_QK), rmap(1)),
                  pl.BlockSpec((nprob, R, DN_V), rmap(2)),
                  pl.BlockSpec((nprob, R, DN_V), rmap(0)),
                  pl.BlockSpec((nprob, R, LANES), rmap(0)),
                  pl.BlockSpec((nsq, SUBLANES, DN_CONV_CH), lambda g, c: (g, 0, 0)),
                  state_spec,
                  pl.BlockSpec((CONV_W, DN_CONV_CH), lambda g, c: (0, 0)),
                  pl.BlockSpec((1, DN_VDIM), lambda g, c: (0, 0))],
        out_specs=[pl.BlockSpec((nprob, R, DN_V), rmap(0)), state_spec],
        out_shape=[jax.ShapeDtypeStruct((G, R * nc, DN_V), BF16),
                   jax.ShapeDtypeStruct(s0.shape, F32)],
        scratch_shapes=[pltpu.VMEM((nsq, C + SUBLANES, DN_CONV_CH), F32),
                        pltpu.VMEM((nsq, DN_HEADS, DN_KDIM, DN_VDIM), F32)],
        compiler_params=_params(("arbitrary", "arbitrary")),
        name="deltanet",
    )(view(zqkv), view(zqkv), view(zqkv), view(sz), view(bg), conv_init, s0, conv_w, dn_norm.reshape(1, DN_VDIM))
    return bo.reshape(M, DN_V), s_new


def _outproj_kernel(u_ref, vn_ref, ga_ref, gb_ref, bo_ref, x_ref, mod_ref, ws_ref, bs_ref, wpa_ref, wpb_ref,
                    wo_ref, n2_ref, wr_ref, br_ref, before_ref, x1_ref, h2_ref, rt_ref, cnt_ref, cnt_s):
    tm = u_ref.shape[0]
    vn = vn_ref[...].astype(BF16)
    parts = []
    for r in range(tm // GMLP_CHUNK):
        rs = slice(r * GMLP_CHUNK, (r + 1) * GMLP_CHUNK)
        row_parts = []
        for g in range(A_GROUPS):
            gs = slice(g * A_GDIM, (g + 1) * A_GDIM)
            row_parts.append(jnp.dot(ws_ref[g], vn[rs, gs], preferred_element_type=F32))
        parts.append(jnp.concatenate(row_parts, axis=1) + bs_ref[...])
    sv = parts[0] if len(parts) == 1 else jnp.concatenate(parts, axis=0)
    a_out = u_ref[...].astype(F32) * sv
    pa = _dot(a_out, wpa_ref[...])
    pb = _dot(bo_ref[...], wpb_ref[...])
    m = ga_ref[...].astype(F32) * pa + gb_ref[...].astype(F32) * pb
    mix = _dot(m, wo_ref[...])
    mod = mod_ref[...]
    x = x_ref[...]
    x1 = x + mod[:, 2:3, :] * mix.reshape(x.shape)
    x1_ref[...] = x1
    y = x1 * lax.rsqrt(jnp.mean(x1 * x1, axis=-1, keepdims=True) + EPS) * n2_ref[...]
    h2 = (y * (1.0 + mod[:, 4:5, :]) + mod[:, 3:4, :]).reshape(tm, D_MODEL)
    for j in range(SUBLANES):
        h2_ref[pl.ds(j, tm, stride=SUBLANES), :] = h2[:, j * LANES:(j + 1) * LANES]

    hh, hl = _hilo(h2)
    wr = wr_ref[...]
    p = jnp.dot(hh, wr, preferred_element_type=F32)
    logits = p[:, :LANES] + (p[:, LANES:] + jnp.dot(hl, wr[:, :LANES], preferred_element_type=F32)) + br_ref[...]
    lane = lax.broadcasted_iota(jnp.int32, logits.shape, 1)
    lanef = lane.astype(F32)
    neg = jnp.float32(-jnp.inf)
    big = jnp.float32(1e9)
    gl = jnp.where(lane < N_GROUPS, logits, neg)
    gmax = jnp.max(gl, axis=-1, keepdims=True)
    gsel = jnp.min(jnp.where(gl == gmax, lanef, big), axis=-1, keepdims=True)
    gp = 1.0 / jnp.sum(jnp.exp(gl - gmax), axis=-1, keepdims=True)
    lo = ROUTER_LANE0 + EXP_PER_GROUP * gsel
    in_grp = (lanef >= lo) & (lanef < lo + EXP_PER_GROUP)
    el = jnp.where(in_grp, logits, neg)
    m1 = jnp.max(el, axis=-1, keepdims=True)
    i1 = jnp.min(jnp.where(el == m1, lanef, big), axis=-1, keepdims=True)
    el2 = jnp.where(lanef == i1, neg, el)
    m2 = jnp.max(el2, axis=-1, keepdims=True)
    i2 = jnp.min(jnp.where(el2 == m2, lanef, big), axis=-1, keepdims=True)
    ex = jnp.exp(m2 - m1)
    w1 = gp / (1.0 + ex)
    w2 = gp * ex / (1.0 + ex)
    e1 = i1 - ROUTER_LANE0
    e2 = i2 - ROUTER_LANE0

    @pl.when(pl.program_id(0) == 0)
    def _():
        cnt_s[...] = jnp.zeros_like(cnt_s)

    oh1 = jnp.where(lanef == e1, 1.0, 0.0)
    oh2 = jnp.where(lanef == e2, 1.0, 0.0)
    oh = oh1 + oh2
    seen = jnp.dot(before_ref[...], oh.astype(BF16), preferred_element_type=F32) + cnt_s[...]
    r1 = jnp.sum(oh1 * seen, axis=-1, keepdims=True)
    r2 = jnp.sum(oh2 * seen, axis=-1, keepdims=True)
    cnt_s[...] += jnp.sum(oh, axis=0, keepdims=True)
    cnt_ref[...] = cnt_s[...]
    rec = jnp.zeros_like(logits)
    for k, val in enumerate((e1, e2, r1, r2, w1, w2)):
        rec = jnp.where(lane == k, val, rec)
    rt_ref[...] = rec


def _outproj(u, vn, ga, gb, bo, x, mod, ws_eff, bs_full, w_pa, w_pb, w_o, norm2, w_r, b_r, bb, tt):
    B, T, _ = x.shape
    tm = bb * tt
    tpb = T // tt
    n_tiles = (B // bb) * tpb
    M = B * T
    cmap = lambda col: (lambda i: (i, col))
    full2 = lambda i: (0, 0)
    return pl.pallas_call(
        _outproj_kernel,
        grid=(n_tiles,),
        in_specs=[pl.BlockSpec((tm, A_WIDTH), cmap(0)),
                  pl.BlockSpec((tm, A_WIDTH), cmap(0)),
                  pl.BlockSpec((tm, D_MODEL), cmap(0)),
                  pl.BlockSpec((tm, D_MODEL), cmap(0)),
                  pl.BlockSpec((tm, DN_V), cmap(0)),
                  pl.BlockSpec((bb, tt, D_MODEL), lambda i: (i // tpb, i % tpb, 0)),
                  pl.BlockSpec((bb, N_MOD, D_MODEL), lambda i: (i // tpb, 0, 0)),
                  pl.BlockSpec((A_GROUPS, GMLP_CHUNK, GMLP_CHUNK), lambda i: (0, 0, 0)),
                  pl.BlockSpec((GMLP_CHUNK, A_WIDTH), full2),
                  pl.BlockSpec((A_WIDTH, D_MODEL), full2),
                  pl.BlockSpec((DN_V, D_MODEL), full2),
                  pl.BlockSpec((D_MODEL, D_MODEL), full2),
                  pl.BlockSpec((1, D_MODEL), full2),
                  pl.BlockSpec((D_MODEL, 2 * LANES), full2),
                  pl.BlockSpec((1, LANES), full2),
                  pl.BlockSpec((tm, tm), full2)],
        out_specs=[pl.BlockSpec((bb, tt, D_MODEL), lambda i: (i // tpb, i % tpb, 0)),
                   pl.BlockSpec((tm * SUBLANES, LANES), cmap(0)),
                   pl.BlockSpec((tm, LANES), cmap(0)),
                   pl.BlockSpec((1, LANES), full2)],
        out_shape=[jax.ShapeDtypeStruct((B, T, D_MODEL), F32),
                   jax.ShapeDtypeStruct((M * SUBLANES, LANES), F32),
                   jax.ShapeDtypeStruct((M, LANES), F32),
                   jax.ShapeDtypeStruct((1, LANES), F32)],
        scratch_shapes=[pltpu.VMEM((1, LANES), F32)],
        compiler_params=_params(("arbitrary",)),
        name="out_proj",
    )(u, vn, ga, gb, bo, x, mod, ws_eff, bs_full, w_pa, w_pb, w_o, norm2.reshape(1, D_MODEL), w_r, b_r,
      jnp.tril(jnp.ones((tm, tm), BF16), -1))


def _tile_copy(src_ref, src_off, dst_ref, dst_off, sem):
    return pltpu.make_async_copy(src_ref.at[pl.ds(pl.multiple_of(src_off, SUBLANES), SUBLANES), :],
                                 dst_ref.at[pl.ds(pl.multiple_of(dst_off, SUBLANES), SUBLANES), :], sem)


def _to_matrix(ref, n):
    return jnp.concatenate([ref[pl.ds(j, n, stride=SUBLANES), :] for j in range(SUBLANES)], axis=1)


def _dispatch_kernel(pad_lo_ref, pad_n_ref, na_ref, pos_ref, h_ref, xs_ref, zero_s, sem, zsem):
    tm = h_ref.shape[0] // SUBLANES
    blk = EXPERT_TILE * SUBLANES
    n_tiles = xs_ref.shape[0] // blk

    @pl.when(pl.program_id(0) == 0)
    def _():
        zero_s[...] = jnp.zeros_like(zero_s)
        tile_fill = lambda i: pltpu.make_async_copy(zero_s, xs_ref.at[pl.ds(pl.multiple_of(i * blk, blk), blk), :], zsem)
        for e in range(N_EXPERTS):

            def pad_start(r, carry, e=e):
                _tile_copy(zero_s, 0, xs_ref, (pad_lo_ref[e] + r) * SUBLANES, zsem).start()
                return carry

            lax.fori_loop(0, pad_n_ref[e], pad_start, 0)

        def idle_start(i, carry):
            tile_fill(i).start()
            return carry

        lax.fori_loop(na_ref[0], n_tiles, idle_start, 0)
        for e in range(N_EXPERTS):

            def pad_wait(r, carry):
                _tile_copy(zero_s, 0, xs_ref, 0, zsem).wait()
                return carry

            lax.fori_loop(0, pad_n_ref[e], pad_wait, 0)

        def idle_wait(i, carry):
            tile_fill(0).wait()
            return carry

        lax.fori_loop(na_ref[0], n_tiles, idle_wait, 0)

    def issue(t, carry):
        for k in range(2):
            _tile_copy(h_ref, t * SUBLANES, xs_ref, pos_ref[0, k, t], sem).start(priority=k)
        return carry

    lax.fori_loop(0, tm, issue, 0, unroll=8)

    def drain(t, carry):
        for k in range(2):
            _tile_copy(h_ref, 0, xs_ref, 0, sem).wait()
        return carry

    lax.fori_loop(0, tm, drain, 0, unroll=8)


def _dispatch(h2, pos, pad_lo, pad_n, n_active, n_rows, tm):
    n_tiles = h2.shape[0] // (tm * SUBLANES)
    grid_spec = pltpu.PrefetchScalarGridSpec(
        num_scalar_prefetch=3,
        grid=(n_tiles,),
        in_specs=[pl.BlockSpec((1, 2, tm), lambda i, *_: (i, 0, 0), memory_space=pltpu.SMEM),
                  pl.BlockSpec((tm * SUBLANES, LANES), lambda i, *_: (i, 0))],
        out_specs=pl.BlockSpec(memory_space=pl.ANY),
        scratch_shapes=[pltpu.VMEM((EXPERT_TILE * SUBLANES, LANES), F32),
                        pltpu.SemaphoreType.DMA(()), pltpu.SemaphoreType.DMA(())],
    )
    return pl.pallas_call(
        _dispatch_kernel,
        grid_spec=grid_spec,
        out_shape=jax.ShapeDtypeStruct((n_rows * SUBLANES, LANES), F32),
        compiler_params=_params(("arbitrary",)),
        name="moe_dispatch",
    )(pad_lo, pad_n, n_active, pos, h2)


def _expert_kernel(te_ref, na_ref, x_ref, wg_ref, wu_ref, wd_ref, o_ref):
    del te_ref
    live = pl.program_id(0) < na_ref[0]

    @pl.when(live)
    def _():
        x = _to_matrix(x_ref, EXPERT_TILE).astype(BF16)
        a = _dot(x, wg_ref[0])
        u = _dot(x, wu_ref[0])
        o = _dot(a * _sigmoid(a) * u, wd_ref[0])
        for j in range(SUBLANES):
            o_ref[pl.ds(j, EXPERT_TILE, stride=SUBLANES), :] = o[:, j * LANES:(j + 1) * LANES]

    @pl.when(jnp.logical_not(live))
    def _():
        o_ref[...] = jnp.zeros_like(o_ref)


def _experts(xs, tile_expert, n_active, wg, wu, wd):
    blk = EXPERT_TILE * SUBLANES
    n_tiles = xs.shape[0] // blk
    live = lambda i, na: jnp.minimum(i, jnp.maximum(na[0] - 1, 0))
    wmap = lambda i, te, na: (te[live(i, na)], 0, 0)
    grid_spec = pltpu.PrefetchScalarGridSpec(
        num_scalar_prefetch=2,
        grid=(n_tiles,),
        in_specs=[pl.BlockSpec((blk, LANES), lambda i, te, na: (live(i, na), 0)),
                  pl.BlockSpec((1, D_MODEL, D_EXPERT), wmap),
                  pl.BlockSpec((1, D_MODEL, D_EXPERT), wmap),
                  pl.BlockSpec((1, D_EXPERT, D_MODEL), wmap)],
        out_specs=pl.BlockSpec((blk, LANES), lambda i, te, na: (i, 0)),
    )
    return pl.pallas_call(
        _expert_kernel,
        grid_spec=grid_spec,
        out_shape=jax.ShapeDtypeStruct(xs.shape, F32),
        compiler_params=_params(("arbitrary",)),
        name="moe_experts",
    )(tile_expert, n_active, xs, wg, wu, wd)


def _combine_kernel(pos_ref, rt_ref, x1_ref, mod_ref, fn_ref, ys_ref, y_ref, g_s, sem):
    tm = rt_ref.shape[0]

    def issue(t, carry):
        for k in range(2):
            _tile_copy(ys_ref, pos_ref[0, k, t], g_s.at[k], t * SUBLANES, sem).start(priority=k)
        return carry

    lax.fori_loop(0, tm, issue, 0, unroll=8)

    def drain(t, carry):
        for k in range(2):
            _tile_copy(ys_ref, 0, g_s.at[k], 0, sem).wait()
        return carry

    lax.fori_loop(0, tm, drain, 0, unroll=8)

    rt = rt_ref[...]
    moe = rt[:, 4:5] * _to_matrix(g_s.at[0], tm) + rt[:, 5:6] * _to_matrix(g_s.at[1], tm)
    mod = mod_ref[...]
    x1 = x1_ref[...]
    x2 = x1 + mod[:, 5:6, :] * moe.reshape(x1.shape)
    y_ref[...] = x2 * lax.rsqrt(jnp.mean(x2 * x2, axis=-1, keepdims=True) + EPS) * fn_ref[...]


def _combine(ys, pos, route, x1, mod, final_norm, bb, tt):
    B, T, _ = x1.shape
    tm = bb * tt
    tpb = T // tt
    n_tiles = (B // bb) * tpb
    xmap = lambda i: (i // tpb, i % tpb, 0)
    return pl.pallas_call(
        _combine_kernel,
        grid=(n_tiles,),
        in_specs=[pl.BlockSpec((1, 2, tm), lambda i: (i, 0, 0), memory_space=pltpu.SMEM),
                  pl.BlockSpec((tm, LANES), lambda i: (i, 0)),
                  pl.BlockSpec((bb, tt, D_MODEL), xmap),
                  pl.BlockSpec((bb, N_MOD, D_MODEL), lambda i: (i // tpb, 0, 0)),
                  pl.BlockSpec((1, D_MODEL), lambda i: (0, 0)),
                  pl.BlockSpec(memory_space=pl.ANY)],
        out_specs=pl.BlockSpec((bb, tt, D_MODEL), xmap),
        out_shape=jax.ShapeDtypeStruct((B, T, D_MODEL), F32),
        scratch_shapes=[pltpu.VMEM((2, tm * SUBLANES, LANES), F32), pltpu.SemaphoreType.DMA(())],
        compiler_params=_params(("arbitrary",)),
        name="moe_combine",
    )(pos, route, x1, mod, final_norm.reshape(1, D_MODEL), ys)


def _moe(h2, route, counts, wg, wu, wd, x1, mod, final_norm, bb, tt):
    M = route.shape[0]
    tm = bb * tt
    n_tiles_tok = M // tm
    max_tiles = (2 * M + N_EXPERTS * (EXPERT_TILE - 1)) // EXPERT_TILE
    cnt = counts[0, :N_EXPERTS].astype(jnp.int32)
    nt = (cnt + EXPERT_TILE - 1) // EXPERT_TILE
    ends = jnp.cumsum(nt)
    base = ((ends - nt) * EXPERT_TILE).astype(jnp.int32)
    n_active = ends[-1:].astype(jnp.int32)
    tile_expert = jnp.minimum(jnp.sum(jnp.arange(max_tiles)[:, None] >= ends[None, :], axis=1),
                              N_EXPERTS - 1).astype(jnp.int32)
    eid = route[:, 0:2].astype(jnp.int32)
    rank = route[:, 2:4].astype(jnp.int32)
    first_row = jnp.sum(jnp.where(eid[..., None] == jnp.arange(N_EXPERTS), base, 0), axis=-1)
    pos = (first_row + rank) * SUBLANES
    pos = jnp.transpose(pos.reshape(n_tiles_tok, tm, 2), (0, 2, 1))
    xs = _dispatch(h2, pos, base + cnt, nt * EXPERT_TILE - cnt, n_active, max_tiles * EXPERT_TILE, tm)
    ys = _experts(xs, tile_expert, n_active, wg, wu, wd)
    return _combine(ys, pos, route, x1, mod, final_norm, bb, tt)


def _layer(x, mod, conv_init, s0, prm, final_norm, *, bb, tt, dn_nprob, dn_nseq, dn_chunk, vn_dtype):
    B, T, _ = x.shape
    u, vn, zqkv, sz, ga, gb, bg = _inproj(x, mod, prm["norm1"], prm["w_main"], prm["w_bg"], prm["ln_v_w"],
                                          prm["ln_v_b"], prm["adt"], bb, tt, vn_dtype)
    nc = T // dn_chunk
    bo, s_new = _deltanet(zqkv, sz, bg, conv_init, s0, prm["conv_w"], prm["dn_norm"], dn_nprob, dn_nseq, dn_chunk, nc)
    L = min(GMLP_CHUNK, T)
    reps = GMLP_CHUNK // L
    wm = prm["w_s_tril"][:, :L, :L]
    if reps > 1:
        blk = (jnp.arange(GMLP_CHUNK)[:, None] // L) == (jnp.arange(GMLP_CHUNK)[None, :] // L)
        wm = jnp.where(blk[None], jnp.tile(wm, (1, reps, reps)), 0.0)
    ws_eff = wm.astype(BF16)
    bs_rows = jnp.tile(jnp.transpose(prm["b_s"][:, :L]), (reps, 1))
    bs_full = jnp.repeat(bs_rows, A_GDIM, axis=1)
    x1, h2, route, counts = _outproj(u, vn, ga, gb, bo, x, mod, ws_eff, bs_full, prm["w_pa"], prm["w_pb"],
                                     prm["w_o"], prm["norm2"], prm["w_r"], prm["b_r"], bb, tt)
    y = _moe(h2, route, counts, prm["w_e_gate"], prm["w_e_up"], prm["w_e_down"], x1, mod, final_norm, bb, tt)
    tail = min(T, CONV_W - 1)
    zqkv_tail = zqkv.reshape(B, T, DN_CONV_CH)[:, T - tail:, :]
    conv_new = jnp.concatenate([conv_init[:, SUBLANES - (CONV_W - 1):, :], zqkv_tail], axis=1)[:, -(CONV_W - 1):, :]
    return y, conv_new, s_new, vn.reshape(B, T, A_WIDTH)


def kernel(x_prompt, x_sample, c_prompt, c_sample, state_conv, state_delta, w_ada, b_ada, norm1, norm2, w_in, conv_w, a_log, dt_bias, dn_norm, ln_v_w, ln_v_b, w_s, b_s, w_pa, w_pb, w_o, w_rg, b_rg, w_re, b_re, w_e_gate, w_e_up, w_e_down, final_norm):
    depth = w_ada.shape[0]
    assert depth == 1
    bp, tp, _ = x_prompt.shape
    bs_, ts, _ = x_sample.shape
    l = 0
    wi = w_in[l]
    o_qkv = 2 * A_WIDTH
    o_z = o_qkv + DN_CONV_CH
    o_b = o_z + DN_V
    o_ga = o_b + 2 * DN_HEADS
    w_main = jnp.concatenate([wi[:, :o_b], wi[:, o_ga:]], axis=1).astype(BF16)
    w_bg = jnp.pad(wi[:, o_b:o_ga], ((0, 0), (0, LANES - 2 * DN_HEADS))).astype(BF16)
    adt = jnp.stack([jnp.pad(a_log[l], (DN_HEADS, LANES - 2 * DN_HEADS)),
                     jnp.pad(dt_bias[l], (DN_HEADS, LANES - 2 * DN_HEADS))])
    w_r = jnp.pad(jnp.concatenate([w_rg[l], w_re[l]], axis=1), ((0, 0), (0, LANES - N_GROUPS - N_EXPERTS)))
    w_r_hi = w_r.astype(BF16)
    w_r = jnp.concatenate([w_r_hi, (w_r - w_r_hi.astype(F32)).astype(BF16)], axis=1)
    b_r = jnp.pad(jnp.concatenate([b_rg[l], b_re[l]]), (0, LANES - N_GROUPS - N_EXPERTS)).reshape(1, LANES)
    tri = jnp.tril(jnp.ones((GMLP_CHUNK, GMLP_CHUNK), F32))
    prm = dict(
        norm1=norm1[l], norm2=norm2[l], w_main=w_main, w_bg=w_bg, adt=adt, ln_v_w=ln_v_w[l], ln_v_b=ln_v_b[l],
        conv_w=conv_w[l], dn_norm=dn_norm[l], w_s_tril=w_s[l] * tri, b_s=b_s[l],
        w_pa=w_pa[l].astype(BF16), w_pb=w_pb[l].astype(BF16), w_o=w_o[l].astype(BF16), w_r=w_r, b_r=b_r,
        w_e_gate=w_e_gate[l].reshape(N_EXPERTS, D_MODEL, D_EXPERT),
        w_e_up=w_e_up[l].reshape(N_EXPERTS, D_MODEL, D_EXPERT),
        w_e_down=w_e_down[l].reshape(N_EXPERTS, D_EXPERT, D_MODEL),
    )

    c_all = jnp.concatenate([c_prompt, c_sample], axis=0)
    mod = _ada(c_all, w_ada[l], b_ada[l]).reshape(bp + bs_, N_MOD, D_MODEL)
    mod_p, mod_s = mod[:bp], mod[bp:]

    pad_rows = SUBLANES - (CONV_W - 1)
    cinit_p = jnp.zeros((bp, SUBLANES, DN_CONV_CH), F32)
    cinit_s = jnp.pad(state_conv[l], ((0, 0), (pad_rows, 0), (0, 0)))
    s0_p = jnp.zeros((bp, DN_HEADS, DN_KDIM, DN_VDIM), F32)

    yp, conv_p, delta_p, _ = _layer(x_prompt, mod_p, cinit_p, s0_p, prm, final_norm,
                                    bb=1, tt=min(ROW_TILE, tp), dn_nprob=4 if bp % 4 == 0 else 1, dn_nseq=1,
                                    dn_chunk=min(DN_CHUNK, tp), vn_dtype=BF16)
    ys, conv_s, delta_s, vn_s = _layer(x_sample, mod_s, cinit_s, state_delta[l], prm, final_norm,
                                       bb=min(bs_, ROW_TILE // ts), tt=ts, dn_nprob=1,
                                       dn_nseq=min(bs_, DN_CHUNK // ts), dn_chunk=ts, vn_dtype=F32)
    return (yp, ys, conv_p[None], conv_s[None], delta_p[None], delta_s[None], vn_s[None])
```

```python
import functools

import jax
import jax.numpy as jnp
from jax import lax
from jax.experimental import pallas as pl
from jax.experimental.pallas import tpu as pltpu

F32 = jnp.float32
BF16 = jnp.bfloat16

D_MODEL = 1024
A_WIDTH = 1024
A_GROUPS = 4
A_GDIM = A_WIDTH // A_GROUPS
GMLP_CHUNK = 128
DN_HEADS = 8
DN_KDIM = 128
DN_VDIM = 128
DN_QK = DN_HEADS * DN_KDIM
DN_V = DN_HEADS * DN_VDIM
DN_CONV_CH = 2 * DN_QK + DN_V
CONV_W = 4
DN_CHUNK = 64
N_GROUPS = 4
EXP_PER_GROUP = 8
N_EXPERTS = N_GROUPS * EXP_PER_GROUP
D_EXPERT = 256
N_MOD = 6
EPS = 1e-6

LANES = 128
SUBLANES = 8
N_MAIN_BLOCKS = 8
ROUTER_LANE0 = N_GROUPS
VMEM_LIMIT = 56 * 1024 * 1024
ROW_TILE = 512
EXPERT_TILE = 256
EXPERTS_PER_STEP = 4


def _sigmoid(x):
    return 1.0 / (1.0 + jnp.exp(-x))


def _gelu(x):
    return 0.5 * x * (1.0 + lax.erf(x * 0.7071067811865476))


def _softplus(x):
    return jnp.maximum(x, 0.0) + jnp.log(1.0 + jnp.exp(-jnp.abs(x)))


def _dot(a, b):
    return jnp.dot(a.astype(BF16), b.astype(BF16), preferred_element_type=F32)


def _dot_nt(a, b):
    return lax.dot_general(a.astype(BF16), b.astype(BF16), (((1,), (1,)), ((), ())), preferred_element_type=F32)


def _dot_tn(a, b):
    return lax.dot_general(a.astype(BF16), b.astype(BF16), (((0,), (0,)), ((), ())), preferred_element_type=F32)


def _split3(x):
    hi = x.astype(BF16)
    r1 = x - hi.astype(F32)
    mid = r1.astype(BF16)
    lo = (r1 - mid.astype(F32)).astype(BF16)
    return hi, mid, lo


def _dot_exact_lhs(a_bf16, x):
    hi, mid, lo = _split3(x)
    f = lambda p: jnp.dot(a_bf16, p, preferred_element_type=F32)
    return f(hi) + f(mid) + f(lo)


_PASSES = {"invp": 3, "invn": 3, "inv_hi": 1, "inv_kmax": 8, "sol": 1, "kk": 1, "ws": 1, "attn": 1, "state": 1}


def _hilo(x):
    hi = x.astype(BF16)
    lo = (x - hi.astype(F32)).astype(BF16)
    return hi, lo


def _mm(site, a, b, dims=(((1,), (0,)), ((), ()))):
    f = lambda p, q: lax.dot_general(p, q, dims, preferred_element_type=F32)
    if _PASSES[site] == 1:
        return f(a.astype(BF16), b.astype(BF16))
    ah, al = _hilo(a)
    bh, bl = _hilo(b)
    return f(ah, bh) + (f(ah, bl) + f(al, bh))


_NT = (((1,), (1,)), ((), ()))
_TN = (((0,), (0,)), ((), ()))

def _params(sem):
    return pltpu.CompilerParams(dimension_semantics=sem, vmem_limit_bytes=VMEM_LIMIT)


def _ada_kernel(c_ref, w_ref, b_ref, o_ref):
    c = c_ref[...]
    o_ref[...] = _dot(c * _sigmoid(c), w_ref[...]) + b_ref[...]


def _ada(c_all, w_ada, b_ada):
    n = c_all.shape[0]
    width = w_ada.shape[1]
    bn = 512
    return pl.pallas_call(
        _ada_kernel,
        grid=(width // bn,),
        in_specs=[pl.BlockSpec((n, D_MODEL), lambda j: (0, 0)),
                  pl.BlockSpec((D_MODEL, bn), lambda j: (0, j)),
                  pl.BlockSpec((1, bn), lambda j: (0, j))],
        out_specs=pl.BlockSpec((n, bn), lambda j: (0, j)),
        out_shape=jax.ShapeDtypeStruct((n, width), F32),
        compiler_params=_params(("arbitrary",)),
        name="ada_mod",
    )(c_all, w_ada, b_ada.reshape(1, width))


def _inproj_kernel(x_ref, mod_ref, n1_ref, w_ref, wbg_ref, lnw_ref, lnb_ref, adt_ref,
                   u_ref, vn_ref, qkv_ref, sz_ref, ga_ref, gb_ref, bg_ref):
    tm = u_ref.shape[0]
    x = x_ref[...]
    y = x * lax.rsqrt(jnp.mean(x * x, axis=-1, keepdims=True) + EPS) * n1_ref[...]
    mod = mod_ref[...]
    h = y * (1.0 + mod[:, 1:2, :]) + mod[:, 0:1, :]
    hb = h.reshape(tm, D_MODEL).astype(BF16)
    zbg = jnp.dot(hb, wbg_ref[...], preferred_element_type=F32)
    lane = lax.broadcasted_iota(jnp.int32, zbg.shape, 1)
    adt = adt_ref[...]
    g = -jnp.exp(adt[0:1, :]) * _softplus(zbg + adt[1:2, :])
    bg_ref[...] = jnp.where(lane < DN_HEADS, _sigmoid(zbg), g)

    blk = lambda j: jnp.dot(hb, w_ref[:, j * 1024:(j + 1) * 1024], preferred_element_type=F32)
    u_ref[...] = _gelu(blk(0)).astype(u_ref.dtype)
    a = _gelu(blk(1))
    ac = a - jnp.mean(a, axis=-1, keepdims=True)
    var = jnp.mean(ac * ac, axis=-1, keepdims=True)
    vn_ref[...] = (ac * lax.rsqrt(var + EPS) * lnw_ref[...] + lnb_ref[...]).astype(vn_ref.dtype)
    for j in range(3):
        qkv_ref[:, j * 1024:(j + 1) * 1024] = blk(2 + j)
    z = blk(5)
    sz_ref[...] = (z * _sigmoid(z)).astype(sz_ref.dtype)
    ga_ref[...] = _sigmoid(blk(6)).astype(ga_ref.dtype)
    gb_ref[...] = _sigmoid(blk(7)).astype(gb_ref.dtype)


def _inproj(x, mod, norm1, w_main, w_bg, ln_w, ln_b, adt, bb, tt, vn_dtype):
    B, T, _ = x.shape
    tm = bb * tt
    tpb = T // tt
    n_tiles = (B // bb) * tpb
    M = B * T
    const = lambda i: (0, 0)
    rows = lambda i: (i, 0)
    resident = dict(pipeline_mode=pl.Buffered(1))
    in_specs = [pl.BlockSpec((bb, tt, D_MODEL), lambda i: (i // tpb, i % tpb, 0)),
                pl.BlockSpec((bb, N_MOD, D_MODEL), lambda i: (i // tpb, 0, 0)),
                pl.BlockSpec((1, D_MODEL), const),
                pl.BlockSpec((D_MODEL, N_MAIN_BLOCKS * 1024), const, **resident),
                pl.BlockSpec((D_MODEL, LANES), const, **resident),
                pl.BlockSpec((1, A_WIDTH), const),
                pl.BlockSpec((1, A_WIDTH), const),
                pl.BlockSpec((2, LANES), const)]
    args = [x, mod, norm1.reshape(1, D_MODEL), w_main, w_bg, ln_w.reshape(1, A_WIDTH), ln_b.reshape(1, A_WIDTH), adt]
    out_specs = [pl.BlockSpec((tm, A_WIDTH), rows),
                 pl.BlockSpec((tm, A_WIDTH), rows),
                 pl.BlockSpec((tm, DN_CONV_CH), rows),
                 pl.BlockSpec((tm, DN_V), rows),
                 pl.BlockSpec((tm, D_MODEL), rows),
                 pl.BlockSpec((tm, D_MODEL), rows),
                 pl.BlockSpec((tm, LANES), rows)]
    out_shape = [jax.ShapeDtypeStruct((M, A_WIDTH), BF16),
                 jax.ShapeDtypeStruct((M, A_WIDTH), vn_dtype),
                 jax.ShapeDtypeStruct((M, DN_CONV_CH), F32),
                 jax.ShapeDtypeStruct((M, DN_V), BF16),
                 jax.ShapeDtypeStruct((M, D_MODEL), BF16),
                 jax.ShapeDtypeStruct((M, D_MODEL), BF16),
                 jax.ShapeDtypeStruct((M, LANES), F32)]
    return pl.pallas_call(
        _inproj_kernel,
        grid=(n_tiles,),
        in_specs=in_specs,
        out_specs=out_specs,
        out_shape=out_shape,
        compiler_params=_params(("arbitrary",)),
        name="in_proj",
    )(*args)


def _inv_unit_lower_minus_eye(lmats, nilpotent):
    ns = [-l for l in lmats]
    ps = list(lmats)
    k = 2
    while k < nilpotent:
        early = k <= _PASSES["inv_kmax"]
        ps = [_mm("invp" if early else "inv_hi", p, p) for p in ps]
        ns = [n + p + _mm("invn" if early else "inv_hi", n, p) for n, p in zip(ns, ps)]
        k *= 2
    return ns


def _head_sumsq(x):
    pair = 2 * DN_KDIM
    r = lax.broadcasted_iota(jnp.int32, (pair, pair), 0) >= DN_KDIM
    c = lax.broadcasted_iota(jnp.int32, (pair, pair), 1) >= DN_KDIM
    ones2 = jnp.where(r == c, 1.0, 0.0).astype(BF16)
    sq = (x * x).astype(BF16)
    return jnp.concatenate([jnp.dot(sq[:, p * pair:(p + 1) * pair], ones2, preferred_element_type=F32)
                            for p in range(x.shape[1] // pair)], axis=1)


def _deltanet_kernel(q_ref, k_ref, v_ref, z_ref, bg_ref, cinit_ref, s0_ref, cw_ref, dnn_ref,
                     bo_ref, sn_ref, xp_s, s_s, *, nprob, nseq, C):
    R = nseq * C
    c = pl.program_id(1)
    nc = pl.num_programs(1)

    @pl.when(c == 0)
    def _():
        xp_s[:, 0:SUBLANES, :] = cinit_ref[...]
        s_s[...] = s0_ref[...]

    @pl.when(c > 0)
    def _():
        xp_s[:, 0:SUBLANES, :] = xp_s[:, C:C + SUBLANES, :]

    for p in range(nprob):
        for s in range(nseq):
            i = p * nseq + s
            xp_s[i, SUBLANES:SUBLANES + C, 0:DN_QK] = q_ref[p, s * C:(s + 1) * C, :]
            xp_s[i, SUBLANES:SUBLANES + C, DN_QK:2 * DN_QK] = k_ref[p, s * C:(s + 1) * C, :]
            xp_s[i, SUBLANES:SUBLANES + C, 2 * DN_QK:DN_CONV_CH] = v_ref[p, s * C:(s + 1) * C, :]

    cw = cw_ref[...]
    base = SUBLANES - (CONV_W - 1)
    acc = None
    for j in range(CONV_W):
        term = xp_s[:, base + j:base + j + C, :] * cw[j:j + 1, :]
        acc = term if acc is None else acc + term
    qkv = acc.reshape(nprob * R, DN_CONV_CH)
    qkv = qkv * _sigmoid(qkv)
    q_all = qkv[:, 0:DN_QK]
    k_all = qkv[:, DN_QK:2 * DN_QK]
    v_all = qkv[:, 2 * DN_QK:DN_CONV_CH]
    qn_all = q_all * lax.rsqrt(_head_sumsq(q_all) + EPS) * (DN_KDIM ** -0.5)
    kn_all = k_all * lax.rsqrt(_head_sumsq(k_all) + EPS)

    row = lax.broadcasted_iota(jnp.int32, (R, R), 0)
    col = lax.broadcasted_iota(jnp.int32, (R, R), 1)
    if nseq > 1:
        shift = C.bit_length() - 1
        same = lax.shift_right_logical(row, shift) == lax.shift_right_logical(col, shift)
        tril = same & (col <= row)
        strict = same & (col < row)
    else:
        tril = col <= row
        strict = col < row
    tril_b = jnp.where(tril, 1.0, 0.0).astype(BF16)
    bg = [bg_ref[p] for p in range(nprob)]
    gc = [_dot_exact_lhs(tril_b, bg[p]) for p in range(nprob)]
    gct = [g.T for g in gc]
    dnn = dnn_ref[...]

    IT = [(p, h) for p in range(nprob) for h in range(DN_HEADS)]
    NI = range(len(IT))
    hsl = [slice(h * DN_KDIM, (h + 1) * DN_KDIM) for h in range(DN_HEADS)]
    rows = [slice(p * R, (p + 1) * R) for p in range(nprob)]
    qn = [qn_all[rows[p], hsl[h]] for p, h in IT]
    kn = [kn_all[rows[p], hsl[h]] for p, h in IT]
    vh = [v_all[rows[p], hsl[h]] for p, h in IT]
    beta = [bg[p][:, h:h + 1] for p, h in IT]
    gcol = [gc[p][:, DN_HEADS + h:DN_HEADS + h + 1] for p, h in IT]
    grow = [gct[p][DN_HEADS + h:DN_HEADS + h + 1, :] for p, h in IT]
    decay = [jnp.exp(jnp.minimum(gcol[i] - grow[i], 0.0)) for i in NI]
    eg = [jnp.exp(gcol[i]) for i in NI]
    kb = [kn[i] * beta[i] for i in NI]
    vb = [vh[i] * beta[i] for i in NI]
    a2 = [_mm("kk", jnp.concatenate([kb[i], qn[i]], axis=0), kn[i], _NT) for i in NI]
    lmat = [jnp.where(strict, a2[i][:R] * decay[i], 0.0) for i in NI]
    attn = [jnp.where(tril, a2[i][R:] * decay[i], 0.0) for i in NI]
    tn = _inv_unit_lower_minus_eye(lmat, C)
    rhs = [jnp.concatenate([vb[i], kb[i] * eg[i]], axis=1) for i in NI]
    sol = [rhs[i] + _mm("sol", tn[i], rhs[i]) for i in NI]
    u = [x[:, :DN_VDIM] for x in sol]
    w = [x[:, DN_VDIM:] for x in sol]
    qg = [qn[i] * eg[i] for i in NI]
    vnew = [[None] * nseq for _ in NI]
    qs = [[None] * nseq for _ in NI]
    for s in range(nseq):
        rs = slice(s * C, (s + 1) * C)
        st = [s_s[p * nseq + s, h] for p, h in IT]
        ws = [_mm("ws", jnp.concatenate([w[i][rs], qg[i][rs]], axis=0), st[i]) for i in NI]
        for i in NI:
            vnew[i][s] = u[i][rs] - ws[i][:C]
            qs[i][s] = ws[i][C:]
        glast = [gcol[i][(s + 1) * C - 1:(s + 1) * C, :] for i in NI]
        kd = [kn[i][rs] * jnp.exp(glast[i] - gcol[i][rs]) for i in NI]
        upd = [_mm("state", kd[i], vnew[i][s], _TN) for i in NI]
        for i, (p, h) in enumerate(IT):
            s_s[p * nseq + s, h] = st[i] * jnp.exp(glast[i]) + upd[i]
    cat = lambda parts: parts[0] if nseq == 1 else jnp.concatenate(parts, axis=0)
    o = [cat(qs[i]) + _mm("attn", attn[i], cat(vnew[i])) for i in NI]
    o_all = jnp.concatenate([jnp.concatenate(o[p * DN_HEADS:(p + 1) * DN_HEADS], axis=1) for p in range(nprob)],
                            axis=0)
    dnn_all = jnp.concatenate([dnn] * DN_HEADS, axis=1)
    on = o_all * lax.rsqrt(_head_sumsq(o_all) * (1.0 / DN_VDIM) + EPS) * dnn_all
    for p in range(nprob):
        bo_ref[p] = (on[rows[p]] * z_ref[p].astype(F32)).astype(bo_ref.dtype)

    @pl.when(c == nc - 1)
    def _():
        sn_ref[...] = s_s[...]


def _deltanet(zqkv, sz, bg, conv_init, s0, conv_w, dn_norm, nprob, nseq, C, nc):
    M = zqkv.shape[0]
    R = nseq * C
    G = M // (R * nc)
    view = lambda a: a.reshape(G, R * nc, a.shape[-1])
    rmap = lambda col: (lambda g, c: (g, c, col))
    nsq = nprob * nseq
    state_spec = pl.BlockSpec((nsq, DN_HEADS, DN_KDIM, DN_VDIM), lambda g, c: (g, 0, 0, 0))
    bo, s_new = pl.pallas_call(
        functools.partial(_deltanet_kernel, nprob=nprob, nseq=nseq, C=C),
        grid=(G // nprob, nc),
        in_specs=[pl.BlockSpec((nprob, R, DN_QK), rmap(0)),
                  pl.BlockSpec((nprob, R, DN_QK), rmap(1)),
                  pl.BlockSpec((nprob, R, DN_V), rmap(2)),
                  pl.BlockSpec((nprob, R, DN_V), rmap(0)),
                  pl.BlockSpec((nprob, R, LANES), rmap(0)),
                  pl.BlockSpec((nsq, SUBLANES, DN_CONV_CH), lambda g, c: (g, 0, 0)),
                  state_spec,
                  pl.BlockSpec((CONV_W, DN_CONV_CH), lambda g, c: (0, 0)),
                  pl.BlockSpec((1, DN_VDIM), lambda g, c: (0, 0))],
        out_specs=[pl.BlockSpec((nprob, R, DN_V), rmap(0)), state_spec],
        out_shape=[jax.ShapeDtypeStruct((G, R * nc, DN_V), BF16),
                   jax.ShapeDtypeStruct(s0.shape, F32)],
        scratch_shapes=[pltpu.VMEM((nsq, C + SUBLANES, DN_CONV_CH), F32),
                        pltpu.VMEM((nsq, DN_HEADS, DN_KDIM, DN_VDIM), F32)],
        compiler_params=_params(("arbitrary", "arbitrary")),
        name="deltanet",
    )(view(zqkv), view(zqkv), view(zqkv), view(sz), view(bg), conv_init, s0, conv_w, dn_norm.reshape(1, DN_VDIM))
    return bo.reshape(M, DN_V), s_new


def _outproj_kernel(u_ref, vn_ref, ga_ref, gb_ref, bo_ref, x_ref, mod_ref, ws_ref, bs_ref, wpa_ref, wpb_ref,
                    wo_ref, n2_ref, wr_ref, br_ref, before_ref, x1_ref, h2_ref, rt_ref, cnt_ref, cnt_s):
    tm = u_ref.shape[0]
    vn = vn_ref[...].astype(BF16)
    parts = []
    for r in range(tm // GMLP_CHUNK):
        rs = slice(r * GMLP_CHUNK, (r + 1) * GMLP_CHUNK)
        row_parts = []
        for g in range(A_GROUPS):
            gs = slice(g * A_GDIM, (g + 1) * A_GDIM)
            row_parts.append(jnp.dot(ws_ref[g], vn[rs, gs], preferred_element_type=F32))
        parts.append(jnp.concatenate(row_parts, axis=1) + bs_ref[...])
    sv = parts[0] if len(parts) == 1 else jnp.concatenate(parts, axis=0)
    a_out = u_ref[...].astype(F32) * sv
    pa = _dot(a_out, wpa_ref[...])
    pb = _dot(bo_ref[...], wpb_ref[...])
    m = ga_ref[...].astype(F32) * pa + gb_ref[...].astype(F32) * pb
    mix = _dot(m, wo_ref[...])
    mod = mod_ref[...]
    x = x_ref[...]
    x1 = x + mod[:, 2:3, :] * mix.reshape(x.shape)
    x1_ref[...] = x1
    y = x1 * lax.rsqrt(jnp.mean(x1 * x1, axis=-1, keepdims=True) + EPS) * n2_ref[...]
    h2 = (y * (1.0 + mod[:, 4:5, :]) + mod[:, 3:4, :]).reshape(tm, D_MODEL)
    for j in range(SUBLANES):
        h2_ref[pl.ds(j, tm, stride=SUBLANES), :] = h2[:, j * LANES:(j + 1) * LANES]

    hh, hl = _hilo(h2)
    wr = wr_ref[...]
    p = jnp.dot(hh, wr, preferred_element_type=F32)
    logits = p[:, :LANES] + (p[:, LANES:] + jnp.dot(hl, wr[:, :LANES], preferred_element_type=F32)) + br_ref[...]
    lane = lax.broadcasted_iota(jnp.int32, logits.shape, 1)
    lanef = lane.astype(F32)
    neg = jnp.float32(-jnp.inf)
    big = jnp.float32(1e9)
    gl = jnp.where(lane < N_GROUPS, logits, neg)
    gmax = jnp.max(gl, axis=-1, keepdims=True)
    gsel = jnp.min(jnp.where(gl == gmax, lanef, big), axis=-1, keepdims=True)
    gp = 1.0 / jnp.sum(jnp.exp(gl - gmax), axis=-1, keepdims=True)
    lo = ROUTER_LANE0 + EXP_PER_GROUP * gsel
    in_grp = (lanef >= lo) & (lanef < lo + EXP_PER_GROUP)
    el = jnp.where(in_grp, logits, neg)
    m1 = jnp.max(el, axis=-1, keepdims=True)
    i1 = jnp.min(jnp.where(el == m1, lanef, big), axis=-1, keepdims=True)
    el2 = jnp.where(lanef == i1, neg, el)
    m2 = jnp.max(el2, axis=-1, keepdims=True)
    i2 = jnp.min(jnp.where(el2 == m2, lanef, big), axis=-1, keepdims=True)
    ex = jnp.exp(m2 - m1)
    w1 = gp / (1.0 + ex)
    w2 = gp * ex / (1.0 + ex)
    e1 = i1 - ROUTER_LANE0
    e2 = i2 - ROUTER_LANE0

    @pl.when(pl.program_id(0) == 0)
    def _():
        cnt_s[...] = jnp.zeros_like(cnt_s)

    oh1 = jnp.where(lanef == e1, 1.0, 0.0)
    oh2 = jnp.where(lanef == e2, 1.0, 0.0)
    oh = oh1 + oh2
    seen = jnp.dot(before_ref[...], oh.astype(BF16), preferred_element_type=F32) + cnt_s[...]
    r1 = jnp.sum(oh1 * seen, axis=-1, keepdims=True)
    r2 = jnp.sum(oh2 * seen, axis=-1, keepdims=True)
    cnt_s[...] += jnp.sum(oh, axis=0, keepdims=True)
    cnt_ref[...] = cnt_s[...]
    rec = jnp.zeros_like(logits)
    for k, val in enumerate((e1, e2, r1, r2, w1, w2)):
        rec = jnp.where(lane == k, val, rec)
    rt_ref[...] = rec


def _outproj(u, vn, ga, gb, bo, x, mod, ws_eff, bs_full, w_pa, w_pb, w_o, norm2, w_r, b_r, bb, tt):
    B, T, _ = x.shape
    tm = bb * tt
    tpb = T // tt
    n_tiles = (B // bb) * tpb
    M = B * T
    cmap = lambda col: (lambda i: (i, col))
    full2 = lambda i: (0, 0)
    return pl.pallas_call(
        _outproj_kernel,
        grid=(n_tiles,),
        in_specs=[pl.BlockSpec((tm, A_WIDTH), cmap(0)),
                  pl.BlockSpec((tm, A_WIDTH), cmap(0)),
                  pl.BlockSpec((tm, D_MODEL), cmap(0)),
                  pl.BlockSpec((tm, D_MODEL), cmap(0)),
                  pl.BlockSpec((tm, DN_V), cmap(0)),
                  pl.BlockSpec((bb, tt, D_MODEL), lambda i: (i // tpb, i % tpb, 0)),
                  pl.BlockSpec((bb, N_MOD, D_MODEL), lambda i: (i // tpb, 0, 0)),
                  pl.BlockSpec((A_GROUPS, GMLP_CHUNK, GMLP_CHUNK), lambda i: (0, 0, 0)),
                  pl.BlockSpec((GMLP_CHUNK, A_WIDTH), full2),
                  pl.BlockSpec((A_WIDTH, D_MODEL), full2),
                  pl.BlockSpec((DN_V, D_MODEL), full2),
                  pl.BlockSpec((D_MODEL, D_MODEL), full2),
                  pl.BlockSpec((1, D_MODEL), full2),
                  pl.BlockSpec((D_MODEL, 2 * LANES), full2),
                  pl.BlockSpec((1, LANES), full2),
                  pl.BlockSpec((tm, tm), full2)],
        out_specs=[pl.BlockSpec((bb, tt, D_MODEL), lambda i: (i // tpb, i % tpb, 0)),
                   pl.BlockSpec((tm * SUBLANES, LANES), cmap(0)),
                   pl.BlockSpec((tm, LANES), cmap(0)),
                   pl.BlockSpec((1, LANES), full2)],
        out_shape=[jax.ShapeDtypeStruct((B, T, D_MODEL), F32),
                   jax.ShapeDtypeStruct((M * SUBLANES, LANES), F32),
                   jax.ShapeDtypeStruct((M, LANES), F32),
                   jax.ShapeDtypeStruct((1, LANES), F32)],
        scratch_shapes=[pltpu.VMEM((1, LANES), F32)],
        compiler_params=_params(("arbitrary",)),
        name="out_proj",
    )(u, vn, ga, gb, bo, x, mod, ws_eff, bs_full, w_pa, w_pb, w_o, norm2.reshape(1, D_MODEL), w_r, b_r,
      jnp.tril(jnp.ones((tm, tm), BF16), -1))


def _tile_copy(src_ref, src_off, dst_ref, dst_off, sem):
    return pltpu.make_async_copy(src_ref.at[pl.ds(pl.multiple_of(src_off, SUBLANES), SUBLANES), :],
                                 dst_ref.at[pl.ds(pl.multiple_of(dst_off, SUBLANES), SUBLANES), :], sem)


def _to_matrix(ref, n):
    return jnp.concatenate([ref[pl.ds(j, n, stride=SUBLANES), :] for j in range(SUBLANES)], axis=1)


def _zero_fill(pad_lo_ref, pad_n_ref, na_ref, xs_ref, zero_s, zsem):
    blk = EXPERT_TILE * SUBLANES
    n_tiles = xs_ref.shape[0] // blk
    zero_s[...] = jnp.zeros_like(zero_s)
    run = lambda off, rows: pltpu.make_async_copy(
        zero_s.at[pl.ds(0, rows * SUBLANES), :],
        xs_ref.at[pl.ds(pl.multiple_of(off * SUBLANES, SUBLANES), rows * SUBLANES), :], zsem)
    sizes = [1 << b for b in reversed(range(EXPERT_TILE.bit_length() - 1))]
    for wait in (False, True):
        for e in range(N_EXPERTS):
            off = pad_lo_ref[e]
            for rows in sizes:
                bit = pad_n_ref[e] & rows

                @pl.when(bit != 0)
                def _(off=off, rows=rows):
                    run(0, rows).wait() if wait else run(off, rows).start()

                off = off + bit

        def idle(i, carry):
            run(0, EXPERT_TILE).wait() if wait else run(i * EXPERT_TILE, EXPERT_TILE).start()
            return carry

        lax.fori_loop(na_ref[0], n_tiles, idle, 0)


def _dispatch_kernel(pad_lo_ref, pad_n_ref, na_ref, pos_ref, *refs, first_tile):
    n_paths = len(first_tile) - 1
    h_refs = refs[:n_paths]
    xs_ref, zero_s, sem, zsem = refs[n_paths:]
    i = pl.program_id(0)

    @pl.when(i == 0)
    def _():
        _zero_fill(pad_lo_ref, pad_n_ref, na_ref, xs_ref, zero_s, zsem)

    tm = h_refs[0].shape[0] // SUBLANES
    for p, h_ref in enumerate(h_refs):
        @pl.when((i >= first_tile[p]) & (i < first_tile[p + 1]))
        def _(h_ref=h_ref):
            def issue(t, carry):
                for k in range(2):
                    _tile_copy(h_ref, t * SUBLANES, xs_ref, pos_ref[0, k, t], sem).start(priority=k)
                return carry

            lax.fori_loop(0, tm, issue, 0, unroll=8)

            def drain(t, carry):
                for k in range(2):
                    _tile_copy(h_ref, 0, xs_ref, 0, sem).wait()
                return carry

            lax.fori_loop(0, tm, drain, 0, unroll=8)


def _dispatch(h2s, pos, tm, n_rows, pad_lo, pad_n, n_active):
    blk = tm * SUBLANES
    tiles = [h.shape[0] // blk for h in h2s]
    first_tile = [sum(tiles[:p]) for p in range(len(tiles) + 1)]
    hmap = lambda p: (lambda i, *_: (jnp.clip(i - first_tile[p], 0, tiles[p] - 1), 0))
    grid_spec = pltpu.PrefetchScalarGridSpec(
        num_scalar_prefetch=3,
        grid=(first_tile[-1],),
        in_specs=[pl.BlockSpec((1, 2, tm), lambda i, *_: (i, 0, 0), memory_space=pltpu.SMEM)]
                 + [pl.BlockSpec((blk, LANES), hmap(p)) for p in range(len(tiles))],
        out_specs=pl.BlockSpec(memory_space=pl.ANY),
        scratch_shapes=[pltpu.VMEM((EXPERT_TILE * SUBLANES, LANES), F32),
                        pltpu.SemaphoreType.DMA(()), pltpu.SemaphoreType.DMA(())],
    )
    return pl.pallas_call(
        functools.partial(_dispatch_kernel, first_tile=tuple(first_tile)),
        grid_spec=grid_spec,
        out_shape=jax.ShapeDtypeStruct((n_rows * SUBLANES, LANES), F32),
        compiler_params=_params(("arbitrary",)),
        name="moe_dispatch",
    )(pad_lo, pad_n, n_active, pos, *h2s)


def _expert_kernel(te_ref, na_ref, x_ref, *refs):
    del te_ref
    w_refs, o_ref = refs[:-1], refs[-1]
    blk = EXPERT_TILE * SUBLANES
    live = EXPERTS_PER_STEP * pl.program_id(0) < na_ref[0]

    @pl.when(live)
    def _():
        TR = range(EXPERTS_PER_STEP)
        wg, wu, wd = ([w_refs[3 * t + k][0] for t in TR] for k in range(3))
        x = [_to_matrix(x_ref.at[pl.ds(t * blk, blk), :], EXPERT_TILE).astype(BF16) for t in TR]
        a = [_dot(x[t], wg[t]) for t in TR]
        u = [_dot(x[t], wu[t]) for t in TR]
        o = [_dot(a[t] * _sigmoid(a[t]) * u[t], wd[t]) for t in TR]
        for t in TR:
            for j in range(SUBLANES):
                o_ref[pl.ds(t * blk + j, EXPERT_TILE, stride=SUBLANES), :] = o[t][:, j * LANES:(j + 1) * LANES]

    @pl.when(jnp.logical_not(live))
    def _():
        o_ref[...] = jnp.zeros_like(o_ref)


def _experts(xs, tile_expert, n_active, wg, wu, wd):
    blk = EXPERT_TILE * SUBLANES * EXPERTS_PER_STEP
    n_steps = xs.shape[0] // blk
    last = lambda na: jnp.maximum(na[0] - 1, 0)
    xmap = lambda i, te, na: (jnp.minimum(i, last(na) // EXPERTS_PER_STEP), 0)
    wmap = lambda t: (lambda i, te, na: (te[jnp.minimum(EXPERTS_PER_STEP * i + t, last(na))], 0, 0))
    w_specs = []
    for t in range(EXPERTS_PER_STEP):
        w_specs += [pl.BlockSpec((1, D_MODEL, D_EXPERT), wmap(t)),
                    pl.BlockSpec((1, D_MODEL, D_EXPERT), wmap(t)),
                    pl.BlockSpec((1, D_EXPERT, D_MODEL), wmap(t))]
    grid_spec = pltpu.PrefetchScalarGridSpec(
        num_scalar_prefetch=2,
        grid=(n_steps,),
        in_specs=[pl.BlockSpec((blk, LANES), xmap)] + w_specs,
        out_specs=pl.BlockSpec((blk, LANES), lambda i, te, na: (i, 0)),
    )
    return pl.pallas_call(
        _expert_kernel,
        grid_spec=grid_spec,
        out_shape=jax.ShapeDtypeStruct(xs.shape, F32),
        compiler_params=_params(("arbitrary",)),
        name="moe_experts",
    )(tile_expert, n_active, xs, *([wg, wu, wd] * EXPERTS_PER_STEP))


def _combine_kernel(pos_ref, rt_ref, x1_ref, mod_ref, fn_ref, ys_ref, y_ref, g_s, sem):
    tm = rt_ref.shape[0]

    def issue(t, carry):
        for k in range(2):
            _tile_copy(ys_ref, pos_ref[0, k, t], g_s.at[k], t * SUBLANES, sem).start(priority=k)
        return carry

    lax.fori_loop(0, tm, issue, 0, unroll=8)

    def drain(t, carry):
        for k in range(2):
            _tile_copy(ys_ref, 0, g_s.at[k], 0, sem).wait()
        return carry

    lax.fori_loop(0, tm, drain, 0, unroll=8)

    rt = rt_ref[...]
    moe = rt[:, 4:5] * _to_matrix(g_s.at[0], tm) + rt[:, 5:6] * _to_matrix(g_s.at[1], tm)
    mod = mod_ref[...]
    x1 = x1_ref[...]
    x2 = x1 + mod[:, 5:6, :] * moe.reshape(x1.shape)
    y_ref[...] = x2 * lax.rsqrt(jnp.mean(x2 * x2, axis=-1, keepdims=True) + EPS) * fn_ref[...]


def _combine(ys, pos, route, x1, mod, final_norm, bb, tt):
    B, T, _ = x1.shape
    tm = bb * tt
    tpb = T // tt
    n_tiles = (B // bb) * tpb
    xmap = lambda i: (i // tpb, i % tpb, 0)
    return pl.pallas_call(
        _combine_kernel,
        grid=(n_tiles,),
        in_specs=[pl.BlockSpec((1, 2, tm), lambda i: (i, 0, 0), memory_space=pltpu.SMEM),
                  pl.BlockSpec((tm, LANES), lambda i: (i, 0)),
                  pl.BlockSpec((bb, tt, D_MODEL), xmap),
                  pl.BlockSpec((bb, N_MOD, D_MODEL), lambda i: (i // tpb, 0, 0)),
                  pl.BlockSpec((1, D_MODEL), lambda i: (0, 0)),
                  pl.BlockSpec(memory_space=pl.ANY)],
        out_specs=pl.BlockSpec((bb, tt, D_MODEL), xmap),
        out_shape=jax.ShapeDtypeStruct((B, T, D_MODEL), F32),
        scratch_shapes=[pltpu.VMEM((2, tm * SUBLANES, LANES), F32), pltpu.SemaphoreType.DMA(())],
        compiler_params=_params(("arbitrary",)),
        name="moe_combine",
    )(pos, route, x1, mod, final_norm.reshape(1, D_MODEL), ys)


def _moe(paths, wg, wu, wd, final_norm):
    n_assign = sum(2 * p["route"].shape[0] for p in paths)
    max_tiles = (n_assign + N_EXPERTS * (EXPERT_TILE - 1)) // EXPERT_TILE
    max_tiles += -max_tiles % EXPERTS_PER_STEP
    cnts = [p["counts"][0, :N_EXPERTS].astype(jnp.int32) for p in paths]
    cnt = sum(cnts)
    nt = (cnt + EXPERT_TILE - 1) // EXPERT_TILE
    ends = jnp.cumsum(nt)
    base = ((ends - nt) * EXPERT_TILE).astype(jnp.int32)
    n_active = ends[-1:].astype(jnp.int32)
    tile_expert = jnp.minimum(jnp.sum(jnp.arange(max_tiles)[:, None] >= ends[None, :], axis=1),
                              N_EXPERTS - 1).astype(jnp.int32)
    tm = paths[0]["bb"] * paths[0]["tt"]
    assert all(p["bb"] * p["tt"] == tm for p in paths)
    start = base
    for p, c in zip(paths, cnts):
        eid = p["route"][:, 0:2].astype(jnp.int32)
        rank = p["route"][:, 2:4].astype(jnp.int32)
        first_row = jnp.sum(jnp.where(eid[..., None] == jnp.arange(N_EXPERTS), start, 0), axis=-1)
        pos = (first_row + rank) * SUBLANES
        p["pos"] = jnp.transpose(pos.reshape(-1, tm, 2), (0, 2, 1))
        start = start + c
    xs = _dispatch([p["h2"] for p in paths], jnp.concatenate([p["pos"] for p in paths], axis=0), tm,
                   max_tiles * EXPERT_TILE, base + cnt, nt * EXPERT_TILE - cnt, n_active)
    ys = _experts(xs, tile_expert, n_active, wg, wu, wd)
    return [_combine(ys, p["pos"], p["route"], p["x1"], p["mod"], final_norm, p["bb"], p["tt"]) for p in paths]


def _mixers(x, mod, conv_init, s0, prm, *, bb, tt, dn_nprob, dn_nseq, dn_chunk, vn_dtype):
    B, T, _ = x.shape
    u, vn, zqkv, sz, ga, gb, bg = _inproj(x, mod, prm["norm1"], prm["w_main"], prm["w_bg"], prm["ln_v_w"],
                                          prm["ln_v_b"], prm["adt"], bb, tt, vn_dtype)
    nc = T // dn_chunk
    bo, s_new = _deltanet(zqkv, sz, bg, conv_init, s0, prm["conv_w"], prm["dn_norm"], dn_nprob, dn_nseq, dn_chunk, nc)
    L = min(GMLP_CHUNK, T)
    reps = GMLP_CHUNK // L
    wm = prm["w_s_tril"][:, :L, :L]
    if reps > 1:
        blk = (jnp.arange(GMLP_CHUNK)[:, None] // L) == (jnp.arange(GMLP_CHUNK)[None, :] // L)
        wm = jnp.where(blk[None], jnp.tile(wm, (1, reps, reps)), 0.0)
    ws_eff = wm.astype(BF16)
    bs_rows = jnp.tile(jnp.transpose(prm["b_s"][:, :L]), (reps, 1))
    bs_full = jnp.repeat(bs_rows, A_GDIM, axis=1)
    x1, h2, route, counts = _outproj(u, vn, ga, gb, bo, x, mod, ws_eff, bs_full, prm["w_pa"], prm["w_pb"],
                                     prm["w_o"], prm["norm2"], prm["w_r"], prm["b_r"], bb, tt)
    tail = min(T, CONV_W - 1)
    zqkv_tail = zqkv.reshape(B, T, DN_CONV_CH)[:, T - tail:, :]
    conv_new = jnp.concatenate([conv_init[:, SUBLANES - (CONV_W - 1):, :], zqkv_tail], axis=1)[:, -(CONV_W - 1):, :]
    path = dict(h2=h2, route=route, counts=counts, x1=x1, mod=mod, bb=bb, tt=tt)
    return path, conv_new, s_new, vn.reshape(B, T, A_WIDTH)


def kernel(x_prompt, x_sample, c_prompt, c_sample, state_conv, state_delta, w_ada, b_ada, norm1, norm2, w_in, conv_w, a_log, dt_bias, dn_norm, ln_v_w, ln_v_b, w_s, b_s, w_pa, w_pb, w_o, w_rg, b_rg, w_re, b_re, w_e_gate, w_e_up, w_e_down, final_norm):
    depth = w_ada.shape[0]
    assert depth == 1
    bp, tp, _ = x_prompt.shape
    bs_, ts, _ = x_sample.shape
    l = 0
    wi = w_in[l]
    o_qkv = 2 * A_WIDTH
    o_z = o_qkv + DN_CONV_CH
    o_b = o_z + DN_V
    o_ga = o_b + 2 * DN_HEADS
    w_main = jnp.concatenate([wi[:, :o_b], wi[:, o_ga:]], axis=1).astype(BF16)
    w_bg = jnp.pad(wi[:, o_b:o_ga], ((0, 0), (0, LANES - 2 * DN_HEADS))).astype(BF16)
    adt = jnp.stack([jnp.pad(a_log[l], (DN_HEADS, LANES - 2 * DN_HEADS)),
                     jnp.pad(dt_bias[l], (DN_HEADS, LANES - 2 * DN_HEADS))])
    w_r = jnp.pad(jnp.concatenate([w_rg[l], w_re[l]], axis=1), ((0, 0), (0, LANES - N_GROUPS - N_EXPERTS)))
    w_r_hi = w_r.astype(BF16)
    w_r = jnp.concatenate([w_r_hi, (w_r - w_r_hi.astype(F32)).astype(BF16)], axis=1)
    b_r = jnp.pad(jnp.concatenate([b_rg[l], b_re[l]]), (0, LANES - N_GROUPS - N_EXPERTS)).reshape(1, LANES)
    tri = jnp.tril(jnp.ones((GMLP_CHUNK, GMLP_CHUNK), F32))
    prm = dict(
        norm1=norm1[l], norm2=norm2[l], w_main=w_main, w_bg=w_bg, adt=adt, ln_v_w=ln_v_w[l], ln_v_b=ln_v_b[l],
        conv_w=conv_w[l], dn_norm=dn_norm[l], w_s_tril=w_s[l] * tri, b_s=b_s[l],
        w_pa=w_pa[l].astype(BF16), w_pb=w_pb[l].astype(BF16), w_o=w_o[l].astype(BF16), w_r=w_r, b_r=b_r,
        w_e_gate=w_e_gate[l].reshape(N_EXPERTS, D_MODEL, D_EXPERT),
        w_e_up=w_e_up[l].reshape(N_EXPERTS, D_MODEL, D_EXPERT),
        w_e_down=w_e_down[l].reshape(N_EXPERTS, D_EXPERT, D_MODEL),
    )

    c_all = jnp.concatenate([c_prompt, c_sample], axis=0)
    mod = _ada(c_all, w_ada[l], b_ada[l]).reshape(bp + bs_, N_MOD, D_MODEL)
    mod_p, mod_s = mod[:bp], mod[bp:]

    pad_rows = SUBLANES - (CONV_W - 1)
    cinit_p = jnp.zeros((bp, SUBLANES, DN_CONV_CH), F32)
    cinit_s = jnp.pad(state_conv[l], ((0, 0), (pad_rows, 0), (0, 0)))
    s0_p = jnp.zeros((bp, DN_HEADS, DN_KDIM, DN_VDIM), F32)

    path_p, conv_p, delta_p, _ = _mixers(x_prompt, mod_p, cinit_p, s0_p, prm,
                                         bb=1, tt=min(ROW_TILE, tp), dn_nprob=4 if bp % 4 == 0 else 1, dn_nseq=1,
                                         dn_chunk=min(DN_CHUNK, tp), vn_dtype=BF16)
    path_s, conv_s, delta_s, vn_s = _mixers(x_sample, mod_s, cinit_s, state_delta[l], prm,
                                            bb=min(bs_, ROW_TILE // ts), tt=ts, dn_nprob=1,
                                            dn_nseq=min(bs_, DN_CHUNK // ts), dn_chunk=ts, vn_dtype=F32)
    yp, ys = _moe([path_p, path_s], prm["w_e_gate"], prm["w_e_up"], prm["w_e_down"], final_norm)
    return (yp, ys, conv_p[None], conv_s[None], delta_p[None], delta_s[None], vn_s[None])
```

```python
import functools

import jax
import jax.numpy as jnp
from jax import lax
from jax.experimental import pallas as pl
from jax.experimental.pallas import tpu as pltpu

F32 = jnp.float32
BF16 = jnp.bfloat16

D_MODEL = 1024
A_WIDTH = 1024
A_GROUPS = 4
A_GDIM = A_WIDTH // A_GROUPS
GMLP_CHUNK = 128
DN_HEADS = 8
DN_KDIM = 128
DN_VDIM = 128
DN_QK = DN_HEADS * DN_KDIM
DN_V = DN_HEADS * DN_VDIM
DN_CONV_CH = 2 * DN_QK + DN_V
CONV_W = 4
DN_CHUNK = 64
N_GROUPS = 4
EXP_PER_GROUP = 8
N_EXPERTS = N_GROUPS * EXP_PER_GROUP
D_EXPERT = 256
N_MOD = 6
EPS = 1e-6

LANES = 128
SUBLANES = 8
ROUTER_LANE0 = N_GROUPS
VMEM_LIMIT = 56 * 1024 * 1024
ROW_TILE = 512
EXPERT_TILE = 512
EXPERTS_PER_STEP = 1


def _sigmoid(x):
    return 1.0 / (1.0 + jnp.exp(-x))


def _gelu(x):
    return 0.5 * x * (1.0 + lax.erf(x * 0.7071067811865476))


def _softplus(x):
    return jnp.maximum(x, 0.0) + jnp.log(1.0 + jnp.exp(-jnp.abs(x)))


def _dot(a, b):
    return jnp.dot(a.astype(BF16), b.astype(BF16), preferred_element_type=F32)


def _dot_nt(a, b):
    return lax.dot_general(a.astype(BF16), b.astype(BF16), (((1,), (1,)), ((), ())), preferred_element_type=F32)


def _dot_tn(a, b):
    return lax.dot_general(a.astype(BF16), b.astype(BF16), (((0,), (0,)), ((), ())), preferred_element_type=F32)


def _split3(x):
    hi = x.astype(BF16)
    r1 = x - hi.astype(F32)
    mid = r1.astype(BF16)
    lo = (r1 - mid.astype(F32)).astype(BF16)
    return hi, mid, lo


def _dot_exact_lhs(a_bf16, x):
    hi, mid, lo = _split3(x)
    f = lambda p: jnp.dot(a_bf16, p, preferred_element_type=F32)
    return f(hi) + f(mid) + f(lo)


_PASSES = {"invp": 3, "invn": 3, "inv_hi": 1, "inv_kmax": 8, "sol": 1, "kk": 1, "ws": 1, "attn": 1, "state": 1}


def _hilo(x):
    hi = x.astype(BF16)
    lo = (x - hi.astype(F32)).astype(BF16)
    return hi, lo


def _mm(site, a, b, dims=(((1,), (0,)), ((), ()))):
    f = lambda p, q: lax.dot_general(p, q, dims, preferred_element_type=F32)
    if _PASSES[site] == 1:
        return f(a.astype(BF16), b.astype(BF16))
    ah, al = _hilo(a)
    bh, bl = _hilo(b)
    return f(ah, bh) + (f(ah, bl) + f(al, bh))


_NT = (((1,), (1,)), ((), ()))
_TN = (((0,), (0,)), ((), ()))

def _params(sem):
    return pltpu.CompilerParams(dimension_semantics=sem, vmem_limit_bytes=VMEM_LIMIT)


def _ada_kernel(c_ref, w_ref, b_ref, o_ref):
    c = c_ref[...]
    o_ref[...] = _dot(c * _sigmoid(c), w_ref[...]) + b_ref[...]


def _ada(c_all, w_ada, b_ada):
    n = c_all.shape[0]
    width = w_ada.shape[1]
    bn = 512
    return pl.pallas_call(
        _ada_kernel,
        grid=(width // bn,),
        in_specs=[pl.BlockSpec((n, D_MODEL), lambda j: (0, 0)),
                  pl.BlockSpec((D_MODEL, bn), lambda j: (0, j)),
                  pl.BlockSpec((1, bn), lambda j: (0, j))],
        out_specs=pl.BlockSpec((n, bn), lambda j: (0, j)),
        out_shape=jax.ShapeDtypeStruct((n, width), F32),
        compiler_params=_params(("arbitrary",)),
        name="ada_mod",
    )(c_all, w_ada, b_ada.reshape(1, width))


def _inproj_kernel(x_ref, mod_ref, n1_ref, wa_ref, wb_ref, wbg_ref, lnw_ref, lnb_ref, adt_ref,
                   u_ref, vn_ref, qkv_ref, sz_ref, ga_ref, gb_ref, bg_ref):
    tm = u_ref.shape[0]
    x = x_ref[...]
    y = x * lax.rsqrt(jnp.mean(x * x, axis=-1, keepdims=True) + EPS) * n1_ref[...]
    mod = mod_ref[...]
    h = y * (1.0 + mod[:, 1:2, :]) + mod[:, 0:1, :]
    hb = h.reshape(tm, D_MODEL).astype(BF16)
    zbg = jnp.dot(hb, wbg_ref[...], preferred_element_type=F32)
    lane = lax.broadcasted_iota(jnp.int32, zbg.shape, 1)
    adt = adt_ref[...]
    g = -jnp.exp(adt[0:1, :]) * _softplus(zbg + adt[1:2, :])
    bg_ref[...] = jnp.where(lane < DN_HEADS, _sigmoid(zbg), g)

    n_a = wa_ref.shape[1] // 1024
    blk = lambda j: jnp.dot(hb, wa_ref[:, j * 1024:(j + 1) * 1024] if j < n_a else
                            wb_ref[:, (j - n_a) * 1024:(j - n_a + 1) * 1024], preferred_element_type=F32)
    u_ref[...] = _gelu(blk(0)).astype(u_ref.dtype)
    a = _gelu(blk(1))
    ac = a - jnp.mean(a, axis=-1, keepdims=True)
    var = jnp.mean(ac * ac, axis=-1, keepdims=True)
    vn_ref[...] = (ac * lax.rsqrt(var + EPS) * lnw_ref[...] + lnb_ref[...]).astype(vn_ref.dtype)
    for j in range(3):
        qkv_ref[:, j * 1024:(j + 1) * 1024] = blk(2 + j)
    z = blk(5)
    sz_ref[...] = (z * _sigmoid(z)).astype(sz_ref.dtype)
    ga_ref[...] = _sigmoid(blk(6)).astype(ga_ref.dtype)
    gb_ref[...] = _sigmoid(blk(7)).astype(gb_ref.dtype)


def _inproj(x, mod, norm1, w_a, w_b, w_bg, ln_w, ln_b, adt, bb, tt, vn_dtype):
    B, T, _ = x.shape
    tm = bb * tt
    tpb = T // tt
    n_tiles = (B // bb) * tpb
    M = B * T
    const = lambda i: (0, 0)
    rows = lambda i: (i, 0)
    resident = dict(pipeline_mode=pl.Buffered(1))
    in_specs = [pl.BlockSpec((bb, tt, D_MODEL), lambda i: (i // tpb, i % tpb, 0)),
                pl.BlockSpec((bb, N_MOD, D_MODEL), lambda i: (i // tpb, 0, 0)),
                pl.BlockSpec((1, D_MODEL), const),
                pl.BlockSpec(w_a.shape, const, **resident),
                pl.BlockSpec(w_b.shape, const, **resident),
                pl.BlockSpec((D_MODEL, LANES), const, **resident),
                pl.BlockSpec((1, A_WIDTH), const),
                pl.BlockSpec((1, A_WIDTH), const),
                pl.BlockSpec((2, LANES), const)]
    args = [x, mod, norm1.reshape(1, D_MODEL), w_a, w_b, w_bg, ln_w.reshape(1, A_WIDTH), ln_b.reshape(1, A_WIDTH),
            adt]
    out_specs = [pl.BlockSpec((tm, A_WIDTH), rows),
                 pl.BlockSpec((tm, A_WIDTH), rows),
                 pl.BlockSpec((tm, DN_CONV_CH), rows),
                 pl.BlockSpec((tm, DN_V), rows),
                 pl.BlockSpec((tm, D_MODEL), rows),
                 pl.BlockSpec((tm, D_MODEL), rows),
                 pl.BlockSpec((tm, LANES), rows)]
    out_shape = [jax.ShapeDtypeStruct((M, A_WIDTH), BF16),
                 jax.ShapeDtypeStruct((M, A_WIDTH), vn_dtype),
                 jax.ShapeDtypeStruct((M, DN_CONV_CH), F32),
                 jax.ShapeDtypeStruct((M, DN_V), BF16),
                 jax.ShapeDtypeStruct((M, D_MODEL), BF16),
                 jax.ShapeDtypeStruct((M, D_MODEL), BF16),
                 jax.ShapeDtypeStruct((M, LANES), F32)]
    return pl.pallas_call(
        _inproj_kernel,
        grid=(n_tiles,),
        in_specs=in_specs,
        out_specs=out_specs,
        out_shape=out_shape,
        compiler_params=_params(("arbitrary",)),
        name="in_proj",
    )(*args)


def _inv_unit_lower_minus_eye(lmats, nilpotent):
    ns = [-l for l in lmats]
    ps = list(lmats)
    k = 2
    while k < nilpotent:
        early = k <= _PASSES["inv_kmax"]
        ps = [_mm("invp" if early else "inv_hi", p, p) for p in ps]
        ns = [n + p + _mm("invn" if early else "inv_hi", n, p) for n, p in zip(ns, ps)]
        k *= 2
    return ns


def _head_sumsq(x):
    pair = 2 * DN_KDIM
    r = lax.broadcasted_iota(jnp.int32, (pair, pair), 0) >= DN_KDIM
    c = lax.broadcasted_iota(jnp.int32, (pair, pair), 1) >= DN_KDIM
    ones2 = jnp.where(r == c, 1.0, 0.0).astype(BF16)
    sq = (x * x).astype(BF16)
    return jnp.concatenate([jnp.dot(sq[:, p * pair:(p + 1) * pair], ones2, preferred_element_type=F32)
                            for p in range(x.shape[1] // pair)], axis=1)


def _deltanet_kernel(q_ref, k_ref, v_ref, z_ref, bg_ref, cinit_ref, s0_ref, cw_ref, dnn_ref,
                     bo_ref, sn_ref, xp_s, s_s, *, nprob, nseq, C):
    R = nseq * C
    c = pl.program_id(1)
    nc = pl.num_programs(1)

    @pl.when(c == 0)
    def _():
        xp_s[:, 0:SUBLANES, :] = cinit_ref[...]
        s_s[...] = s0_ref[...]

    @pl.when(c > 0)
    def _():
        xp_s[:, 0:SUBLANES, :] = xp_s[:, C:C + SUBLANES, :]

    for p in range(nprob):
        for s in range(nseq):
            i = p * nseq + s
            xp_s[i, SUBLANES:SUBLANES + C, 0:DN_QK] = q_ref[p, s * C:(s + 1) * C, :]
            xp_s[i, SUBLANES:SUBLANES + C, DN_QK:2 * DN_QK] = k_ref[p, s * C:(s + 1) * C, :]
            xp_s[i, SUBLANES:SUBLANES + C, 2 * DN_QK:DN_CONV_CH] = v_ref[p, s * C:(s + 1) * C, :]

    cw = cw_ref[...]
    base = SUBLANES - (CONV_W - 1)
    acc = None
    for j in range(CONV_W):
        term = xp_s[:, base + j:base + j + C, :] * cw[j:j + 1, :]
        acc = term if acc is None else acc + term
    qkv = acc.reshape(nprob * R, DN_CONV_CH)
    qkv = qkv * _sigmoid(qkv)
    q_all = qkv[:, 0:DN_QK]
    k_all = qkv[:, DN_QK:2 * DN_QK]
    v_all = qkv[:, 2 * DN_QK:DN_CONV_CH]
    qn_all = q_all * lax.rsqrt(_head_sumsq(q_all) + EPS) * (DN_KDIM ** -0.5)
    kn_all = k_all * lax.rsqrt(_head_sumsq(k_all) + EPS)

    row = lax.broadcasted_iota(jnp.int32, (R, R), 0)
    col = lax.broadcasted_iota(jnp.int32, (R, R), 1)
    if nseq > 1:
        shift = C.bit_length() - 1
        same = lax.shift_right_logical(row, shift) == lax.shift_right_logical(col, shift)
        tril = same & (col <= row)
        strict = same & (col < row)
    else:
        tril = col <= row
        strict = col < row
    tril_b = jnp.where(tril, 1.0, 0.0).astype(BF16)
    bg = [bg_ref[p] for p in range(nprob)]
    gc = [_dot_exact_lhs(tril_b, bg[p]) for p in range(nprob)]
    gct = [g.T for g in gc]
    dnn = dnn_ref[...]

    IT = [(p, h) for p in range(nprob) for h in range(DN_HEADS)]
    NI = range(len(IT))
    hsl = [slice(h * DN_KDIM, (h + 1) * DN_KDIM) for h in range(DN_HEADS)]
    rows = [slice(p * R, (p + 1) * R) for p in range(nprob)]
    qn = [qn_all[rows[p], hsl[h]] for p, h in IT]
    kn = [kn_all[rows[p], hsl[h]] for p, h in IT]
    vh = [v_all[rows[p], hsl[h]] for p, h in IT]
    beta = [bg[p][:, h:h + 1] for p, h in IT]
    gcol = [gc[p][:, DN_HEADS + h:DN_HEADS + h + 1] for p, h in IT]
    grow = [gct[p][DN_HEADS + h:DN_HEADS + h + 1, :] for p, h in IT]
    decay = [jnp.exp(jnp.minimum(gcol[i] - grow[i], 0.0)) for i in NI]
    eg = [jnp.exp(gcol[i]) for i in NI]
    kb = [kn[i] * beta[i] for i in NI]
    vb = [vh[i] * beta[i] for i in NI]
    a2 = [_mm("kk", jnp.concatenate([kb[i], qn[i]], axis=0), kn[i], _NT) for i in NI]
    lmat = [jnp.where(strict, a2[i][:R] * decay[i], 0.0) for i in NI]
    attn = [jnp.where(tril, a2[i][R:] * decay[i], 0.0) for i in NI]
    tn = _inv_unit_lower_minus_eye(lmat, C)
    rhs = [jnp.concatenate([vb[i], kb[i] * eg[i]], axis=1) for i in NI]
    sol = [rhs[i] + _mm("sol", tn[i], rhs[i]) for i in NI]
    u = [x[:, :DN_VDIM] for x in sol]
    w = [x[:, DN_VDIM:] for x in sol]
    qg = [qn[i] * eg[i] for i in NI]
    vnew = [[None] * nseq for _ in NI]
    qs = [[None] * nseq for _ in NI]
    for s in range(nseq):
        rs = slice(s * C, (s + 1) * C)
        st = [s_s[p * nseq + s, h] for p, h in IT]
        ws = [_mm("ws", jnp.concatenate([w[i][rs], qg[i][rs]], axis=0), st[i]) for i in NI]
        for i in NI:
            vnew[i][s] = u[i][rs] - ws[i][:C]
            qs[i][s] = ws[i][C:]
        glast = [gcol[i][(s + 1) * C - 1:(s + 1) * C, :] for i in NI]
        kd = [kn[i][rs] * jnp.exp(glast[i] - gcol[i][rs]) for i in NI]
        upd = [_mm("state", kd[i], vnew[i][s], _TN) for i in NI]
        for i, (p, h) in enumerate(IT):
            s_s[p * nseq + s, h] = st[i] * jnp.exp(glast[i]) + upd[i]
    cat = lambda parts: parts[0] if nseq == 1 else jnp.concatenate(parts, axis=0)
    o = [cat(qs[i]) + _mm("attn", attn[i], cat(vnew[i])) for i in NI]
    o_all = jnp.concatenate([jnp.concatenate(o[p * DN_HEADS:(p + 1) * DN_HEADS], axis=1) for p in range(nprob)],
                            axis=0)
    dnn_all = jnp.concatenate([dnn] * DN_HEADS, axis=1)
    on = o_all * lax.rsqrt(_head_sumsq(o_all) * (1.0 / DN_VDIM) + EPS) * dnn_all
    for p in range(nprob):
        bo_ref[p] = (on[rows[p]] * z_ref[p].astype(F32)).astype(bo_ref.dtype)

    @pl.when(c == nc - 1)
    def _():
        sn_ref[...] = s_s[...]


def _deltanet(zqkv, sz, bg, conv_init, s0, conv_w, dn_norm, nprob, nseq, C, nc):
    M = zqkv.shape[0]
    R = nseq * C
    G = M // (R * nc)
    view = lambda a: a.reshape(G, R * nc, a.shape[-1])
    rmap = lambda col: (lambda g, c: (g, c, col))
    nsq = nprob * nseq
    state_spec = pl.BlockSpec((nsq, DN_HEADS, DN_KDIM, DN_VDIM), lambda g, c: (g, 0, 0, 0))
    bo, s_new = pl.pallas_call(
        functools.partial(_deltanet_kernel, nprob=nprob, nseq=nseq, C=C),
        grid=(G // nprob, nc),
        in_specs=[pl.BlockSpec((nprob, R, DN_QK), rmap(0)),
                  pl.BlockSpec((nprob, R, DN_QK), rmap(1)),
                  pl.BlockSpec((nprob, R, DN_V), rmap(2)),
                  pl.BlockSpec((nprob, R, DN_V), rmap(0)),
                  pl.BlockSpec((nprob, R, LANES), rmap(0)),
                  pl.BlockSpec((nsq, SUBLANES, DN_CONV_CH), lambda g, c: (g, 0, 0)),
                  state_spec,
                  pl.BlockSpec((CONV_W, DN_CONV_CH), lambda g, c: (0, 0)),
                  pl.BlockSpec((1, DN_VDIM), lambda g, c: (0, 0))],
        out_specs=[pl.BlockSpec((nprob, R, DN_V), rmap(0)), state_spec],
        out_shape=[jax.ShapeDtypeStruct((G, R * nc, DN_V), BF16),
                   jax.ShapeDtypeStruct(s0.shape, F32)],
        scratch_shapes=[pltpu.VMEM((nsq, C + SUBLANES, DN_CONV_CH), F32),
                        pltpu.VMEM((nsq, DN_HEADS, DN_KDIM, DN_VDIM), F32)],
        compiler_params=_params(("arbitrary", "arbitrary")),
        name="deltanet",
    )(view(zqkv), view(zqkv), view(zqkv), view(sz), view(bg), conv_init, s0, conv_w, dn_norm.reshape(1, DN_VDIM))
    return bo.reshape(M, DN_V), s_new


def _outproj_kernel(u_ref, vn_ref, ga_ref, gb_ref, bo_ref, x_ref, mod_ref, ws_ref, bs_ref, wpa_ref, wpb_ref,
                    wo_ref, n2_ref, wr_ref, br_ref, before_ref, x1_ref, h2_ref, rt_ref, cnt_ref, cnt_s):
    tm = u_ref.shape[0]
    vn = vn_ref[...].astype(BF16)
    parts = []
    for r in range(tm // GMLP_CHUNK):
        rs = slice(r * GMLP_CHUNK, (r + 1) * GMLP_CHUNK)
        row_parts = []
        for g in range(A_GROUPS):
            gs = slice(g * A_GDIM, (g + 1) * A_GDIM)
            row_parts.append(jnp.dot(ws_ref[g], vn[rs, gs], preferred_element_type=F32))
        parts.append(jnp.concatenate(row_parts, axis=1) + bs_ref[...])
    sv = parts[0] if len(parts) == 1 else jnp.concatenate(parts, axis=0)
    a_out = u_ref[...].astype(F32) * sv
    pa = _dot(a_out, wpa_ref[...])
    pb = _dot(bo_ref[...], wpb_ref[...])
    m = ga_ref[...].astype(F32) * pa + gb_ref[...].astype(F32) * pb
    mix = _dot(m, wo_ref[...])
    mod = mod_ref[...]
    x = x_ref[...]
    x1 = x + mod[:, 2:3, :] * mix.reshape(x.shape)
    x1_ref[...] = x1
    y = x1 * lax.rsqrt(jnp.mean(x1 * x1, axis=-1, keepdims=True) + EPS) * n2_ref[...]
    h2 = (y * (1.0 + mod[:, 4:5, :]) + mod[:, 3:4, :]).reshape(tm, D_MODEL)
    for j in range(SUBLANES):
        h2_ref[pl.ds(j, tm, stride=SUBLANES), :] = h2[:, j * LANES:(j + 1) * LANES]

    hh, hl = _hilo(h2)
    wr = wr_ref[...]
    p = jnp.dot(hh, wr, preferred_element_type=F32)
    logits = p[:, :LANES] + (p[:, LANES:] + jnp.dot(hl, wr[:, :LANES], preferred_element_type=F32)) + br_ref[...]
    lane = lax.broadcasted_iota(jnp.int32, logits.shape, 1)
    lanef = lane.astype(F32)
    neg = jnp.float32(-jnp.inf)
    big = jnp.float32(1e9)
    gl = jnp.where(lane < N_GROUPS, logits, neg)
    gmax = jnp.max(gl, axis=-1, keepdims=True)
    gsel = jnp.min(jnp.where(gl == gmax, lanef, big), axis=-1, keepdims=True)
    gp = 1.0 / jnp.sum(jnp.exp(gl - gmax), axis=-1, keepdims=True)
    lo = ROUTER_LANE0 + EXP_PER_GROUP * gsel
    in_grp = (lanef >= lo) & (lanef < lo + EXP_PER_GROUP)
    el = jnp.where(in_grp, logits, neg)
    m1 = jnp.max(el, axis=-1, keepdims=True)
    i1 = jnp.min(jnp.where(el == m1, lanef, big), axis=-1, keepdims=True)
    el2 = jnp.where(lanef == i1, neg, el)
    m2 = jnp.max(el2, axis=-1, keepdims=True)
    i2 = jnp.min(jnp.where(el2 == m2, lanef, big), axis=-1, keepdims=True)
    ex = jnp.exp(m2 - m1)
    w1 = gp / (1.0 + ex)
    w2 = gp * ex / (1.0 + ex)
    e1 = i1 - ROUTER_LANE0
    e2 = i2 - ROUTER_LANE0

    @pl.when(pl.program_id(0) == 0)
    def _():
        cnt_s[...] = jnp.zeros_like(cnt_s)

    oh1 = jnp.where(lanef == e1, 1.0, 0.0)
    oh2 = jnp.where(lanef == e2, 1.0, 0.0)
    oh = oh1 + oh2
    seen = jnp.dot(before_ref[...], oh.astype(BF16), preferred_element_type=F32) + cnt_s[...]
    r1 = jnp.sum(oh1 * seen, axis=-1, keepdims=True)
    r2 = jnp.sum(oh2 * seen, axis=-1, keepdims=True)
    cnt_s[...] += jnp.sum(oh, axis=0, keepdims=True)
    cnt_ref[...] = cnt_s[...]
    rec = jnp.zeros_like(logits)
    for k, val in enumerate((e1, e2, r1, r2, w1, w2)):
        rec = jnp.where(lane == k, val, rec)
    rt_ref[...] = rec


def _outproj(u, vn, ga, gb, bo, x, mod, ws_eff, bs_full, w_pa, w_pb, w_o, norm2, w_r, b_r, bb, tt):
    B, T, _ = x.shape
    tm = bb * tt
    tpb = T // tt
    n_tiles = (B // bb) * tpb
    M = B * T
    cmap = lambda col: (lambda i: (i, col))
    full2 = lambda i: (0, 0)
    return pl.pallas_call(
        _outproj_kernel,
        grid=(n_tiles,),
        in_specs=[pl.BlockSpec((tm, A_WIDTH), cmap(0)),
                  pl.BlockSpec((tm, A_WIDTH), cmap(0)),
                  pl.BlockSpec((tm, D_MODEL), cmap(0)),
                  pl.BlockSpec((tm, D_MODEL), cmap(0)),
                  pl.BlockSpec((tm, DN_V), cmap(0)),
                  pl.BlockSpec((bb, tt, D_MODEL), lambda i: (i // tpb, i % tpb, 0)),
                  pl.BlockSpec((bb, N_MOD, D_MODEL), lambda i: (i // tpb, 0, 0)),
                  pl.BlockSpec((A_GROUPS, GMLP_CHUNK, GMLP_CHUNK), lambda i: (0, 0, 0)),
                  pl.BlockSpec((GMLP_CHUNK, A_WIDTH), full2),
                  pl.BlockSpec((A_WIDTH, D_MODEL), full2),
                  pl.BlockSpec((DN_V, D_MODEL), full2),
                  pl.BlockSpec((D_MODEL, D_MODEL), full2),
                  pl.BlockSpec((1, D_MODEL), full2),
                  pl.BlockSpec((D_MODEL, 2 * LANES), full2),
                  pl.BlockSpec((1, LANES), full2),
                  pl.BlockSpec((tm, tm), full2)],
        out_specs=[pl.BlockSpec((bb, tt, D_MODEL), lambda i: (i // tpb, i % tpb, 0)),
                   pl.BlockSpec((tm * SUBLANES, LANES), cmap(0)),
                   pl.BlockSpec((tm, LANES), cmap(0)),
                   pl.BlockSpec((1, LANES), full2)],
        out_shape=[jax.ShapeDtypeStruct((B, T, D_MODEL), F32),
                   jax.ShapeDtypeStruct((M * SUBLANES, LANES), F32),
                   jax.ShapeDtypeStruct((M, LANES), F32),
                   jax.ShapeDtypeStruct((1, LANES), F32)],
        scratch_shapes=[pltpu.VMEM((1, LANES), F32)],
        compiler_params=_params(("arbitrary",)),
        name="out_proj",
    )(u, vn, ga, gb, bo, x, mod, ws_eff, bs_full, w_pa, w_pb, w_o, norm2.reshape(1, D_MODEL), w_r, b_r,
      jnp.tril(jnp.ones((tm, tm), BF16), -1))


def _tile_copy(src_ref, src_off, dst_ref, dst_off, sem):
    return pltpu.make_async_copy(src_ref.at[pl.ds(pl.multiple_of(src_off, SUBLANES), SUBLANES), :],
                                 dst_ref.at[pl.ds(pl.multiple_of(dst_off, SUBLANES), SUBLANES), :], sem)


def _to_matrix(ref, n):
    return jnp.concatenate([ref[pl.ds(j, n, stride=SUBLANES), :] for j in range(SUBLANES)], axis=1)


def _zero_fill(pad_lo_ref, pad_n_ref, na_ref, xs_ref, zero_s, zsem):
    blk = EXPERT_TILE * SUBLANES
    n_tiles = xs_ref.shape[0] // blk
    zero_s[...] = jnp.zeros_like(zero_s)
    run = lambda off, rows: pltpu.make_async_copy(
        zero_s.at[pl.ds(0, rows * SUBLANES), :],
        xs_ref.at[pl.ds(pl.multiple_of(off * SUBLANES, SUBLANES), rows * SUBLANES), :], zsem)
    sizes = [1 << b for b in reversed(range(EXPERT_TILE.bit_length() - 1))]
    for wait in (False, True):
        for e in range(N_EXPERTS):
            off = pad_lo_ref[e]
            for rows in sizes:
                bit = pad_n_ref[e] & rows

                @pl.when(bit != 0)
                def _(off=off, rows=rows):
                    run(0, rows).wait() if wait else run(off, rows).start()

                off = off + bit

        def idle(i, carry):
            run(0, EXPERT_TILE).wait() if wait else run(i * EXPERT_TILE, EXPERT_TILE).start()
            return carry

        lax.fori_loop(na_ref[0], n_tiles, idle, 0)


def _dispatch_kernel(pad_lo_ref, pad_n_ref, na_ref, pos_ref, *refs, first_tile):
    n_paths = len(first_tile) - 1
    h_refs = refs[:n_paths]
    xs_ref, zero_s, sem, zsem = refs[n_paths:]
    i = pl.program_id(0)

    @pl.when(i == 0)
    def _():
        _zero_fill(pad_lo_ref, pad_n_ref, na_ref, xs_ref, zero_s, zsem)

    tm = h_refs[0].shape[0] // SUBLANES
    for p, h_ref in enumerate(h_refs):
        @pl.when((i >= first_tile[p]) & (i < first_tile[p + 1]))
        def _(h_ref=h_ref):
            def issue(t, carry):
                for k in range(2):
                    _tile_copy(h_ref, t * SUBLANES, xs_ref, pos_ref[0, k, t], sem).start(priority=k)
                return carry

            lax.fori_loop(0, tm, issue, 0, unroll=8)

            def drain(t, carry):
                for k in range(2):
                    _tile_copy(h_ref, 0, xs_ref, 0, sem).wait()
                return carry

            lax.fori_loop(0, tm, drain, 0, unroll=8)


def _dispatch(h2s, pos, tm, n_rows, pad_lo, pad_n, n_active):
    blk = tm * SUBLANES
    tiles = [h.shape[0] // blk for h in h2s]
    first_tile = [sum(tiles[:p]) for p in range(len(tiles) + 1)]
    hmap = lambda p: (lambda i, *_: (jnp.clip(i - first_tile[p], 0, tiles[p] - 1), 0))
    grid_spec = pltpu.PrefetchScalarGridSpec(
        num_scalar_prefetch=3,
        grid=(first_tile[-1],),
        in_specs=[pl.BlockSpec((1, 2, tm), lambda i, *_: (i, 0, 0), memory_space=pltpu.SMEM)]
                 + [pl.BlockSpec((blk, LANES), hmap(p)) for p in range(len(tiles))],
        out_specs=pl.BlockSpec(memory_space=pl.ANY),
        scratch_shapes=[pltpu.VMEM((EXPERT_TILE * SUBLANES, LANES), F32),
                        pltpu.SemaphoreType.DMA(()), pltpu.SemaphoreType.DMA(())],
    )
    return pl.pallas_call(
        functools.partial(_dispatch_kernel, first_tile=tuple(first_tile)),
        grid_spec=grid_spec,
        out_shape=jax.ShapeDtypeStruct((n_rows * SUBLANES, LANES), F32),
        compiler_params=_params(("arbitrary",)),
        name="moe_dispatch",
    )(pad_lo, pad_n, n_active, pos, *h2s)


def _expert_kernel(te_ref, na_ref, x_ref, *refs):
    del te_ref
    w_refs, o_ref = refs[:-1], refs[-1]
    blk = EXPERT_TILE * SUBLANES
    live = EXPERTS_PER_STEP * pl.program_id(0) < na_ref[0]

    @pl.when(live)
    def _():
        TR = range(EXPERTS_PER_STEP)
        wg, wu, wd = ([w_refs[3 * t + k][0] for t in TR] for k in range(3))
        x = [_to_matrix(x_ref.at[pl.ds(t * blk, blk), :], EXPERT_TILE).astype(BF16) for t in TR]
        a = [_dot(x[t], wg[t]) for t in TR]
        u = [_dot(x[t], wu[t]) for t in TR]
        o = [_dot(a[t] * _sigmoid(a[t]) * u[t], wd[t]) for t in TR]
        for t in TR:
            for j in range(SUBLANES):
                o_ref[pl.ds(t * blk + j, EXPERT_TILE, stride=SUBLANES), :] = o[t][:, j * LANES:(j + 1) * LANES]

    @pl.when(jnp.logical_not(live))
    def _():
        o_ref[...] = jnp.zeros_like(o_ref)


def _experts(xs, tile_expert, n_active, wg, wu, wd):
    blk = EXPERT_TILE * SUBLANES * EXPERTS_PER_STEP
    n_steps = xs.shape[0] // blk
    last = lambda na: jnp.maximum(na[0] - 1, 0)
    xmap = lambda i, te, na: (jnp.minimum(i, last(na) // EXPERTS_PER_STEP), 0)
    wmap = lambda t: (lambda i, te, na: (te[jnp.minimum(EXPERTS_PER_STEP * i + t, last(na))], 0, 0))
    w_specs = []
    for t in range(EXPERTS_PER_STEP):
        w_specs += [pl.BlockSpec((1, D_MODEL, D_EXPERT), wmap(t)),
                    pl.BlockSpec((1, D_MODEL, D_EXPERT), wmap(t)),
                    pl.BlockSpec((1, D_EXPERT, D_MODEL), wmap(t))]
    grid_spec = pltpu.PrefetchScalarGridSpec(
        num_scalar_prefetch=2,
        grid=(n_steps,),
        in_specs=[pl.BlockSpec((blk, LANES), xmap)] + w_specs,
        out_specs=pl.BlockSpec((blk, LANES), lambda i, te, na: (i, 0)),
    )
    return pl.pallas_call(
        _expert_kernel,
        grid_spec=grid_spec,
        out_shape=jax.ShapeDtypeStruct(xs.shape, F32),
        compiler_params=_params(("arbitrary",)),
        name="moe_experts",
    )(tile_expert, n_active, xs, *([wg, wu, wd] * EXPERTS_PER_STEP))


def _combine_kernel(pos_ref, pos_next_ref, rt_ref, x1_ref, mod_ref, fn_ref, ys_ref, y_ref, g_s, sems):
    tm = rt_ref.shape[0]
    i = pl.program_id(0)
    n = pl.num_programs(0)
    slot = i % 2

    def gather(p_ref, s):
        def issue(t, carry):
            for k in range(2):
                _tile_copy(ys_ref, p_ref[0, k, t], g_s.at[s, k], t * SUBLANES, sems.at[s]).start(priority=k)
            return carry

        lax.fori_loop(0, tm, issue, 0, unroll=8)

    @pl.when(i == 0)
    def _():
        gather(pos_ref, slot)

    @pl.when(i + 1 < n)
    def _():
        gather(pos_next_ref, 1 - slot)

    def drain(t, carry):
        for k in range(2):
            _tile_copy(ys_ref, 0, g_s.at[slot, k], 0, sems.at[slot]).wait()
        return carry

    lax.fori_loop(0, tm, drain, 0, unroll=8)

    rt = rt_ref[...]
    moe = rt[:, 4:5] * _to_matrix(g_s.at[slot, 0], tm) + rt[:, 5:6] * _to_matrix(g_s.at[slot, 1], tm)
    mod = mod_ref[...]
    x1 = x1_ref[...]
    x2 = x1 + mod[:, 5:6, :] * moe.reshape(x1.shape)
    y_ref[...] = x2 * lax.rsqrt(jnp.mean(x2 * x2, axis=-1, keepdims=True) + EPS) * fn_ref[...]


def _combine(ys, pos, route, x1, mod, final_norm, bb, tt):
    B, T, _ = x1.shape
    tm = bb * tt
    tpb = T // tt
    n_tiles = (B // bb) * tpb
    xmap = lambda i: (i // tpb, i % tpb, 0)
    return pl.pallas_call(
        _combine_kernel,
        grid=(n_tiles,),
        in_specs=[pl.BlockSpec((1, 2, tm), lambda i: (i, 0, 0), memory_space=pltpu.SMEM),
                  pl.BlockSpec((1, 2, tm), lambda i: (jnp.minimum(i + 1, n_tiles - 1), 0, 0),
                               memory_space=pltpu.SMEM),
                  pl.BlockSpec((tm, LANES), lambda i: (i, 0)),
                  pl.BlockSpec((bb, tt, D_MODEL), xmap),
                  pl.BlockSpec((bb, N_MOD, D_MODEL), lambda i: (i // tpb, 0, 0)),
                  pl.BlockSpec((1, D_MODEL), lambda i: (0, 0)),
                  pl.BlockSpec(memory_space=pl.ANY)],
        out_specs=pl.BlockSpec((bb, tt, D_MODEL), xmap),
        out_shape=jax.ShapeDtypeStruct((B, T, D_MODEL), F32),
        scratch_shapes=[pltpu.VMEM((2, 2, tm * SUBLANES, LANES), F32), pltpu.SemaphoreType.DMA((2,))],
        compiler_params=_params(("arbitrary",)),
        name="moe_combine",
    )(pos, pos, route, x1, mod, final_norm.reshape(1, D_MODEL), ys)


def _moe(paths, wg, wu, wd, final_norm):
    n_assign = sum(2 * p["route"].shape[0] for p in paths)
    max_tiles = (n_assign + N_EXPERTS * (EXPERT_TILE - 1)) // EXPERT_TILE
    max_tiles += -max_tiles % EXPERTS_PER_STEP
    cnts = [p["counts"][0, :N_EXPERTS].astype(jnp.int32) for p in paths]
    cnt = sum(cnts)
    nt = (cnt + EXPERT_TILE - 1) // EXPERT_TILE
    ends = jnp.cumsum(nt)
    base = ((ends - nt) * EXPERT_TILE).astype(jnp.int32)
    n_active = ends[-1:].astype(jnp.int32)
    tile_expert = jnp.minimum(jnp.sum(jnp.arange(max_tiles)[:, None] >= ends[None, :], axis=1),
                              N_EXPERTS - 1).astype(jnp.int32)
    tm = paths[0]["bb"] * paths[0]["tt"]
    assert all(p["bb"] * p["tt"] == tm for p in paths)
    start = base
    for p, c in zip(paths, cnts):
        eid = p["route"][:, 0:2].astype(jnp.int32)
        rank = p["route"][:, 2:4].astype(jnp.int32)
        first_row = jnp.sum(jnp.where(eid[..., None] == jnp.arange(N_EXPERTS), start, 0), axis=-1)
        pos = (first_row + rank) * SUBLANES
        p["pos"] = jnp.transpose(pos.reshape(-1, tm, 2), (0, 2, 1))
        start = start + c
    xs = _dispatch([p["h2"] for p in paths], jnp.concatenate([p["pos"] for p in paths], axis=0), tm,
                   max_tiles * EXPERT_TILE, base + cnt, nt * EXPERT_TILE - cnt, n_active)
    ys = _experts(xs, tile_expert, n_active, wg, wu, wd)
    return [_combine(ys, p["pos"], p["route"], p["x1"], p["mod"], final_norm, p["bb"], p["tt"]) for p in paths]


def _mixers(x, mod, conv_init, s0, prm, *, bb, tt, dn_nprob, dn_nseq, dn_chunk, vn_dtype):
    B, T, _ = x.shape
    u, vn, zqkv, sz, ga, gb, bg = _inproj(x, mod, prm["norm1"], prm["w_a"], prm["w_b"], prm["w_bg"], prm["ln_v_w"],
                                          prm["ln_v_b"], prm["adt"], bb, tt, vn_dtype)
    nc = T // dn_chunk
    bo, s_new = _deltanet(zqkv, sz, bg, conv_init, s0, prm["conv_w"], prm["dn_norm"], dn_nprob, dn_nseq, dn_chunk, nc)
    L = min(GMLP_CHUNK, T)
    reps = GMLP_CHUNK // L
    wm = prm["w_s_tril"][:, :L, :L]
    if reps > 1:
        blk = (jnp.arange(GMLP_CHUNK)[:, None] // L) == (jnp.arange(GMLP_CHUNK)[None, :] // L)
        wm = jnp.where(blk[None], jnp.tile(wm, (1, reps, reps)), 0.0)
    ws_eff = wm.astype(BF16)
    bs_rows = jnp.tile(jnp.transpose(prm["b_s"][:, :L]), (reps, 1))
    bs_full = jnp.repeat(bs_rows, A_GDIM, axis=1)
    x1, h2, route, counts = _outproj(u, vn, ga, gb, bo, x, mod, ws_eff, bs_full, prm["w_pa"], prm["w_pb"],
                                     prm["w_o"], prm["norm2"], prm["w_r"], prm["b_r"], bb, tt)
    tail = min(T, CONV_W - 1)
    zqkv_tail = zqkv.reshape(B, T, DN_CONV_CH)[:, T - tail:, :]
    conv_new = jnp.concatenate([conv_init[:, SUBLANES - (CONV_W - 1):, :], zqkv_tail], axis=1)[:, -(CONV_W - 1):, :]
    path = dict(h2=h2, route=route, counts=counts, x1=x1, mod=mod, bb=bb, tt=tt)
    return path, conv_new, s_new, vn.reshape(B, T, A_WIDTH)


def kernel(x_prompt, x_sample, c_prompt, c_sample, state_conv, state_delta, w_ada, b_ada, norm1, norm2, w_in, conv_w, a_log, dt_bias, dn_norm, ln_v_w, ln_v_b, w_s, b_s, w_pa, w_pb, w_o, w_rg, b_rg, w_re, b_re, w_e_gate, w_e_up, w_e_down, final_norm):
    depth = w_ada.shape[0]
    assert depth == 1
    bp, tp, _ = x_prompt.shape
    bs_, ts, _ = x_sample.shape
    l = 0
    wi = w_in[l]
    o_qkv = 2 * A_WIDTH
    o_z = o_qkv + DN_CONV_CH
    o_b = o_z + DN_V
    o_ga = o_b + 2 * DN_HEADS
    w_a = wi[:, :o_b].astype(BF16)
    w_b = wi[:, o_ga:].astype(BF16)
    w_bg = jnp.pad(wi[:, o_b:o_ga], ((0, 0), (0, LANES - 2 * DN_HEADS))).astype(BF16)
    adt = jnp.stack([jnp.pad(a_log[l], (DN_HEADS, LANES - 2 * DN_HEADS)),
                     jnp.pad(dt_bias[l], (DN_HEADS, LANES - 2 * DN_HEADS))])
    w_r = jnp.pad(jnp.concatenate([w_rg[l], w_re[l]], axis=1), ((0, 0), (0, LANES - N_GROUPS - N_EXPERTS)))
    w_r_hi = w_r.astype(BF16)
    w_r = jnp.concatenate([w_r_hi, (w_r - w_r_hi.astype(F32)).astype(BF16)], axis=1)
    b_r = jnp.pad(jnp.concatenate([b_rg[l], b_re[l]]), (0, LANES - N_GROUPS - N_EXPERTS)).reshape(1, LANES)
    tri = jnp.tril(jnp.ones((GMLP_CHUNK, GMLP_CHUNK), F32))
    prm = dict(
        norm1=norm1[l], norm2=norm2[l], w_a=w_a, w_b=w_b, w_bg=w_bg, adt=adt, ln_v_w=ln_v_w[l], ln_v_b=ln_v_b[l],
        conv_w=conv_w[l], dn_norm=dn_norm[l], w_s_tril=w_s[l] * tri, b_s=b_s[l],
        w_pa=w_pa[l].astype(BF16), w_pb=w_pb[l].astype(BF16), w_o=w_o[l].astype(BF16), w_r=w_r, b_r=b_r,
        w_e_gate=w_e_gate[l].reshape(N_EXPERTS, D_MODEL, D_EXPERT),
        w_e_up=w_e_up[l].reshape(N_EXPERTS, D_MODEL, D_EXPERT),
        w_e_down=w_e_down[l].reshape(N_EXPERTS, D_EXPERT, D_MODEL),
    )

    c_all = jnp.concatenate([c_prompt, c_sample], axis=0)
    mod = _ada(c_all, w_ada[l], b_ada[l]).reshape(bp + bs_, N_MOD, D_MODEL)
    mod_p, mod_s = mod[:bp], mod[bp:]

    pad_rows = SUBLANES - (CONV_W - 1)
    cinit_p = jnp.zeros((bp, SUBLANES, DN_CONV_CH), F32)
    cinit_s = jnp.pad(state_conv[l], ((0, 0), (pad_rows, 0), (0, 0)))
    s0_p = jnp.zeros((bp, DN_HEADS, DN_KDIM, DN_VDIM), F32)

    path_p, conv_p, delta_p, _ = _mixers(x_prompt, mod_p, cinit_p, s0_p, prm,
                                         bb=1, tt=min(ROW_TILE, tp), dn_nprob=4 if bp % 4 == 0 else 1, dn_nseq=1,
                                         dn_chunk=min(DN_CHUNK, tp), vn_dtype=BF16)
    path_s, conv_s, delta_s, vn_s = _mixers(x_sample, mod_s, cinit_s, state_delta[l], prm,
                                            bb=min(bs_, ROW_TILE // ts), tt=ts,
                                            dn_nprob=2 if bs_ % (2 * DN_CHUNK // ts) == 0 else 1,
                                            dn_nseq=min(bs_, DN_CHUNK // ts), dn_chunk=ts, vn_dtype=F32)
    yp, ys = _moe([path_p, path_s], prm["w_e_gate"], prm["w_e_up"], prm["w_e_down"], final_norm)
    return (yp, ys, conv_p[None], conv_s[None], delta_p[None], delta_s[None], vn_s[None])
```

```python
import functools

import jax
import jax.numpy as jnp
from jax import lax
from jax.experimental import pallas as pl
from jax.experimental.pallas import tpu as pltpu

F32 = jnp.float32
BF16 = jnp.bfloat16

D_MODEL = 1024
A_WIDTH = 1024
A_GROUPS = 4
A_GDIM = A_WIDTH // A_GROUPS
GMLP_CHUNK = 128
DN_HEADS = 8
DN_KDIM = 128
DN_VDIM = 128
DN_QK = DN_HEADS * DN_KDIM
DN_V = DN_HEADS * DN_VDIM
DN_CONV_CH = 2 * DN_QK + DN_V
CONV_W = 4
DN_CHUNK = 64
N_GROUPS = 4
EXP_PER_GROUP = 8
N_EXPERTS = N_GROUPS * EXP_PER_GROUP
D_EXPERT = 256
N_MOD = 6
EPS = 1e-6

LANES = 128
SUBLANES = 8
ROUTER_LANE0 = N_GROUPS
VMEM_LIMIT = 56 * 1024 * 1024
ROW_TILE = 512
OUT_SPLIT = 2
EXPERT_TILE = 512
EXPERTS_PER_STEP = 1


def _sigmoid(x):
    return 1.0 / (1.0 + jnp.exp(-x))


def _gelu(x):
    return 0.5 * x * (1.0 + lax.erf(x * 0.7071067811865476))


def _softplus(x):
    return jnp.maximum(x, 0.0) + jnp.log(1.0 + jnp.exp(-jnp.abs(x)))


def _dot(a, b):
    return jnp.dot(a.astype(BF16), b.astype(BF16), preferred_element_type=F32)


def _dot_nt(a, b):
    return lax.dot_general(a.astype(BF16), b.astype(BF16), (((1,), (1,)), ((), ())), preferred_element_type=F32)


def _dot_tn(a, b):
    return lax.dot_general(a.astype(BF16), b.astype(BF16), (((0,), (0,)), ((), ())), preferred_element_type=F32)


def _split3(x):
    hi = x.astype(BF16)
    r1 = x - hi.astype(F32)
    mid = r1.astype(BF16)
    lo = (r1 - mid.astype(F32)).astype(BF16)
    return hi, mid, lo


def _dot_exact_lhs(a_bf16, x):
    hi, mid, lo = _split3(x)
    f = lambda p: jnp.dot(a_bf16, p, preferred_element_type=F32)
    return f(hi) + f(mid) + f(lo)


_PASSES = {"invp": 3, "invn": 3, "inv_hi": 1, "inv_kmax": 8, "sol": 1, "kk": 1, "ws": 1, "attn": 1, "state": 1}


def _hilo(x):
    hi = x.astype(BF16)
    lo = (x - hi.astype(F32)).astype(BF16)
    return hi, lo


def _mm(site, a, b, dims=(((1,), (0,)), ((), ()))):
    f = lambda p, q: lax.dot_general(p, q, dims, preferred_element_type=F32)
    if _PASSES[site] == 1:
        return f(a.astype(BF16), b.astype(BF16))
    ah, al = _hilo(a)
    bh, bl = _hilo(b)
    return f(ah, bh) + (f(ah, bl) + f(al, bh))


_NT = (((1,), (1,)), ((), ()))
_TN = (((0,), (0,)), ((), ()))

def _params(sem):
    return pltpu.CompilerParams(dimension_semantics=sem, vmem_limit_bytes=VMEM_LIMIT)


def _ada_kernel(c_ref, w_ref, b_ref, o_ref):
    c = c_ref[...]
    o_ref[...] = _dot(c * _sigmoid(c), w_ref[...]) + b_ref[...]


def _ada(c_all, w_ada, b_ada):
    n = c_all.shape[0]
    width = w_ada.shape[1]
    bn = 512
    return pl.pallas_call(
        _ada_kernel,
        grid=(width // bn,),
        in_specs=[pl.BlockSpec((n, D_MODEL), lambda j: (0, 0)),
                  pl.BlockSpec((D_MODEL, bn), lambda j: (0, j)),
                  pl.BlockSpec((1, bn), lambda j: (0, j))],
        out_specs=pl.BlockSpec((n, bn), lambda j: (0, j)),
        out_shape=jax.ShapeDtypeStruct((n, width), F32),
        compiler_params=_params(("arbitrary",)),
        name="ada_mod",
    )(c_all, w_ada, b_ada.reshape(1, width))


IN_BLOCK = 1024
N_ALIGNED = 6


def _cast_kernel(w_ref, o_ref):
    o_ref[...] = w_ref[0].astype(o_ref.dtype)


def _realign_kernel(a_ref, b_ref, o_ref, bg_ref):
    s = 2 * DN_HEADS
    a = a_ref[0]
    o_ref[...] = jnp.concatenate([a[:, s:], b_ref[0, :, :s]], axis=1).astype(o_ref.dtype)

    @pl.when(pl.program_id(0) == 0)
    def _():
        lane = lax.broadcasted_iota(jnp.int32, (D_MODEL, LANES), 1)
        bg_ref[...] = jnp.where(lane < s, a[:, :LANES], 0.0).astype(bg_ref.dtype)


def _prep_in_weights(w_in):
    w_a = pl.pallas_call(
        _cast_kernel,
        grid=(N_ALIGNED,),
        in_specs=[pl.BlockSpec((1, D_MODEL, IN_BLOCK), lambda j: (0, 0, j))],
        out_specs=pl.BlockSpec((D_MODEL, IN_BLOCK), lambda j: (0, j)),
        out_shape=jax.ShapeDtypeStruct((D_MODEL, N_ALIGNED * IN_BLOCK), BF16),
        compiler_params=_params(("arbitrary",)),
        name="prep_w_in",
    )(w_in)
    w_b, w_bg = pl.pallas_call(
        _realign_kernel,
        grid=(2,),
        in_specs=[pl.BlockSpec((1, D_MODEL, IN_BLOCK), lambda j: (0, 0, N_ALIGNED + j)),
                  pl.BlockSpec((1, D_MODEL, IN_BLOCK), lambda j: (0, 0, N_ALIGNED + j + 1))],
        out_specs=[pl.BlockSpec((D_MODEL, IN_BLOCK), lambda j: (0, j)),
                   pl.BlockSpec((D_MODEL, LANES), lambda j: (0, 0))],
        out_shape=[jax.ShapeDtypeStruct((D_MODEL, 2 * IN_BLOCK), BF16),
                   jax.ShapeDtypeStruct((D_MODEL, LANES), BF16)],
        compiler_params=_params(("arbitrary",)),
        name="prep_w_gates",
    )(w_in, w_in)
    return w_a, w_b, w_bg


def _inproj_kernel(x_ref, mod_ref, n1_ref, wa_ref, wb_ref, wbg_ref, lnw_ref, lnb_ref, adt_ref,
                   u_ref, vn_ref, qkv_ref, sz_ref, ga_ref, gb_ref, bg_ref):
    tm = u_ref.shape[0]
    x = x_ref[...]
    y = x * lax.rsqrt(jnp.mean(x * x, axis=-1, keepdims=True) + EPS) * n1_ref[...]
    mod = mod_ref[...]
    h = y * (1.0 + mod[:, 1:2, :]) + mod[:, 0:1, :]
    hb = h.reshape(tm, D_MODEL).astype(BF16)
    zbg = jnp.dot(hb, wbg_ref[...], preferred_element_type=F32)
    lane = lax.broadcasted_iota(jnp.int32, zbg.shape, 1)
    adt = adt_ref[...]
    g = -jnp.exp(adt[0:1, :]) * _softplus(zbg + adt[1:2, :])
    bg_ref[...] = jnp.where(lane < DN_HEADS, _sigmoid(zbg), g)

    n_a = wa_ref.shape[1] // 1024
    blk = lambda j: jnp.dot(hb, wa_ref[:, j * 1024:(j + 1) * 1024] if j < n_a else
                            wb_ref[:, (j - n_a) * 1024:(j - n_a + 1) * 1024], preferred_element_type=F32)
    u_ref[...] = _gelu(blk(0)).astype(u_ref.dtype)
    a = _gelu(blk(1))
    ac = a - jnp.mean(a, axis=-1, keepdims=True)
    var = jnp.mean(ac * ac, axis=-1, keepdims=True)
    vn_ref[...] = (ac * lax.rsqrt(var + EPS) * lnw_ref[...] + lnb_ref[...]).astype(vn_ref.dtype)
    for j in range(3):
        qkv_ref[:, j * 1024:(j + 1) * 1024] = blk(2 + j)
    z = blk(5)
    sz_ref[...] = (z * _sigmoid(z)).astype(sz_ref.dtype)
    ga_ref[...] = _sigmoid(blk(6)).astype(ga_ref.dtype)
    gb_ref[...] = _sigmoid(blk(7)).astype(gb_ref.dtype)


def _inproj(x, mod, norm1, w_a, w_b, w_bg, ln_w, ln_b, adt, bb, tt, vn_dtype):
    B, T, _ = x.shape
    tm = bb * tt
    tpb = T // tt
    n_tiles = (B // bb) * tpb
    M = B * T
    const = lambda i: (0, 0)
    rows = lambda i: (i, 0)
    resident = dict(pipeline_mode=pl.Buffered(1))
    in_specs = [pl.BlockSpec((bb, tt, D_MODEL), lambda i: (i // tpb, i % tpb, 0)),
                pl.BlockSpec((bb, N_MOD, D_MODEL), lambda i: (i // tpb, 0, 0)),
                pl.BlockSpec((1, D_MODEL), const),
                pl.BlockSpec(w_a.shape, const, **resident),
                pl.BlockSpec(w_b.shape, const, **resident),
                pl.BlockSpec((D_MODEL, LANES), const, **resident),
                pl.BlockSpec((1, A_WIDTH), const),
                pl.BlockSpec((1, A_WIDTH), const),
                pl.BlockSpec((2, LANES), const)]
    args = [x, mod, norm1.reshape(1, D_MODEL), w_a, w_b, w_bg, ln_w.reshape(1, A_WIDTH), ln_b.reshape(1, A_WIDTH),
            adt]
    out_specs = [pl.BlockSpec((tm, A_WIDTH), rows),
                 pl.BlockSpec((tm, A_WIDTH), rows),
                 pl.BlockSpec((tm, DN_CONV_CH), rows),
                 pl.BlockSpec((tm, DN_V), rows),
                 pl.BlockSpec((tm, D_MODEL), rows),
                 pl.BlockSpec((tm, D_MODEL), rows),
                 pl.BlockSpec((tm, LANES), rows)]
    out_shape = [jax.ShapeDtypeStruct((M, A_WIDTH), BF16),
                 jax.ShapeDtypeStruct((M, A_WIDTH), vn_dtype),
                 jax.ShapeDtypeStruct((M, DN_CONV_CH), F32),
                 jax.ShapeDtypeStruct((M, DN_V), BF16),
                 jax.ShapeDtypeStruct((M, D_MODEL), BF16),
                 jax.ShapeDtypeStruct((M, D_MODEL), BF16),
                 jax.ShapeDtypeStruct((M, LANES), F32)]
    return pl.pallas_call(
        _inproj_kernel,
        grid=(n_tiles,),
        in_specs=in_specs,
        out_specs=out_specs,
        out_shape=out_shape,
        compiler_params=_params(("arbitrary",)),
        name="in_proj",
    )(*args)


def _inv_unit_lower_minus_eye(lmats, nilpotent):
    ns = [-l for l in lmats]
    ps = list(lmats)
    k = 2
    while k < nilpotent:
        early = k <= _PASSES["inv_kmax"]
        ps = [_mm("invp" if early else "inv_hi", p, p) for p in ps]
        ns = [n + p + _mm("invn" if early else "inv_hi", n, p) for n, p in zip(ns, ps)]
        k *= 2
    return ns


def _head_sumsq(x):
    pair = 2 * DN_KDIM
    r = lax.broadcasted_iota(jnp.int32, (pair, pair), 0) >= DN_KDIM
    c = lax.broadcasted_iota(jnp.int32, (pair, pair), 1) >= DN_KDIM
    ones2 = jnp.where(r == c, 1.0, 0.0).astype(BF16)
    sq = (x * x).astype(BF16)
    return jnp.concatenate([jnp.dot(sq[:, p * pair:(p + 1) * pair], ones2, preferred_element_type=F32)
                            for p in range(x.shape[1] // pair)], axis=1)


def _deltanet_kernel(q_ref, k_ref, v_ref, z_ref, bg_ref, cinit_ref, s0_ref, cw_ref, dnn_ref,
                     bo_ref, sn_ref, xp_s, s_s, *, nprob, nseq, C):
    R = nseq * C
    c = pl.program_id(1)
    nc = pl.num_programs(1)

    @pl.when(c == 0)
    def _():
        xp_s[:, SUBLANES - (CONV_W - 1):SUBLANES, :] = cinit_ref[...]
        s_s[...] = s0_ref[...]

    @pl.when(c > 0)
    def _():
        xp_s[:, 0:SUBLANES, :] = xp_s[:, C:C + SUBLANES, :]

    for p in range(nprob):
        for s in range(nseq):
            i = p * nseq + s
            xp_s[i, SUBLANES:SUBLANES + C, 0:DN_QK] = q_ref[p, s * C:(s + 1) * C, :]
            xp_s[i, SUBLANES:SUBLANES + C, DN_QK:2 * DN_QK] = k_ref[p, s * C:(s + 1) * C, :]
            xp_s[i, SUBLANES:SUBLANES + C, 2 * DN_QK:DN_CONV_CH] = v_ref[p, s * C:(s + 1) * C, :]

    cw = cw_ref[...]
    base = SUBLANES - (CONV_W - 1)
    acc = None
    for j in range(CONV_W):
        term = xp_s[:, base + j:base + j + C, :] * cw[j:j + 1, :]
        acc = term if acc is None else acc + term
    qkv = acc.reshape(nprob * R, DN_CONV_CH)
    qkv = qkv * _sigmoid(qkv)
    q_all = qkv[:, 0:DN_QK]
    k_all = qkv[:, DN_QK:2 * DN_QK]
    v_all = qkv[:, 2 * DN_QK:DN_CONV_CH]
    qn_all = q_all * lax.rsqrt(_head_sumsq(q_all) + EPS) * (DN_KDIM ** -0.5)
    kn_all = k_all * lax.rsqrt(_head_sumsq(k_all) + EPS)

    row = lax.broadcasted_iota(jnp.int32, (R, R), 0)
    col = lax.broadcasted_iota(jnp.int32, (R, R), 1)
    if nseq > 1:
        shift = C.bit_length() - 1
        same = lax.shift_right_logical(row, shift) == lax.shift_right_logical(col, shift)
        tril = same & (col <= row)
        strict = same & (col < row)
    else:
        tril = col <= row
        strict = col < row
    tril_b = jnp.where(tril, 1.0, 0.0).astype(BF16)
    bg = [bg_ref[p] for p in range(nprob)]
    gc = [_dot_exact_lhs(tril_b, bg[p]) for p in range(nprob)]
    gct = [g.T for g in gc]
    dnn = dnn_ref[...]

    IT = [(p, h) for p in range(nprob) for h in range(DN_HEADS)]
    NI = range(len(IT))
    hsl = [slice(h * DN_KDIM, (h + 1) * DN_KDIM) for h in range(DN_HEADS)]
    rows = [slice(p * R, (p + 1) * R) for p in range(nprob)]
    qn = [qn_all[rows[p], hsl[h]] for p, h in IT]
    kn = [kn_all[rows[p], hsl[h]] for p, h in IT]
    vh = [v_all[rows[p], hsl[h]] for p, h in IT]
    beta = [bg[p][:, h:h + 1] for p, h in IT]
    gcol = [gc[p][:, DN_HEADS + h:DN_HEADS + h + 1] for p, h in IT]
    grow = [gct[p][DN_HEADS + h:DN_HEADS + h + 1, :] for p, h in IT]
    decay = [jnp.exp(jnp.minimum(gcol[i] - grow[i], 0.0)) for i in NI]
    eg = [jnp.exp(gcol[i]) for i in NI]
    kb = [kn[i] * beta[i] for i in NI]
    vb = [vh[i] * beta[i] for i in NI]
    a2 = [_mm("kk", jnp.concatenate([kb[i], qn[i]], axis=0), kn[i], _NT) for i in NI]
    lmat = [jnp.where(strict, a2[i][:R] * decay[i], 0.0) for i in NI]
    attn = [jnp.where(tril, a2[i][R:] * decay[i], 0.0) for i in NI]
    tn = _inv_unit_lower_minus_eye(lmat, C)
    rhs = [jnp.concatenate([vb[i], kb[i] * eg[i]], axis=1) for i in NI]
    sol = [rhs[i] + _mm("sol", tn[i], rhs[i]) for i in NI]
    u = [x[:, :DN_VDIM] for x in sol]
    w = [x[:, DN_VDIM:] for x in sol]
    qg = [qn[i] * eg[i] for i in NI]
    vnew = [[None] * nseq for _ in NI]
    qs = [[None] * nseq for _ in NI]
    for s in range(nseq):
        rs = slice(s * C, (s + 1) * C)
        st = [s_s[p * nseq + s, h] for p, h in IT]
        ws = [_mm("ws", jnp.concatenate([w[i][rs], qg[i][rs]], axis=0), st[i]) for i in NI]
        for i in NI:
            vnew[i][s] = u[i][rs] - ws[i][:C]
            qs[i][s] = ws[i][C:]
        glast = [gcol[i][(s + 1) * C - 1:(s + 1) * C, :] for i in NI]
        kd = [kn[i][rs] * jnp.exp(glast[i] - gcol[i][rs]) for i in NI]
        upd = [_mm("state", kd[i], vnew[i][s], _TN) for i in NI]
        for i, (p, h) in enumerate(IT):
            s_s[p * nseq + s, h] = st[i] * jnp.exp(glast[i]) + upd[i]
    cat = lambda parts: parts[0] if nseq == 1 else jnp.concatenate(parts, axis=0)
    o = [cat(qs[i]) + _mm("attn", attn[i], cat(vnew[i])) for i in NI]
    o_all = jnp.concatenate([jnp.concatenate(o[p * DN_HEADS:(p + 1) * DN_HEADS], axis=1) for p in range(nprob)],
                            axis=0)
    dnn_all = jnp.concatenate([dnn] * DN_HEADS, axis=1)
    on = o_all * lax.rsqrt(_head_sumsq(o_all) * (1.0 / DN_VDIM) + EPS) * dnn_all
    for p in range(nprob):
        bo_ref[p] = (on[rows[p]] * z_ref[p].astype(F32)).astype(bo_ref.dtype)

    @pl.when(c == nc - 1)
    def _():
        sn_ref[...] = s_s[...]


def _deltanet(zqkv, sz, bg, conv_init, s0, conv_w, dn_norm, nprob, nseq, C, nc):
    M = zqkv.shape[0]
    R = nseq * C
    G = M // (R * nc)
    view = lambda a: a.reshape(G, R * nc, a.shape[-1])
    rmap = lambda col: (lambda g, c: (g, c, col))
    nsq = nprob * nseq
    state_spec = pl.BlockSpec((nsq, DN_HEADS, DN_KDIM, DN_VDIM), lambda g, c: (g, 0, 0, 0))
    bo, s_new = pl.pallas_call(
        functools.partial(_deltanet_kernel, nprob=nprob, nseq=nseq, C=C),
        grid=(G // nprob, nc),
        in_specs=[pl.BlockSpec((nprob, R, DN_QK), rmap(0)),
                  pl.BlockSpec((nprob, R, DN_QK), rmap(1)),
                  pl.BlockSpec((nprob, R, DN_V), rmap(2)),
                  pl.BlockSpec((nprob, R, DN_V), rmap(0)),
                  pl.BlockSpec((nprob, R, LANES), rmap(0)),
                  pl.BlockSpec((nsq, CONV_W - 1, DN_CONV_CH), lambda g, c: (g, 0, 0)),
                  state_spec,
                  pl.BlockSpec((CONV_W, DN_CONV_CH), lambda g, c: (0, 0)),
                  pl.BlockSpec((1, DN_VDIM), lambda g, c: (0, 0))],
        out_specs=[pl.BlockSpec((nprob, R, DN_V), rmap(0)), state_spec],
        out_shape=[jax.ShapeDtypeStruct((G, R * nc, DN_V), BF16),
                   jax.ShapeDtypeStruct(s0.shape, F32)],
        scratch_shapes=[pltpu.VMEM((nsq, C + SUBLANES, DN_CONV_CH), F32),
                        pltpu.VMEM((nsq, DN_HEADS, DN_KDIM, DN_VDIM), F32)],
        compiler_params=_params(("arbitrary", "arbitrary")),
        name="deltanet",
    )(view(zqkv), view(zqkv), view(zqkv), view(sz), view(bg), conv_init, s0, conv_w, dn_norm.reshape(1, DN_VDIM))
    return bo.reshape(M, DN_V), s_new


def _outproj_kernel(u_ref, vn_ref, ga_ref, gb_ref, bo_ref, x_ref, mod_ref, ws_ref, bs_ref, wpa_ref, wpb_ref,
                    wo_ref, n2_ref, wr_ref, br_ref, before_ref, x1_ref, h2_ref, rt_ref, cnt_ref, cnt_s):
    @pl.when(pl.program_id(0) == 0)
    def _():
        cnt_s[...] = jnp.zeros_like(cnt_s)

    tm = u_ref.shape[0]
    hm = tm // OUT_SPLIT
    bb, tt, _ = x_ref.shape
    HR = range(OUT_SPLIT)
    rows = [slice(h * hm, (h + 1) * hm) for h in HR]
    if bb == 1:
        xsl = [(slice(None), slice(h * (tt // OUT_SPLIT), (h + 1) * (tt // OUT_SPLIT))) for h in HR]
        msl = [slice(None)] * OUT_SPLIT
    else:
        xsl = [(slice(h * (bb // OUT_SPLIT), (h + 1) * (bb // OUT_SPLIT)), slice(None)) for h in HR]
        msl = [s[0] for s in xsl]

    def spatial(h):
        parts = []
        for r in range(hm // GMLP_CHUNK):
            rs = slice(h * hm + r * GMLP_CHUNK, h * hm + (r + 1) * GMLP_CHUNK)
            vn = vn_ref[rs, :].astype(BF16)
            cols = [jnp.dot(ws_ref[g], vn[:, g * A_GDIM:(g + 1) * A_GDIM], preferred_element_type=F32)
                    for g in range(A_GROUPS)]
            parts.append(jnp.concatenate(cols, axis=1) + bs_ref[...])
        return parts[0] if len(parts) == 1 else jnp.concatenate(parts, axis=0)

    sv = [spatial(h) for h in HR]
    a_out = [u_ref[rows[h], :].astype(F32) * sv[h] for h in HR]
    pa = [_dot(a_out[h], wpa_ref[...]) for h in HR]
    pb = [_dot(bo_ref[rows[h], :], wpb_ref[...]) for h in HR]
    m = [ga_ref[rows[h], :].astype(F32) * pa[h] + gb_ref[rows[h], :].astype(F32) * pb[h] for h in HR]
    mix = [_dot(m[h], wo_ref[...]) for h in HR]
    mod = [mod_ref[msl[h]] for h in HR]
    x = [x_ref[xsl[h][0], xsl[h][1], :] for h in HR]
    x1 = [x[h] + mod[h][:, 2:3, :] * mix[h].reshape(x[h].shape) for h in HR]
    for h in HR:
        x1_ref[xsl[h][0], xsl[h][1], :] = x1[h]
    y = [x1[h] * lax.rsqrt(jnp.mean(x1[h] * x1[h], axis=-1, keepdims=True) + EPS) * n2_ref[...] for h in HR]
    h2 = [(y[h] * (1.0 + mod[h][:, 4:5, :]) + mod[h][:, 3:4, :]).reshape(hm, D_MODEL) for h in HR]
    for h in HR:
        for j in range(SUBLANES):
            h2_ref[pl.ds(h * hm * SUBLANES + j, hm, stride=SUBLANES), :] = h2[h][:, j * LANES:(j + 1) * LANES]

    wr = wr_ref[...]
    lane = lax.broadcasted_iota(jnp.int32, (hm, LANES), 1)
    lanef = lane.astype(F32)
    neg = jnp.float32(-jnp.inf)
    big = jnp.float32(1e9)

    def route(h2h):
        hh, hl = _hilo(h2h)
        p = jnp.dot(hh, wr, preferred_element_type=F32)
        logits = (p[:, :LANES] + (p[:, LANES:] + jnp.dot(hl, wr[:, :LANES], preferred_element_type=F32))
                  + br_ref[...])
        gl = jnp.where(lane < N_GROUPS, logits, neg)
        gmax = jnp.max(gl, axis=-1, keepdims=True)
        gsel = jnp.min(jnp.where(gl == gmax, lanef, big), axis=-1, keepdims=True)
        gp = 1.0 / jnp.sum(jnp.exp(gl - gmax), axis=-1, keepdims=True)
        lo = ROUTER_LANE0 + EXP_PER_GROUP * gsel
        in_grp = (lanef >= lo) & (lanef < lo + EXP_PER_GROUP)
        el = jnp.where(in_grp, logits, neg)
        m1 = jnp.max(el, axis=-1, keepdims=True)
        i1 = jnp.min(jnp.where(el == m1, lanef, big), axis=-1, keepdims=True)
        el2 = jnp.where(lanef == i1, neg, el)
        m2 = jnp.max(el2, axis=-1, keepdims=True)
        i2 = jnp.min(jnp.where(el2 == m2, lanef, big), axis=-1, keepdims=True)
        ex = jnp.exp(m2 - m1)
        return i1 - ROUTER_LANE0, i2 - ROUTER_LANE0, gp / (1.0 + ex), gp * ex / (1.0 + ex)

    routed = [route(h2[h]) for h in HR]

    seen0 = cnt_s[...]
    for h in HR:
        e1, e2, w1, w2 = routed[h]
        oh1 = jnp.where(lanef == e1, 1.0, 0.0)
        oh2 = jnp.where(lanef == e2, 1.0, 0.0)
        oh = oh1 + oh2
        seen = jnp.dot(before_ref[...], oh.astype(BF16), preferred_element_type=F32) + seen0
        r1 = jnp.sum(oh1 * seen, axis=-1, keepdims=True)
        r2 = jnp.sum(oh2 * seen, axis=-1, keepdims=True)
        seen0 = seen0 + jnp.sum(oh, axis=0, keepdims=True)
        rec = jnp.zeros((hm, LANES), F32)
        for k, val in enumerate((e1, e2, r1, r2, w1, w2)):
            rec = jnp.where(lane == k, val, rec)
        rt_ref[rows[h], :] = rec
    cnt_s[...] = seen0
    cnt_ref[...] = seen0


def _outproj(u, vn, ga, gb, bo, x, mod, ws_eff, bs_full, w_pa, w_pb, w_o, norm2, w_r, b_r, bb, tt):
    B, T, _ = x.shape
    tm = bb * tt
    tpb = T // tt
    n_tiles = (B // bb) * tpb
    M = B * T
    cmap = lambda col: (lambda i: (i, col))
    full2 = lambda i: (0, 0)
    return pl.pallas_call(
        _outproj_kernel,
        grid=(n_tiles,),
        in_specs=[pl.BlockSpec((tm, A_WIDTH), cmap(0)),
                  pl.BlockSpec((tm, A_WIDTH), cmap(0)),
                  pl.BlockSpec((tm, D_MODEL), cmap(0)),
                  pl.BlockSpec((tm, D_MODEL), cmap(0)),
                  pl.BlockSpec((tm, DN_V), cmap(0)),
                  pl.BlockSpec((bb, tt, D_MODEL), lambda i: (i // tpb, i % tpb, 0)),
                  pl.BlockSpec((bb, N_MOD, D_MODEL), lambda i: (i // tpb, 0, 0)),
                  pl.BlockSpec((A_GROUPS, GMLP_CHUNK, GMLP_CHUNK), lambda i: (0, 0, 0)),
                  pl.BlockSpec((GMLP_CHUNK, A_WIDTH), full2),
                  pl.BlockSpec((A_WIDTH, D_MODEL), full2),
                  pl.BlockSpec((DN_V, D_MODEL), full2),
                  pl.BlockSpec((D_MODEL, D_MODEL), full2),
                  pl.BlockSpec((1, D_MODEL), full2),
                  pl.BlockSpec((D_MODEL, 2 * LANES), full2),
                  pl.BlockSpec((1, LANES), full2),
                  pl.BlockSpec((tm // OUT_SPLIT, tm // OUT_SPLIT), full2)],
        out_specs=[pl.BlockSpec((bb, tt, D_MODEL), lambda i: (i // tpb, i % tpb, 0)),
                   pl.BlockSpec((tm * SUBLANES, LANES), cmap(0)),
                   pl.BlockSpec((tm, LANES), cmap(0)),
                   pl.BlockSpec((1, LANES), full2)],
        out_shape=[jax.ShapeDtypeStruct((B, T, D_MODEL), F32),
                   jax.ShapeDtypeStruct((M * SUBLANES, LANES), F32),
                   jax.ShapeDtypeStruct((M, LANES), F32),
                   jax.ShapeDtypeStruct((1, LANES), F32)],
        scratch_shapes=[pltpu.VMEM((1, LANES), F32)],
        compiler_params=_params(("arbitrary",)),
        name="out_proj",
    )(u, vn, ga, gb, bo, x, mod, ws_eff, bs_full, w_pa, w_pb, w_o, norm2.reshape(1, D_MODEL), w_r, b_r,
      jnp.tril(jnp.ones((tm // OUT_SPLIT, tm // OUT_SPLIT), BF16), -1))


def _tile_copy(src_ref, src_off, dst_ref, dst_off, sem):
    return pltpu.make_async_copy(src_ref.at[pl.ds(pl.multiple_of(src_off, SUBLANES), SUBLANES), :],
                                 dst_ref.at[pl.ds(pl.multiple_of(dst_off, SUBLANES), SUBLANES), :], sem)


def _to_matrix(ref, n):
    return jnp.concatenate([ref[pl.ds(j, n, stride=SUBLANES), :] for j in range(SUBLANES)], axis=1)


def _zero_fill(pad_lo_ref, pad_n_ref, na_ref, xs_ref, zero_s, zsem):
    blk = EXPERT_TILE * SUBLANES
    n_tiles = xs_ref.shape[0] // blk
    zero_s[...] = jnp.zeros_like(zero_s)
    run = lambda off, rows: pltpu.make_async_copy(
        zero_s.at[pl.ds(0, rows * SUBLANES), :],
        xs_ref.at[pl.ds(pl.multiple_of(off * SUBLANES, SUBLANES), rows * SUBLANES), :], zsem)
    sizes = [1 << b for b in reversed(range(EXPERT_TILE.bit_length() - 1))]
    for wait in (False, True):
        for e in range(N_EXPERTS):
            off = pad_lo_ref[e]
            for rows in sizes:
                bit = pad_n_ref[e] & rows

                @pl.when(bit != 0)
                def _(off=off, rows=rows):
                    run(0, rows).wait() if wait else run(off, rows).start()

                off = off + bit

        def idle(i, carry):
            run(0, EXPERT_TILE).wait() if wait else run(i * EXPERT_TILE, EXPERT_TILE).start()
            return carry

        lax.fori_loop(na_ref[0], n_tiles, idle, 0)


def _dispatch_kernel(pad_lo_ref, pad_n_ref, na_ref, pos_ref, *refs, first_tile):
    n_paths = len(first_tile) - 1
    h_refs = refs[:n_paths]
    xs_ref, zero_s, sem, zsem = refs[n_paths:]
    i = pl.program_id(0)

    @pl.when(i == 0)
    def _():
        _zero_fill(pad_lo_ref, pad_n_ref, na_ref, xs_ref, zero_s, zsem)

    tm = h_refs[0].shape[0] // SUBLANES
    for p, h_ref in enumerate(h_refs):
        @pl.when((i >= first_tile[p]) & (i < first_tile[p + 1]))
        def _(h_ref=h_ref):
            def issue(t, carry):
                for k in range(2):
                    _tile_copy(h_ref, t * SUBLANES, xs_ref, pos_ref[0, k, t], sem).start(priority=k)
                return carry

            lax.fori_loop(0, tm, issue, 0, unroll=8)

            def drain(t, carry):
                for k in range(2):
                    _tile_copy(h_ref, 0, xs_ref, 0, sem).wait()
                return carry

            lax.fori_loop(0, tm, drain, 0, unroll=8)


def _dispatch(h2s, pos, tm, n_rows, pad_lo, pad_n, n_active):
    blk = tm * SUBLANES
    tiles = [h.shape[0] // blk for h in h2s]
    first_tile = [sum(tiles[:p]) for p in range(len(tiles) + 1)]
    hmap = lambda p: (lambda i, *_: (jnp.clip(i - first_tile[p], 0, tiles[p] - 1), 0))
    grid_spec = pltpu.PrefetchScalarGridSpec(
        num_scalar_prefetch=3,
        grid=(first_tile[-1],),
        in_specs=[pl.BlockSpec((1, 2, tm), lambda i, *_: (i, 0, 0), memory_space=pltpu.SMEM)]
                 + [pl.BlockSpec((blk, LANES), hmap(p)) for p in range(len(tiles))],
        out_specs=pl.BlockSpec(memory_space=pl.ANY),
        scratch_shapes=[pltpu.VMEM((EXPERT_TILE * SUBLANES, LANES), F32),
                        pltpu.SemaphoreType.DMA(()), pltpu.SemaphoreType.DMA(())],
    )
    return pl.pallas_call(
        functools.partial(_dispatch_kernel, first_tile=tuple(first_tile)),
        grid_spec=grid_spec,
        out_shape=jax.ShapeDtypeStruct((n_rows * SUBLANES, LANES), F32),
        compiler_params=_params(("arbitrary",)),
        name="moe_dispatch",
    )(pad_lo, pad_n, n_active, pos, *h2s)


def _expert_kernel(te_ref, na_ref, x_ref, *refs):
    del te_ref
    w_refs, o_ref = refs[:-1], refs[-1]
    blk = EXPERT_TILE * SUBLANES
    live = EXPERTS_PER_STEP * pl.program_id(0) < na_ref[0]

    @pl.when(live)
    def _():
        TR = range(EXPERTS_PER_STEP)
        wg, wu, wd = ([w_refs[3 * t + k][0] for t in TR] for k in range(3))
        x = [_to_matrix(x_ref.at[pl.ds(t * blk, blk), :], EXPERT_TILE).astype(BF16) for t in TR]
        a = [_dot(x[t], wg[t]) for t in TR]
        u = [_dot(x[t], wu[t]) for t in TR]
        o = [_dot(a[t] * _sigmoid(a[t]) * u[t], wd[t]) for t in TR]
        for t in TR:
            for j in range(SUBLANES):
                o_ref[pl.ds(t * blk + j, EXPERT_TILE, stride=SUBLANES), :] = o[t][:, j * LANES:(j + 1) * LANES]

    @pl.when(jnp.logical_not(live))
    def _():
        o_ref[...] = jnp.zeros_like(o_ref)


def _experts(xs, tile_expert, n_active, wg, wu, wd):
    blk = EXPERT_TILE * SUBLANES * EXPERTS_PER_STEP
    n_steps = xs.shape[0] // blk
    last = lambda na: jnp.maximum(na[0] - 1, 0)
    xmap = lambda i, te, na: (jnp.minimum(i, last(na) // EXPERTS_PER_STEP), 0)
    wmap = lambda t: (lambda i, te, na: (te[jnp.minimum(EXPERTS_PER_STEP * i + t, last(na))], 0, 0))
    w_specs = []
    for t in range(EXPERTS_PER_STEP):
        w_specs += [pl.BlockSpec((1, D_MODEL, D_EXPERT), wmap(t)),
                    pl.BlockSpec((1, D_MODEL, D_EXPERT), wmap(t)),
                    pl.BlockSpec((1, D_EXPERT, D_MODEL), wmap(t))]
    grid_spec = pltpu.PrefetchScalarGridSpec(
        num_scalar_prefetch=2,
        grid=(n_steps,),
        in_specs=[pl.BlockSpec((blk, LANES), xmap)] + w_specs,
        out_specs=pl.BlockSpec((blk, LANES), lambda i, te, na: (i, 0)),
    )
    return pl.pallas_call(
        _expert_kernel,
        grid_spec=grid_spec,
        out_shape=jax.ShapeDtypeStruct(xs.shape, F32),
        compiler_params=_params(("arbitrary",)),
        name="moe_experts",
    )(tile_expert, n_active, xs, *([wg, wu, wd] * EXPERTS_PER_STEP))


def _combine_kernel(pos_ref, pos_next_ref, rt_ref, x1_ref, mod_ref, fn_ref, ys_ref, y_ref, g_s, sems):
    tm = rt_ref.shape[0]
    i = pl.program_id(0)
    n = pl.num_programs(0)
    slot = i % 2

    def gather(p_ref, s):
        def issue(t, carry):
            for k in range(2):
                _tile_copy(ys_ref, p_ref[0, k, t], g_s.at[s, k], t * SUBLANES, sems.at[s]).start(priority=k)
            return carry

        lax.fori_loop(0, tm, issue, 0, unroll=8)

    @pl.when(i == 0)
    def _():
        gather(pos_ref, slot)

    @pl.when(i + 1 < n)
    def _():
        gather(pos_next_ref, 1 - slot)

    def drain(t, carry):
        for k in range(2):
            _tile_copy(ys_ref, 0, g_s.at[slot, k], 0, sems.at[slot]).wait()
        return carry

    lax.fori_loop(0, tm, drain, 0, unroll=8)

    rt = rt_ref[...]
    moe = rt[:, 4:5] * _to_matrix(g_s.at[slot, 0], tm) + rt[:, 5:6] * _to_matrix(g_s.at[slot, 1], tm)
    mod = mod_ref[...]
    x1 = x1_ref[...]
    x2 = x1 + mod[:, 5:6, :] * moe.reshape(x1.shape)
    y_ref[...] = x2 * lax.rsqrt(jnp.mean(x2 * x2, axis=-1, keepdims=True) + EPS) * fn_ref[...]


def _combine(ys, pos, route, x1, mod, final_norm, bb, tt):
    B, T, _ = x1.shape
    tm = bb * tt
    tpb = T // tt
    n_tiles = (B // bb) * tpb
    xmap = lambda i: (i // tpb, i % tpb, 0)
    return pl.pallas_call(
        _combine_kernel,
        grid=(n_tiles,),
        in_specs=[pl.BlockSpec((1, 2, tm), lambda i: (i, 0, 0), memory_space=pltpu.SMEM),
                  pl.BlockSpec((1, 2, tm), lambda i: (jnp.minimum(i + 1, n_tiles - 1), 0, 0),
                               memory_space=pltpu.SMEM),
                  pl.BlockSpec((tm, LANES), lambda i: (i, 0)),
                  pl.BlockSpec((bb, tt, D_MODEL), xmap),
                  pl.BlockSpec((bb, N_MOD, D_MODEL), lambda i: (i // tpb, 0, 0)),
                  pl.BlockSpec((1, D_MODEL), lambda i: (0, 0)),
                  pl.BlockSpec(memory_space=pl.ANY)],
        out_specs=pl.BlockSpec((bb, tt, D_MODEL), xmap),
        out_shape=jax.ShapeDtypeStruct((B, T, D_MODEL), F32),
        scratch_shapes=[pltpu.VMEM((2, 2, tm * SUBLANES, LANES), F32), pltpu.SemaphoreType.DMA((2,))],
        compiler_params=_params(("arbitrary",)),
        name="moe_combine",
    )(pos, pos, route, x1, mod, final_norm.reshape(1, D_MODEL), ys)


def _moe(paths, wg, wu, wd, final_norm):
    n_assign = sum(2 * p["route"].shape[0] for p in paths)
    max_tiles = (n_assign + N_EXPERTS * (EXPERT_TILE - 1)) // EXPERT_TILE
    max_tiles += -max_tiles % EXPERTS_PER_STEP
    cnts = [p["counts"][0, :N_EXPERTS].astype(jnp.int32) for p in paths]
    cnt = sum(cnts)
    nt = (cnt + EXPERT_TILE - 1) // EXPERT_TILE
    ends = jnp.cumsum(nt)
    base = ((ends - nt) * EXPERT_TILE).astype(jnp.int32)
    n_active = ends[-1:].astype(jnp.int32)
    tile_expert = jnp.minimum(jnp.sum(jnp.arange(max_tiles)[:, None] >= ends[None, :], axis=1),
                              N_EXPERTS - 1).astype(jnp.int32)
    tm = paths[0]["bb"] * paths[0]["tt"]
    assert all(p["bb"] * p["tt"] == tm for p in paths)
    start = base
    for p, c in zip(paths, cnts):
        eid = p["route"][:, 0:2].astype(jnp.int32)
        rank = p["route"][:, 2:4].astype(jnp.int32)
        first_row = jnp.sum(jnp.where(eid[..., None] == jnp.arange(N_EXPERTS), start, 0), axis=-1)
        pos = (first_row + rank) * SUBLANES
        p["pos"] = jnp.transpose(pos.reshape(-1, tm, 2), (0, 2, 1))
        start = start + c
    xs = _dispatch([p["h2"] for p in paths], jnp.concatenate([p["pos"] for p in paths], axis=0), tm,
                   max_tiles * EXPERT_TILE, base + cnt, nt * EXPERT_TILE - cnt, n_active)
    ys = _experts(xs, tile_expert, n_active, wg, wu, wd)
    return [_combine(ys, p["pos"], p["route"], p["x1"], p["mod"], final_norm, p["bb"], p["tt"]) for p in paths]


def _mixers(x, mod, conv_init, s0, prm, *, bb, tt, dn_nprob, dn_nseq, dn_chunk, vn_dtype):
    B, T, _ = x.shape
    u, vn, zqkv, sz, ga, gb, bg = _inproj(x, mod, prm["norm1"], prm["w_a"], prm["w_b"], prm["w_bg"], prm["ln_v_w"],
                                          prm["ln_v_b"], prm["adt"], bb, tt, vn_dtype)
    nc = T // dn_chunk
    bo, s_new = _deltanet(zqkv, sz, bg, conv_init, s0, prm["conv_w"], prm["dn_norm"], dn_nprob, dn_nseq, dn_chunk, nc)
    L = min(GMLP_CHUNK, T)
    reps = GMLP_CHUNK // L
    wm = prm["w_s_tril"][:, :L, :L]
    if reps > 1:
        blk = (jnp.arange(GMLP_CHUNK)[:, None] // L) == (jnp.arange(GMLP_CHUNK)[None, :] // L)
        wm = jnp.where(blk[None], jnp.tile(wm, (1, reps, reps)), 0.0)
    ws_eff = wm.astype(BF16)
    bs_rows = jnp.tile(jnp.transpose(prm["b_s"][:, :L]), (reps, 1))
    bs_full = jnp.repeat(bs_rows, A_GDIM, axis=1)
    x1, h2, route, counts = _outproj(u, vn, ga, gb, bo, x, mod, ws_eff, bs_full, prm["w_pa"], prm["w_pb"],
                                     prm["w_o"], prm["norm2"], prm["w_r"], prm["b_r"], bb, tt)
    tail = min(T, CONV_W - 1)
    zqkv_tail = zqkv.reshape(B, T, DN_CONV_CH)[:, T - tail:, :]
    conv_new = jnp.concatenate([conv_init, zqkv_tail], axis=1)[:, -(CONV_W - 1):, :]
    path = dict(h2=h2, route=route, counts=counts, x1=x1, mod=mod, bb=bb, tt=tt)
    return path, conv_new, s_new, vn.reshape(B, T, A_WIDTH)


def kernel(x_prompt, x_sample, c_prompt, c_sample, state_conv, state_delta, w_ada, b_ada, norm1, norm2, w_in, conv_w, a_log, dt_bias, dn_norm, ln_v_w, ln_v_b, w_s, b_s, w_pa, w_pb, w_o, w_rg, b_rg, w_re, b_re, w_e_gate, w_e_up, w_e_down, final_norm):
    depth = w_ada.shape[0]
    assert depth == 1
    bp, tp, _ = x_prompt.shape
    bs_, ts, _ = x_sample.shape
    l = 0
    assert w_in.shape[-1] == (N_ALIGNED + 2) * IN_BLOCK + 2 * DN_HEADS
    w_a, w_b, w_bg = _prep_in_weights(w_in[l:l + 1])
    adt = jnp.stack([jnp.pad(a_log[l], (DN_HEADS, LANES - 2 * DN_HEADS)),
                     jnp.pad(dt_bias[l], (DN_HEADS, LANES - 2 * DN_HEADS))])
    w_r = jnp.pad(jnp.concatenate([w_rg[l], w_re[l]], axis=1), ((0, 0), (0, LANES - N_GROUPS - N_EXPERTS)))
    w_r_hi = w_r.astype(BF16)
    w_r = jnp.concatenate([w_r_hi, (w_r - w_r_hi.astype(F32)).astype(BF16)], axis=1)
    b_r = jnp.pad(jnp.concatenate([b_rg[l], b_re[l]]), (0, LANES - N_GROUPS - N_EXPERTS)).reshape(1, LANES)
    tri = jnp.tril(jnp.ones((GMLP_CHUNK, GMLP_CHUNK), F32))
    prm = dict(
        norm1=norm1[l], norm2=norm2[l], w_a=w_a, w_b=w_b, w_bg=w_bg, adt=adt, ln_v_w=ln_v_w[l], ln_v_b=ln_v_b[l],
        conv_w=conv_w[l], dn_norm=dn_norm[l], w_s_tril=w_s[l] * tri, b_s=b_s[l],
        w_pa=w_pa[l].astype(BF16), w_pb=w_pb[l].astype(BF16), w_o=w_o[l].astype(BF16), w_r=w_r, b_r=b_r,
        w_e_gate=w_e_gate[l].reshape(N_EXPERTS, D_MODEL, D_EXPERT),
        w_e_up=w_e_up[l].reshape(N_EXPERTS, D_MODEL, D_EXPERT),
        w_e_down=w_e_down[l].reshape(N_EXPERTS, D_EXPERT, D_MODEL),
    )

    c_all = jnp.concatenate([c_prompt, c_sample], axis=0)
    mod = _ada(c_all, w_ada[l], b_ada[l]).reshape(bp + bs_, N_MOD, D_MODEL)
    mod_p, mod_s = mod[:bp], mod[bp:]

    cinit_p = jnp.zeros((bp, CONV_W - 1, DN_CONV_CH), F32)
    cinit_s = state_conv[l]
    s0_p = jnp.zeros((bp, DN_HEADS, DN_KDIM, DN_VDIM), F32)

    path_p, conv_p, delta_p, _ = _mixers(x_prompt, mod_p, cinit_p, s0_p, prm,
                                         bb=1, tt=min(ROW_TILE, tp), dn_nprob=4 if bp % 4 == 0 else 1, dn_nseq=1,
                                         dn_chunk=min(DN_CHUNK, tp), vn_dtype=BF16)
    path_s, conv_s, delta_s, vn_s = _mixers(x_sample, mod_s, cinit_s, state_delta[l], prm,
                                            bb=min(bs_, ROW_TILE // ts), tt=ts,
                                            dn_nprob=2 if bs_ % (2 * DN_CHUNK // ts) == 0 else 1,
                                            dn_nseq=min(bs_, DN_CHUNK // ts), dn_chunk=ts, vn_dtype=F32)
    yp, ys = _moe([path_p, path_s], prm["w_e_gate"], prm["w_e_up"], prm["w_e_down"], final_norm)
    return (yp, ys, conv_p[None], conv_s[None], delta_p[None], delta_s[None], vn_s[None])
```

```python
import functools

import jax
import jax.numpy as jnp
from jax import lax
from jax.experimental import pallas as pl
from jax.experimental.pallas import tpu as pltpu

F32 = jnp.float32
BF16 = jnp.bfloat16

D_MODEL = 1024
A_WIDTH = 1024
A_GROUPS = 4
A_GDIM = A_WIDTH // A_GROUPS
GMLP_CHUNK = 128
DN_HEADS = 8
DN_KDIM = 128
DN_VDIM = 128
DN_QK = DN_HEADS * DN_KDIM
DN_V = DN_HEADS * DN_VDIM
DN_CONV_CH = 2 * DN_QK + DN_V
CONV_W = 4
DN_CHUNK = 64
N_GROUPS = 4
EXP_PER_GROUP = 8
N_EXPERTS = N_GROUPS * EXP_PER_GROUP
D_EXPERT = 256
N_MOD = 6
EPS = 1e-6

LANES = 128
SUBLANES = 8
ROUTER_LANE0 = N_GROUPS
VMEM_LIMIT = 56 * 1024 * 1024
ROW_TILE = 512
OUT_SPLIT = 2
EXPERT_TILE = 512
EXPERTS_PER_STEP = 1
DISPATCH_RING = 3


def _sigmoid(x):
    return 1.0 / (1.0 + jnp.exp(-x))


def _gelu(x):
    return 0.5 * x * (1.0 + lax.erf(x * 0.7071067811865476))


def _softplus(x):
    return jnp.maximum(x, 0.0) + jnp.log(1.0 + jnp.exp(-jnp.abs(x)))


def _dot(a, b):
    return jnp.dot(a.astype(BF16), b.astype(BF16), preferred_element_type=F32)


def _split3(x):
    hi = x.astype(BF16)
    r1 = x - hi.astype(F32)
    mid = r1.astype(BF16)
    lo = (r1 - mid.astype(F32)).astype(BF16)
    return hi, mid, lo


def _dot_exact_lhs(a_bf16, x):
    hi, mid, lo = _split3(x)
    f = lambda p: jnp.dot(a_bf16, p, preferred_element_type=F32)
    return f(hi) + f(mid) + f(lo)


_PASSES = {"invp": 3, "invn": 3, "inv_hi": 1, "inv_kmax": 8, "sol": 1, "kk": 1, "ws": 1, "attn": 1, "state": 1}


def _hilo(x):
    hi = x.astype(BF16)
    lo = (x - hi.astype(F32)).astype(BF16)
    return hi, lo


def _mm(site, a, b, dims=(((1,), (0,)), ((), ()))):
    f = lambda p, q: lax.dot_general(p, q, dims, preferred_element_type=F32)
    if _PASSES[site] == 1:
        return f(a.astype(BF16), b.astype(BF16))
    ah, al = _hilo(a)
    bh, bl = _hilo(b)
    return f(ah, bh) + (f(ah, bl) + f(al, bh))


_NT = (((1,), (1,)), ((), ()))
_TN = (((0,), (0,)), ((), ()))

def _params(sem):
    return pltpu.CompilerParams(dimension_semantics=sem, vmem_limit_bytes=VMEM_LIMIT)


def _ada_kernel(c_ref, w_ref, b_ref, o_ref):
    c = c_ref[...]
    o_ref[...] = _dot(c * _sigmoid(c), w_ref[...]) + b_ref[...]


def _ada(c_all, w_ada, b_ada):
    n = c_all.shape[0]
    width = w_ada.shape[1]
    bn = 512
    return pl.pallas_call(
        _ada_kernel,
        grid=(width // bn,),
        in_specs=[pl.BlockSpec((n, D_MODEL), lambda j: (0, 0)),
                  pl.BlockSpec((D_MODEL, bn), lambda j: (0, j)),
                  pl.BlockSpec((1, bn), lambda j: (0, j))],
        out_specs=pl.BlockSpec((n, bn), lambda j: (0, j)),
        out_shape=jax.ShapeDtypeStruct((n, width), F32),
        compiler_params=_params(("arbitrary",)),
        name="ada_mod",
    )(c_all, w_ada, b_ada.reshape(1, width))


IN_BLOCK = 1024
N_ALIGNED = 6


def _cast_kernel(w_ref, o_ref):
    o_ref[...] = w_ref[0].astype(o_ref.dtype)


def _realign_kernel(a_ref, b_ref, o_ref, bg_ref):
    s = 2 * DN_HEADS
    a = a_ref[0]
    o_ref[...] = jnp.concatenate([a[:, s:], b_ref[0, :, :s]], axis=1).astype(o_ref.dtype)

    @pl.when(pl.program_id(0) == 0)
    def _():
        lane = lax.broadcasted_iota(jnp.int32, (D_MODEL, LANES), 1)
        bg_ref[...] = jnp.where(lane < s, a[:, :LANES], 0.0).astype(bg_ref.dtype)


def _prep_in_weights(w_in):
    w_a = pl.pallas_call(
        _cast_kernel,
        grid=(N_ALIGNED,),
        in_specs=[pl.BlockSpec((1, D_MODEL, IN_BLOCK), lambda j: (0, 0, j))],
        out_specs=pl.BlockSpec((D_MODEL, IN_BLOCK), lambda j: (0, j)),
        out_shape=jax.ShapeDtypeStruct((D_MODEL, N_ALIGNED * IN_BLOCK), BF16),
        compiler_params=_params(("arbitrary",)),
        name="prep_w_in",
    )(w_in)
    w_b, w_bg = pl.pallas_call(
        _realign_kernel,
        grid=(2,),
        in_specs=[pl.BlockSpec((1, D_MODEL, IN_BLOCK), lambda j: (0, 0, N_ALIGNED + j)),
                  pl.BlockSpec((1, D_MODEL, IN_BLOCK), lambda j: (0, 0, N_ALIGNED + j + 1))],
        out_specs=[pl.BlockSpec((D_MODEL, IN_BLOCK), lambda j: (0, j)),
                   pl.BlockSpec((D_MODEL, LANES), lambda j: (0, 0))],
        out_shape=[jax.ShapeDtypeStruct((D_MODEL, 2 * IN_BLOCK), BF16),
                   jax.ShapeDtypeStruct((D_MODEL, LANES), BF16)],
        compiler_params=_params(("arbitrary",)),
        name="prep_w_gates",
    )(w_in, w_in)
    return w_a, w_b, w_bg


def _inproj_kernel(x_ref, mod_ref, n1_ref, wa_ref, wb_ref, wbg_ref, lnw_ref, lnb_ref, adt_ref,
                   u_ref, vn_ref, qkv_ref, sz_ref, ga_ref, gb_ref, bg_ref):
    tm = u_ref.shape[0]
    x = x_ref[...]
    y = x * lax.rsqrt(jnp.mean(x * x, axis=-1, keepdims=True) + EPS) * n1_ref[...]
    mod = mod_ref[...]
    h = y * (1.0 + mod[:, 1:2, :]) + mod[:, 0:1, :]
    hb = h.reshape(tm, D_MODEL).astype(BF16)
    zbg = jnp.dot(hb, wbg_ref[...], preferred_element_type=F32)
    lane = lax.broadcasted_iota(jnp.int32, zbg.shape, 1)
    adt = adt_ref[...]
    g = -jnp.exp(adt[0:1, :]) * _softplus(zbg + adt[1:2, :])
    bg_ref[...] = jnp.where(lane < DN_HEADS, _sigmoid(zbg), g)

    n_a = wa_ref.shape[1] // 1024
    blk = lambda j: jnp.dot(hb, wa_ref[:, j * 1024:(j + 1) * 1024] if j < n_a else
                            wb_ref[:, (j - n_a) * 1024:(j - n_a + 1) * 1024], preferred_element_type=F32)
    u_ref[...] = _gelu(blk(0)).astype(u_ref.dtype)
    a = _gelu(blk(1))
    ac = a - jnp.mean(a, axis=-1, keepdims=True)
    var = jnp.mean(ac * ac, axis=-1, keepdims=True)
    vn_ref[...] = (ac * lax.rsqrt(var + EPS) * lnw_ref[...] + lnb_ref[...]).astype(vn_ref.dtype)
    for j in range(3):
        qkv_ref[:, j * 1024:(j + 1) * 1024] = blk(2 + j)
    z = blk(5)
    sz_ref[...] = (z * _sigmoid(z)).astype(sz_ref.dtype)
    ga_ref[...] = _sigmoid(blk(6)).astype(ga_ref.dtype)
    gb_ref[...] = _sigmoid(blk(7)).astype(gb_ref.dtype)


def _inproj(x, mod, norm1, w_a, w_b, w_bg, ln_w, ln_b, adt, bb, tt, vn_dtype):
    B, T, _ = x.shape
    tm = bb * tt
    tpb = T // tt
    n_tiles = (B // bb) * tpb
    M = B * T
    const = lambda i: (0, 0)
    rows = lambda i: (i, 0)
    resident = dict(pipeline_mode=pl.Buffered(1))
    in_specs = [pl.BlockSpec((bb, tt, D_MODEL), lambda i: (i // tpb, i % tpb, 0)),
                pl.BlockSpec((bb, N_MOD, D_MODEL), lambda i: (i // tpb, 0, 0)),
                pl.BlockSpec((1, D_MODEL), const),
                pl.BlockSpec(w_a.shape, const, **resident),
                pl.BlockSpec(w_b.shape, const, **resident),
                pl.BlockSpec((D_MODEL, LANES), const, **resident),
                pl.BlockSpec((1, A_WIDTH), const),
                pl.BlockSpec((1, A_WIDTH), const),
                pl.BlockSpec((2, LANES), const)]
    args = [x, mod, norm1.reshape(1, D_MODEL), w_a, w_b, w_bg, ln_w.reshape(1, A_WIDTH), ln_b.reshape(1, A_WIDTH),
            adt]
    out_specs = [pl.BlockSpec((tm, A_WIDTH), rows),
                 pl.BlockSpec((tm, A_WIDTH), rows),
                 pl.BlockSpec((tm, DN_CONV_CH), rows),
                 pl.BlockSpec((tm, DN_V), rows),
                 pl.BlockSpec((tm, D_MODEL), rows),
                 pl.BlockSpec((tm, D_MODEL), rows),
                 pl.BlockSpec((tm, LANES), rows)]
    out_shape = [jax.ShapeDtypeStruct((M, A_WIDTH), BF16),
                 jax.ShapeDtypeStruct((M, A_WIDTH), vn_dtype),
                 jax.ShapeDtypeStruct((M, DN_CONV_CH), F32),
                 jax.ShapeDtypeStruct((M, DN_V), BF16),
                 jax.ShapeDtypeStruct((M, D_MODEL), BF16),
                 jax.ShapeDtypeStruct((M, D_MODEL), BF16),
                 jax.ShapeDtypeStruct((M, LANES), F32)]
    return pl.pallas_call(
        _inproj_kernel,
        grid=(n_tiles,),
        in_specs=in_specs,
        out_specs=out_specs,
        out_shape=out_shape,
        compiler_params=_params(("arbitrary",)),
        name="in_proj",
    )(*args)


def _inv_unit_lower_minus_eye(lmats, nilpotent):
    ns = [-l for l in lmats]
    ps = list(lmats)
    k = 2
    while k < nilpotent:
        early = k <= _PASSES["inv_kmax"]
        ps = [_mm("invp" if early else "inv_hi", p, p) for p in ps]
        ns = [n + p + _mm("invn" if early else "inv_hi", n, p) for n, p in zip(ns, ps)]
        k *= 2
    return ns


def _head_sumsq(x):
    pair = 2 * DN_KDIM
    r = lax.broadcasted_iota(jnp.int32, (pair, pair), 0) >= DN_KDIM
    c = lax.broadcasted_iota(jnp.int32, (pair, pair), 1) >= DN_KDIM
    ones2 = jnp.where(r == c, 1.0, 0.0).astype(BF16)
    sq = (x * x).astype(BF16)
    return jnp.concatenate([jnp.dot(sq[:, p * pair:(p + 1) * pair], ones2, preferred_element_type=F32)
                            for p in range(x.shape[1] // pair)], axis=1)


def _deltanet_kernel(q_ref, k_ref, v_ref, z_ref, bg_ref, cinit_ref, s0_ref, cw_ref, dnn_ref,
                     bo_ref, sn_ref, xp_s, s_s, *, nprob, nseq, C):
    R = nseq * C
    c = pl.program_id(1)
    nc = pl.num_programs(1)

    @pl.when(c == 0)
    def _():
        xp_s[:, SUBLANES - (CONV_W - 1):SUBLANES, :] = cinit_ref[...]
        s_s[...] = s0_ref[...]

    @pl.when(c > 0)
    def _():
        xp_s[:, 0:SUBLANES, :] = xp_s[:, C:C + SUBLANES, :]

    for p in range(nprob):
        for s in range(nseq):
            i = p * nseq + s
            xp_s[i, SUBLANES:SUBLANES + C, 0:DN_QK] = q_ref[p, s * C:(s + 1) * C, :]
            xp_s[i, SUBLANES:SUBLANES + C, DN_QK:2 * DN_QK] = k_ref[p, s * C:(s + 1) * C, :]
            xp_s[i, SUBLANES:SUBLANES + C, 2 * DN_QK:DN_CONV_CH] = v_ref[p, s * C:(s + 1) * C, :]

    cw = cw_ref[...]
    base = SUBLANES - (CONV_W - 1)
    acc = None
    for j in range(CONV_W):
        term = xp_s[:, base + j:base + j + C, :] * cw[j:j + 1, :]
        acc = term if acc is None else acc + term
    qkv = acc.reshape(nprob * R, DN_CONV_CH)
    qkv = qkv * _sigmoid(qkv)
    q_all = qkv[:, 0:DN_QK]
    k_all = qkv[:, DN_QK:2 * DN_QK]
    v_all = qkv[:, 2 * DN_QK:DN_CONV_CH]
    qn_all = q_all * lax.rsqrt(_head_sumsq(q_all) + EPS) * (DN_KDIM ** -0.5)
    kn_all = k_all * lax.rsqrt(_head_sumsq(k_all) + EPS)

    row = lax.broadcasted_iota(jnp.int32, (R, R), 0)
    col = lax.broadcasted_iota(jnp.int32, (R, R), 1)
    if nseq > 1:
        shift = C.bit_length() - 1
        same = lax.shift_right_logical(row, shift) == lax.shift_right_logical(col, shift)
        tril = same & (col <= row)
        strict = same & (col < row)
    else:
        tril = col <= row
        strict = col < row
    tril_b = jnp.where(tril, 1.0, 0.0).astype(BF16)
    bg = [bg_ref[p] for p in range(nprob)]
    gc = [_dot_exact_lhs(tril_b, bg[p]) for p in range(nprob)]
    gct = [g.T for g in gc]
    dnn = dnn_ref[...]

    IT = [(p, h) for p in range(nprob) for h in range(DN_HEADS)]
    NI = range(len(IT))
    hsl = [slice(h * DN_KDIM, (h + 1) * DN_KDIM) for h in range(DN_HEADS)]
    rows = [slice(p * R, (p + 1) * R) for p in range(nprob)]
    qn = [qn_all[rows[p], hsl[h]] for p, h in IT]
    kn = [kn_all[rows[p], hsl[h]] for p, h in IT]
    vh = [v_all[rows[p], hsl[h]] for p, h in IT]
    beta = [bg[p][:, h:h + 1] for p, h in IT]
    gcol = [gc[p][:, DN_HEADS + h:DN_HEADS + h + 1] for p, h in IT]
    grow = [gct[p][DN_HEADS + h:DN_HEADS + h + 1, :] for p, h in IT]
    decay = [jnp.exp(jnp.minimum(gcol[i] - grow[i], 0.0)) for i in NI]
    eg = [jnp.exp(gcol[i]) for i in NI]
    kb = [kn[i] * beta[i] for i in NI]
    vb = [vh[i] * beta[i] for i in NI]
    a2 = [_mm("kk", jnp.concatenate([kb[i], qn[i]], axis=0), kn[i], _NT) for i in NI]
    lmat = [jnp.where(strict, a2[i][:R] * decay[i], 0.0) for i in NI]
    attn = [jnp.where(tril, a2[i][R:] * decay[i], 0.0) for i in NI]
    tn = _inv_unit_lower_minus_eye(lmat, C)
    rhs = [jnp.concatenate([vb[i], kb[i] * eg[i]], axis=1) for i in NI]
    sol = [rhs[i] + _mm("sol", tn[i], rhs[i]) for i in NI]
    u = [x[:, :DN_VDIM] for x in sol]
    w = [x[:, DN_VDIM:] for x in sol]
    qg = [qn[i] * eg[i] for i in NI]
    vnew = [[None] * nseq for _ in NI]
    qs = [[None] * nseq for _ in NI]
    for s in range(nseq):
        rs = slice(s * C, (s + 1) * C)
        st = [s_s[p * nseq + s, h] for p, h in IT]
        ws = [_mm("ws", jnp.concatenate([w[i][rs], qg[i][rs]], axis=0), st[i]) for i in NI]
        for i in NI:
            vnew[i][s] = u[i][rs] - ws[i][:C]
            qs[i][s] = ws[i][C:]
        glast = [gcol[i][(s + 1) * C - 1:(s + 1) * C, :] for i in NI]
        kd = [kn[i][rs] * jnp.exp(glast[i] - gcol[i][rs]) for i in NI]
        upd = [_mm("state", kd[i], vnew[i][s], _TN) for i in NI]
        for i, (p, h) in enumerate(IT):
            s_s[p * nseq + s, h] = st[i] * jnp.exp(glast[i]) + upd[i]
    cat = lambda parts: parts[0] if nseq == 1 else jnp.concatenate(parts, axis=0)
    o = [cat(qs[i]) + _mm("attn", attn[i], cat(vnew[i])) for i in NI]
    o_all = jnp.concatenate([jnp.concatenate(o[p * DN_HEADS:(p + 1) * DN_HEADS], axis=1) for p in range(nprob)],
                            axis=0)
    dnn_all = jnp.concatenate([dnn] * DN_HEADS, axis=1)
    on = o_all * lax.rsqrt(_head_sumsq(o_all) * (1.0 / DN_VDIM) + EPS) * dnn_all
    for p in range(nprob):
        bo_ref[p] = (on[rows[p]] * z_ref[p].astype(F32)).astype(bo_ref.dtype)

    @pl.when(c == nc - 1)
    def _():
        sn_ref[...] = s_s[...]


def _deltanet(zqkv, sz, bg, conv_init, s0, conv_w, dn_norm, nprob, nseq, C, nc):
    M = zqkv.shape[0]
    R = nseq * C
    G = M // (R * nc)
    view = lambda a: a.reshape(G, R * nc, a.shape[-1])
    rmap = lambda col: (lambda g, c: (g, c, col))
    nsq = nprob * nseq
    state_spec = pl.BlockSpec((nsq, DN_HEADS, DN_KDIM, DN_VDIM), lambda g, c: (g, 0, 0, 0))
    bo, s_new = pl.pallas_call(
        functools.partial(_deltanet_kernel, nprob=nprob, nseq=nseq, C=C),
        grid=(G // nprob, nc),
        in_specs=[pl.BlockSpec((nprob, R, DN_QK), rmap(0)),
                  pl.BlockSpec((nprob, R, DN_QK), rmap(1)),
                  pl.BlockSpec((nprob, R, DN_V), rmap(2)),
                  pl.BlockSpec((nprob, R, DN_V), rmap(0)),
                  pl.BlockSpec((nprob, R, LANES), rmap(0)),
                  pl.BlockSpec((nsq, CONV_W - 1, DN_CONV_CH), lambda g, c: (g, 0, 0)),
                  state_spec,
                  pl.BlockSpec((CONV_W, DN_CONV_CH), lambda g, c: (0, 0)),
                  pl.BlockSpec((1, DN_VDIM), lambda g, c: (0, 0))],
        out_specs=[pl.BlockSpec((nprob, R, DN_V), rmap(0)), state_spec],
        out_shape=[jax.ShapeDtypeStruct((G, R * nc, DN_V), BF16),
                   jax.ShapeDtypeStruct(s0.shape, F32)],
        scratch_shapes=[pltpu.VMEM((nsq, C + SUBLANES, DN_CONV_CH), F32),
                        pltpu.VMEM((nsq, DN_HEADS, DN_KDIM, DN_VDIM), F32)],
        compiler_params=_params(("arbitrary", "arbitrary")),
        name="deltanet",
    )(view(zqkv), view(zqkv), view(zqkv), view(sz), view(bg), conv_init, s0, conv_w, dn_norm.reshape(1, DN_VDIM))
    return bo.reshape(M, DN_V), s_new


def _outproj_kernel(u_ref, vn_ref, ga_ref, gb_ref, bo_ref, x_ref, mod_ref, ws_ref, bs_ref, wpa_ref, wpb_ref,
                    wo_ref, n2_ref, wr_ref, br_ref, before_ref, x1_ref, h2_ref, rt_ref, cnt_ref, cnt_s):
    @pl.when(pl.program_id(0) == 0)
    def _():
        cnt_s[...] = jnp.zeros_like(cnt_s)

    tm = u_ref.shape[0]
    hm = tm // OUT_SPLIT
    bb, tt, _ = x_ref.shape
    HR = range(OUT_SPLIT)
    rows = [slice(h * hm, (h + 1) * hm) for h in HR]
    if bb == 1:
        xsl = [(slice(None), slice(h * (tt // OUT_SPLIT), (h + 1) * (tt // OUT_SPLIT))) for h in HR]
        msl = [slice(None)] * OUT_SPLIT
    else:
        xsl = [(slice(h * (bb // OUT_SPLIT), (h + 1) * (bb // OUT_SPLIT)), slice(None)) for h in HR]
        msl = [s[0] for s in xsl]

    def spatial(h):
        parts = []
        for r in range(hm // GMLP_CHUNK):
            rs = slice(h * hm + r * GMLP_CHUNK, h * hm + (r + 1) * GMLP_CHUNK)
            vn = vn_ref[rs, :].astype(BF16)
            cols = [jnp.dot(ws_ref[g], vn[:, g * A_GDIM:(g + 1) * A_GDIM], preferred_element_type=F32)
                    for g in range(A_GROUPS)]
            parts.append(jnp.concatenate(cols, axis=1) + bs_ref[...])
        return parts[0] if len(parts) == 1 else jnp.concatenate(parts, axis=0)

    sv = [spatial(h) for h in HR]
    a_out = [u_ref[rows[h], :].astype(F32) * sv[h] for h in HR]
    pa = [_dot(a_out[h], wpa_ref[...]) for h in HR]
    pb = [_dot(bo_ref[rows[h], :], wpb_ref[...]) for h in HR]
    m = [ga_ref[rows[h], :].astype(F32) * pa[h] + gb_ref[rows[h], :].astype(F32) * pb[h] for h in HR]
    mix = [_dot(m[h], wo_ref[...]) for h in HR]
    mod = [mod_ref[msl[h]] for h in HR]
    x = [x_ref[xsl[h][0], xsl[h][1], :] for h in HR]
    x1 = [x[h] + mod[h][:, 2:3, :] * mix[h].reshape(x[h].shape) for h in HR]
    for h in HR:
        x1_ref[xsl[h][0], xsl[h][1], :] = x1[h]
    y = [x1[h] * lax.rsqrt(jnp.mean(x1[h] * x1[h], axis=-1, keepdims=True) + EPS) * n2_ref[...] for h in HR]
    h2 = [(y[h] * (1.0 + mod[h][:, 4:5, :]) + mod[h][:, 3:4, :]).reshape(hm, D_MODEL) for h in HR]
    for h in HR:
        for j in range(SUBLANES):
            h2_ref[pl.ds(h * hm * SUBLANES + j, hm, stride=SUBLANES), :] = h2[h][:, j * LANES:(j + 1) * LANES]

    wr = wr_ref[...]
    lane = lax.broadcasted_iota(jnp.int32, (hm, LANES), 1)
    lanef = lane.astype(F32)
    neg = jnp.float32(-jnp.inf)
    big = jnp.float32(1e9)

    def route(h2h):
        hh, hl = _hilo(h2h)
        p = jnp.dot(hh, wr, preferred_element_type=F32)
        logits = (p[:, :LANES] + (p[:, LANES:] + jnp.dot(hl, wr[:, :LANES], preferred_element_type=F32))
                  + br_ref[...])
        gl = jnp.where(lane < N_GROUPS, logits, neg)
        gmax = jnp.max(gl, axis=-1, keepdims=True)
        gsel = jnp.min(jnp.where(gl == gmax, lanef, big), axis=-1, keepdims=True)
        gp = 1.0 / jnp.sum(jnp.exp(gl - gmax), axis=-1, keepdims=True)
        lo = ROUTER_LANE0 + EXP_PER_GROUP * gsel
        in_grp = (lanef >= lo) & (lanef < lo + EXP_PER_GROUP)
        el = jnp.where(in_grp, logits, neg)
        m1 = jnp.max(el, axis=-1, keepdims=True)
        i1 = jnp.min(jnp.where(el == m1, lanef, big), axis=-1, keepdims=True)
        el2 = jnp.where(lanef == i1, neg, el)
        m2 = jnp.max(el2, axis=-1, keepdims=True)
        i2 = jnp.min(jnp.where(el2 == m2, lanef, big), axis=-1, keepdims=True)
        ex = jnp.exp(m2 - m1)
        return i1 - ROUTER_LANE0, i2 - ROUTER_LANE0, gp / (1.0 + ex), gp * ex / (1.0 + ex)

    routed = [route(h2[h]) for h in HR]

    seen0 = cnt_s[...]
    for h in HR:
        e1, e2, w1, w2 = routed[h]
        oh1 = jnp.where(lanef == e1, 1.0, 0.0)
        oh2 = jnp.where(lanef == e2, 1.0, 0.0)
        oh = oh1 + oh2
        seen = jnp.dot(before_ref[...], oh.astype(BF16), preferred_element_type=F32) + seen0
        r1 = jnp.sum(oh1 * seen, axis=-1, keepdims=True)
        r2 = jnp.sum(oh2 * seen, axis=-1, keepdims=True)
        seen0 = seen0 + jnp.sum(oh, axis=0, keepdims=True)
        rec = jnp.zeros((hm, LANES), F32)
        for k, val in enumerate((e1, e2, r1, r2, w1, w2)):
            rec = jnp.where(lane == k, val, rec)
        rt_ref[rows[h], :] = rec
    cnt_s[...] = seen0
    cnt_ref[...] = seen0


def _outproj(u, vn, ga, gb, bo, x, mod, ws_eff, bs_full, w_pa, w_pb, w_o, norm2, w_r, b_r, bb, tt):
    B, T, _ = x.shape
    tm = bb * tt
    tpb = T // tt
    n_tiles = (B // bb) * tpb
    M = B * T
    cmap = lambda col: (lambda i: (i, col))
    full2 = lambda i: (0, 0)
    return pl.pallas_call(
        _outproj_kernel,
        grid=(n_tiles,),
        in_specs=[pl.BlockSpec((tm, A_WIDTH), cmap(0)),
                  pl.BlockSpec((tm, A_WIDTH), cmap(0)),
                  pl.BlockSpec((tm, D_MODEL), cmap(0)),
                  pl.BlockSpec((tm, D_MODEL), cmap(0)),
                  pl.BlockSpec((tm, DN_V), cmap(0)),
                  pl.BlockSpec((bb, tt, D_MODEL), lambda i: (i // tpb, i % tpb, 0)),
                  pl.BlockSpec((bb, N_MOD, D_MODEL), lambda i: (i // tpb, 0, 0)),
                  pl.BlockSpec((A_GROUPS, GMLP_CHUNK, GMLP_CHUNK), lambda i: (0, 0, 0)),
                  pl.BlockSpec((GMLP_CHUNK, A_WIDTH), full2),
                  pl.BlockSpec((A_WIDTH, D_MODEL), full2),
                  pl.BlockSpec((DN_V, D_MODEL), full2),
                  pl.BlockSpec((D_MODEL, D_MODEL), full2),
                  pl.BlockSpec((1, D_MODEL), full2),
                  pl.BlockSpec((D_MODEL, 2 * LANES), full2),
                  pl.BlockSpec((1, LANES), full2),
                  pl.BlockSpec((tm // OUT_SPLIT, tm // OUT_SPLIT), full2)],
        out_specs=[pl.BlockSpec((bb, tt, D_MODEL), lambda i: (i // tpb, i % tpb, 0)),
                   pl.BlockSpec((tm * SUBLANES, LANES), cmap(0)),
                   pl.BlockSpec((tm, LANES), cmap(0)),
                   pl.BlockSpec((1, LANES), full2)],
        out_shape=[jax.ShapeDtypeStruct((B, T, D_MODEL), F32),
                   jax.ShapeDtypeStruct((M * SUBLANES, LANES), F32),
                   jax.ShapeDtypeStruct((M, LANES), F32),
                   jax.ShapeDtypeStruct((1, LANES), F32)],
        scratch_shapes=[pltpu.VMEM((1, LANES), F32)],
        compiler_params=_params(("arbitrary",)),
        name="out_proj",
    )(u, vn, ga, gb, bo, x, mod, ws_eff, bs_full, w_pa, w_pb, w_o, norm2.reshape(1, D_MODEL), w_r, b_r,
      jnp.tril(jnp.ones((tm // OUT_SPLIT, tm // OUT_SPLIT), BF16), -1))


def _tile_copy(src_ref, src_off, dst_ref, dst_off, sem):
    return pltpu.make_async_copy(src_ref.at[pl.ds(pl.multiple_of(src_off, SUBLANES), SUBLANES), :],
                                 dst_ref.at[pl.ds(pl.multiple_of(dst_off, SUBLANES), SUBLANES), :], sem)


def _to_matrix(ref, n):
    return jnp.concatenate([ref[pl.ds(j, n, stride=SUBLANES), :] for j in range(SUBLANES)], axis=1)


def _zero_fill(pad_lo_ref, pad_n_ref, na_ref, xs_ref, zero_s, zsem):
    blk = EXPERT_TILE * SUBLANES
    n_tiles = xs_ref.shape[0] // blk
    zero_s[...] = jnp.zeros_like(zero_s)
    run = lambda off, rows: pltpu.make_async_copy(
        zero_s.at[pl.ds(0, rows * SUBLANES), :],
        xs_ref.at[pl.ds(pl.multiple_of(off * SUBLANES, SUBLANES), rows * SUBLANES), :], zsem)
    sizes = [1 << b for b in reversed(range(EXPERT_TILE.bit_length() - 1))]
    for wait in (False, True):
        for e in range(N_EXPERTS):
            off = pad_lo_ref[e]
            for rows in sizes:
                bit = pad_n_ref[e] & rows

                @pl.when(bit != 0)
                def _(off=off, rows=rows):
                    run(0, rows).wait() if wait else run(off, rows).start()

                off = off + bit

        def idle(i, carry):
            run(0, EXPERT_TILE).wait() if wait else run(i * EXPERT_TILE, EXPERT_TILE).start()
            return carry

        lax.fori_loop(na_ref[0], n_tiles, idle, 0)


def _dispatch_kernel(pad_lo_ref, pad_n_ref, na_ref, pos_ref, *refs, first_tile):
    n_paths = len(first_tile) - 1
    n_steps = first_tile[-1]
    h_refs = refs[:n_paths]
    xs_ref, zero_s, ring_s, sems, zsem = refs[n_paths:]
    n_ring = ring_s.shape[0]
    i = pl.program_id(0)
    slot = i % n_ring

    @pl.when(i == 0)
    def _():
        _zero_fill(pad_lo_ref, pad_n_ref, na_ref, xs_ref, zero_s, zsem)

    tm = h_refs[0].shape[0] // SUBLANES

    def drain(s):
        def wait(t, carry):
            for k in range(2):
                _tile_copy(ring_s.at[s], 0, xs_ref, 0, sems.at[s]).wait()
            return carry

        lax.fori_loop(0, tm, wait, 0, unroll=8)

    @pl.when(i >= n_ring)
    def _():
        drain(slot)

    for p, h_ref in enumerate(h_refs):
        @pl.when((i >= first_tile[p]) & (i < first_tile[p + 1]))
        def _(h_ref=h_ref):
            ring_s[slot] = h_ref[...]

            def issue(t, carry):
                for k in range(2):
                    _tile_copy(ring_s.at[slot], t * SUBLANES, xs_ref, pos_ref[0, k, t],
                               sems.at[slot]).start(priority=k)
                return carry

            lax.fori_loop(0, tm, issue, 0, unroll=8)

    @pl.when(i == n_steps - 1)
    def _():
        for d in range(min(n_ring, n_steps)):
            drain((n_steps - 1 - d) % n_ring)


def _dispatch(h2s, pos, tm, n_rows, pad_lo, pad_n, n_active):
    blk = tm * SUBLANES
    tiles = [h.shape[0] // blk for h in h2s]
    first_tile = [sum(tiles[:p]) for p in range(len(tiles) + 1)]
    hmap = lambda p: (lambda i, *_: (jnp.clip(i - first_tile[p], 0, tiles[p] - 1), 0))
    grid_spec = pltpu.PrefetchScalarGridSpec(
        num_scalar_prefetch=3,
        grid=(first_tile[-1],),
        in_specs=[pl.BlockSpec((1, 2, tm), lambda i, *_: (i, 0, 0), memory_space=pltpu.SMEM)]
                 + [pl.BlockSpec((blk, LANES), hmap(p)) for p in range(len(tiles))],
        out_specs=pl.BlockSpec(memory_space=pl.ANY),
        scratch_shapes=[pltpu.VMEM((EXPERT_TILE * SUBLANES, LANES), F32),
                        pltpu.VMEM((DISPATCH_RING, blk, LANES), F32),
                        pltpu.SemaphoreType.DMA((DISPATCH_RING,)), pltpu.SemaphoreType.DMA(())],
    )
    return pl.pallas_call(
        functools.partial(_dispatch_kernel, first_tile=tuple(first_tile)),
        grid_spec=grid_spec,
        out_shape=jax.ShapeDtypeStruct((n_rows * SUBLANES, LANES), F32),
        compiler_params=_params(("arbitrary",)),
        name="moe_dispatch",
    )(pad_lo, pad_n, n_active, pos, *h2s)


def _expert_kernel(te_ref, na_ref, x_ref, *refs):
    del te_ref
    w_refs, o_ref = refs[:-1], refs[-1]
    blk = EXPERT_TILE * SUBLANES
    live = EXPERTS_PER_STEP * pl.program_id(0) < na_ref[0]

    @pl.when(live)
    def _():
        TR = range(EXPERTS_PER_STEP)
        wg, wu, wd = ([w_refs[3 * t + k][0] for t in TR] for k in range(3))
        x = [_to_matrix(x_ref.at[pl.ds(t * blk, blk), :], EXPERT_TILE).astype(BF16) for t in TR]
        a = [_dot(x[t], wg[t]) for t in TR]
        u = [_dot(x[t], wu[t]) for t in TR]
        o = [_dot(a[t] * _sigmoid(a[t]) * u[t], wd[t]) for t in TR]
        for t in TR:
            for j in range(SUBLANES):
                o_ref[pl.ds(t * blk + j, EXPERT_TILE, stride=SUBLANES), :] = o[t][:, j * LANES:(j + 1) * LANES]

    @pl.when(jnp.logical_not(live))
    def _():
        o_ref[...] = jnp.zeros_like(o_ref)


def _experts(xs, tile_expert, n_active, wg, wu, wd):
    blk = EXPERT_TILE * SUBLANES * EXPERTS_PER_STEP
    n_steps = xs.shape[0] // blk
    last = lambda na: jnp.maximum(na[0] - 1, 0)
    xmap = lambda i, te, na: (jnp.minimum(i, last(na) // EXPERTS_PER_STEP), 0)
    wmap = lambda t: (lambda i, te, na: (te[jnp.minimum(EXPERTS_PER_STEP * i + t, last(na))], 0, 0))
    w_specs = []
    for t in range(EXPERTS_PER_STEP):
        w_specs += [pl.BlockSpec((1, D_MODEL, D_EXPERT), wmap(t)),
                    pl.BlockSpec((1, D_MODEL, D_EXPERT), wmap(t)),
                    pl.BlockSpec((1, D_EXPERT, D_MODEL), wmap(t))]
    grid_spec = pltpu.PrefetchScalarGridSpec(
        num_scalar_prefetch=2,
        grid=(n_steps,),
        in_specs=[pl.BlockSpec((blk, LANES), xmap)] + w_specs,
        out_specs=pl.BlockSpec((blk, LANES), lambda i, te, na: (i, 0)),
    )
    return pl.pallas_call(
        _expert_kernel,
        grid_spec=grid_spec,
        out_shape=jax.ShapeDtypeStruct(xs.shape, F32),
        compiler_params=_params(("arbitrary",)),
        name="moe_experts",
    )(tile_expert, n_active, xs, *([wg, wu, wd] * EXPERTS_PER_STEP))


def _combine_kernel(pos_ref, pos_next_ref, rt_ref, x1_ref, mod_ref, fn_ref, ys_ref, y_ref, g_s, sems):
    tm = rt_ref.shape[0]
    i = pl.program_id(0)
    n = pl.num_programs(0)
    slot = i % 2

    def gather(p_ref, s):
        def issue(t, carry):
            for k in range(2):
                _tile_copy(ys_ref, p_ref[0, k, t], g_s.at[s, k], t * SUBLANES, sems.at[s]).start(priority=k)
            return carry

        lax.fori_loop(0, tm, issue, 0, unroll=8)

    @pl.when(i == 0)
    def _():
        gather(pos_ref, slot)

    @pl.when(i + 1 < n)
    def _():
        gather(pos_next_ref, 1 - slot)

    def drain(t, carry):
        for k in range(2):
            _tile_copy(ys_ref, 0, g_s.at[slot, k], 0, sems.at[slot]).wait()
        return carry

    lax.fori_loop(0, tm, drain, 0, unroll=8)

    rt = rt_ref[...]
    moe = rt[:, 4:5] * _to_matrix(g_s.at[slot, 0], tm) + rt[:, 5:6] * _to_matrix(g_s.at[slot, 1], tm)
    mod = mod_ref[...]
    x1 = x1_ref[...]
    x2 = x1 + mod[:, 5:6, :] * moe.reshape(x1.shape)
    y_ref[...] = x2 * lax.rsqrt(jnp.mean(x2 * x2, axis=-1, keepdims=True) + EPS) * fn_ref[...]


def _combine(ys, pos, route, x1, mod, final_norm, bb, tt):
    B, T, _ = x1.shape
    tm = bb * tt
    tpb = T // tt
    n_tiles = (B // bb) * tpb
    xmap = lambda i: (i // tpb, i % tpb, 0)
    return pl.pallas_call(
        _combine_kernel,
        grid=(n_tiles,),
        in_specs=[pl.BlockSpec((1, 2, tm), lambda i: (i, 0, 0), memory_space=pltpu.SMEM),
                  pl.BlockSpec((1, 2, tm), lambda i: (jnp.minimum(i + 1, n_tiles - 1), 0, 0),
                               memory_space=pltpu.SMEM),
                  pl.BlockSpec((tm, LANES), lambda i: (i, 0)),
                  pl.BlockSpec((bb, tt, D_MODEL), xmap),
                  pl.BlockSpec((bb, N_MOD, D_MODEL), lambda i: (i // tpb, 0, 0)),
                  pl.BlockSpec((1, D_MODEL), lambda i: (0, 0)),
                  pl.BlockSpec(memory_space=pl.ANY)],
        out_specs=pl.BlockSpec((bb, tt, D_MODEL), xmap),
        out_shape=jax.ShapeDtypeStruct((B, T, D_MODEL), F32),
        scratch_shapes=[pltpu.VMEM((2, 2, tm * SUBLANES, LANES), F32), pltpu.SemaphoreType.DMA((2,))],
        compiler_params=_params(("arbitrary",)),
        name="moe_combine",
    )(pos, pos, route, x1, mod, final_norm.reshape(1, D_MODEL), ys)


def _moe(paths, wg, wu, wd, final_norm):
    n_assign = sum(2 * p["route"].shape[0] for p in paths)
    max_tiles = (n_assign + N_EXPERTS * (EXPERT_TILE - 1)) // EXPERT_TILE
    max_tiles += -max_tiles % EXPERTS_PER_STEP
    cnts = [p["counts"][0, :N_EXPERTS].astype(jnp.int32) for p in paths]
    cnt = sum(cnts)
    nt = (cnt + EXPERT_TILE - 1) // EXPERT_TILE
    ends = jnp.cumsum(nt)
    base = ((ends - nt) * EXPERT_TILE).astype(jnp.int32)
    n_active = ends[-1:].astype(jnp.int32)
    tile_expert = jnp.minimum(jnp.sum(jnp.arange(max_tiles)[:, None] >= ends[None, :], axis=1),
                              N_EXPERTS - 1).astype(jnp.int32)
    tm = paths[0]["bb"] * paths[0]["tt"]
    assert all(p["bb"] * p["tt"] == tm for p in paths)
    start = base
    for p, c in zip(paths, cnts):
        eid = p["route"][:, 0:2].astype(jnp.int32)
        rank = p["route"][:, 2:4].astype(jnp.int32)
        first_row = jnp.sum(jnp.where(eid[..., None] == jnp.arange(N_EXPERTS), start, 0), axis=-1)
        pos = (first_row + rank) * SUBLANES
        p["pos"] = jnp.transpose(pos.reshape(-1, tm, 2), (0, 2, 1))
        start = start + c
    xs = _dispatch([p["h2"] for p in paths], jnp.concatenate([p["pos"] for p in paths], axis=0), tm,
                   max_tiles * EXPERT_TILE, base + cnt, nt * EXPERT_TILE - cnt, n_active)
    ys = _experts(xs, tile_expert, n_active, wg, wu, wd)
    return [_combine(ys, p["pos"], p["route"], p["x1"], p["mod"], final_norm, p["bb"], p["tt"]) for p in paths]


def _mixers(x, mod, conv_init, s0, prm, *, bb, tt, dn_nprob, dn_nseq, dn_chunk, vn_dtype):
    B, T, _ = x.shape
    u, vn, zqkv, sz, ga, gb, bg = _inproj(x, mod, prm["norm1"], prm["w_a"], prm["w_b"], prm["w_bg"], prm["ln_v_w"],
                                          prm["ln_v_b"], prm["adt"], bb, tt, vn_dtype)
    nc = T // dn_chunk
    bo, s_new = _deltanet(zqkv, sz, bg, conv_init, s0, prm["conv_w"], prm["dn_norm"], dn_nprob, dn_nseq, dn_chunk, nc)
    L = min(GMLP_CHUNK, T)
    reps = GMLP_CHUNK // L
    wm = prm["w_s_tril"][:, :L, :L]
    if reps > 1:
        blk = (jnp.arange(GMLP_CHUNK)[:, None] // L) == (jnp.arange(GMLP_CHUNK)[None, :] // L)
        wm = jnp.where(blk[None], jnp.tile(wm, (1, reps, reps)), 0.0)
    ws_eff = wm.astype(BF16)
    bs_rows = jnp.tile(jnp.transpose(prm["b_s"][:, :L]), (reps, 1))
    bs_full = jnp.repeat(bs_rows, A_GDIM, axis=1)
    x1, h2, route, counts = _outproj(u, vn, ga, gb, bo, x, mod, ws_eff, bs_full, prm["w_pa"], prm["w_pb"],
                                     prm["w_o"], prm["norm2"], prm["w_r"], prm["b_r"], bb, tt)
    tail = min(T, CONV_W - 1)
    zqkv_tail = zqkv.reshape(B, T, DN_CONV_CH)[:, T - tail:, :]
    conv_new = jnp.concatenate([conv_init, zqkv_tail], axis=1)[:, -(CONV_W - 1):, :]
    path = dict(h2=h2, route=route, counts=counts, x1=x1, mod=mod, bb=bb, tt=tt)
    return path, conv_new, s_new, vn.reshape(B, T, A_WIDTH)


def kernel(x_prompt, x_sample, c_prompt, c_sample, state_conv, state_delta, w_ada, b_ada, norm1, norm2, w_in, conv_w, a_log, dt_bias, dn_norm, ln_v_w, ln_v_b, w_s, b_s, w_pa, w_pb, w_o, w_rg, b_rg, w_re, b_re, w_e_gate, w_e_up, w_e_down, final_norm):
    depth = w_ada.shape[0]
    assert depth == 1
    bp, tp, _ = x_prompt.shape
    bs_, ts, _ = x_sample.shape
    l = 0
    assert w_in.shape[-1] == (N_ALIGNED + 2) * IN_BLOCK + 2 * DN_HEADS
    w_a, w_b, w_bg = _prep_in_weights(w_in[l:l + 1])
    adt = jnp.stack([jnp.pad(a_log[l], (DN_HEADS, LANES - 2 * DN_HEADS)),
                     jnp.pad(dt_bias[l], (DN_HEADS, LANES - 2 * DN_HEADS))])
    w_r = jnp.pad(jnp.concatenate([w_rg[l], w_re[l]], axis=1), ((0, 0), (0, LANES - N_GROUPS - N_EXPERTS)))
    w_r_hi = w_r.astype(BF16)
    w_r = jnp.concatenate([w_r_hi, (w_r - w_r_hi.astype(F32)).astype(BF16)], axis=1)
    b_r = jnp.pad(jnp.concatenate([b_rg[l], b_re[l]]), (0, LANES - N_GROUPS - N_EXPERTS)).reshape(1, LANES)
    tri = jnp.tril(jnp.ones((GMLP_CHUNK, GMLP_CHUNK), F32))
    prm = dict(
        norm1=norm1[l], norm2=norm2[l], w_a=w_a, w_b=w_b, w_bg=w_bg, adt=adt, ln_v_w=ln_v_w[l], ln_v_b=ln_v_b[l],
        conv_w=conv_w[l], dn_norm=dn_norm[l], w_s_tril=w_s[l] * tri, b_s=b_s[l],
        w_pa=w_pa[l].astype(BF16), w_pb=w_pb[l].astype(BF16), w_o=w_o[l].astype(BF16), w_r=w_r, b_r=b_r,
        w_e_gate=w_e_gate[l].reshape(N_EXPERTS, D_MODEL, D_EXPERT),
        w_e_up=w_e_up[l].reshape(N_EXPERTS, D_MODEL, D_EXPERT),
        w_e_down=w_e_down[l].reshape(N_EXPERTS, D_EXPERT, D_MODEL),
    )

    c_all = jnp.concatenate([c_prompt, c_sample], axis=0)
    mod = _ada(c_all, w_ada[l], b_ada[l]).reshape(bp + bs_, N_MOD, D_MODEL)
    mod_p, mod_s = mod[:bp], mod[bp:]

    cinit_p = jnp.zeros((bp, CONV_W - 1, DN_CONV_CH), F32)
    cinit_s = state_conv[l]
    s0_p = jnp.zeros((bp, DN_HEADS, DN_KDIM, DN_VDIM), F32)

    path_p, conv_p, delta_p, _ = _mixers(x_prompt, mod_p, cinit_p, s0_p, prm,
                                         bb=1, tt=min(ROW_TILE, tp), dn_nprob=4 if bp % 4 == 0 else 1, dn_nseq=1,
                                         dn_chunk=min(DN_CHUNK, tp), vn_dtype=BF16)
    path_s, conv_s, delta_s, vn_s = _mixers(x_sample, mod_s, cinit_s, state_delta[l], prm,
                                            bb=min(bs_, ROW_TILE // ts), tt=ts,
                                            dn_nprob=2 if bs_ % (2 * DN_CHUNK // ts) == 0 else 1,
                                            dn_nseq=min(bs_, DN_CHUNK // ts), dn_chunk=ts, vn_dtype=F32)
    yp, ys = _moe([path_p, path_s], prm["w_e_gate"], prm["w_e_up"], prm["w_e_down"], final_norm)
    return (yp, ys, conv_p[None], conv_s[None], delta_p[None], delta_s[None], vn_s[None])
```

```python
import functools

import jax
import jax.numpy as jnp
from jax import lax
from jax.experimental import pallas as pl
from jax.experimental.pallas import tpu as pltpu

F32 = jnp.float32
BF16 = jnp.bfloat16

D_MODEL = 1024
A_WIDTH = 1024
A_GROUPS = 4
A_GDIM = A_WIDTH // A_GROUPS
GMLP_CHUNK = 128
DN_HEADS = 8
DN_KDIM = 128
DN_VDIM = 128
DN_QK = DN_HEADS * DN_KDIM
DN_V = DN_HEADS * DN_VDIM
DN_CONV_CH = 2 * DN_QK + DN_V
CONV_W = 4
DN_CHUNK = 64
N_GROUPS = 4
EXP_PER_GROUP = 8
N_EXPERTS = N_GROUPS * EXP_PER_GROUP
D_EXPERT = 256
N_MOD = 6
EPS = 1e-6

LANES = 128
SUBLANES = 8
ROUTER_LANE0 = N_GROUPS
VMEM_LIMIT = 56 * 1024 * 1024
ROW_TILE = 512
IN_SPLIT = 2
OUT_SPLIT = 2
EXPERT_TILE = 512
EXPERTS_PER_STEP = 1


def _sigmoid(x):
    return 1.0 / (1.0 + jnp.exp(-x))


def _gelu(x):
    return 0.5 * x * (1.0 + lax.erf(x * 0.7071067811865476))


def _softplus(x):
    return jnp.maximum(x, 0.0) + jnp.log(1.0 + jnp.exp(-jnp.abs(x)))


def _dot(a, b):
    return jnp.dot(a.astype(BF16), b.astype(BF16), preferred_element_type=F32)


def _split3(x):
    hi = x.astype(BF16)
    r1 = x - hi.astype(F32)
    mid = r1.astype(BF16)
    lo = (r1 - mid.astype(F32)).astype(BF16)
    return hi, mid, lo


def _dot_exact_lhs(a_bf16, x):
    hi, mid, lo = _split3(x)
    f = lambda p: jnp.dot(a_bf16, p, preferred_element_type=F32)
    return f(hi) + f(mid) + f(lo)


_PASSES = {"invp": 3, "invn": 3, "inv_hi": 1, "inv_kmax": 8, "sol": 1, "kk": 1, "ws": 1, "attn": 1, "state": 1}


def _hilo(x):
    hi = x.astype(BF16)
    lo = (x - hi.astype(F32)).astype(BF16)
    return hi, lo


def _mm(site, a, b, dims=(((1,), (0,)), ((), ()))):
    f = lambda p, q: lax.dot_general(p, q, dims, preferred_element_type=F32)
    if _PASSES[site] == 1:
        return f(a.astype(BF16), b.astype(BF16))
    ah, al = _hilo(a)
    bh, bl = _hilo(b)
    return f(ah, bh) + (f(ah, bl) + f(al, bh))


_NT = (((1,), (1,)), ((), ()))
_TN = (((0,), (0,)), ((), ()))

def _params(sem):
    return pltpu.CompilerParams(dimension_semantics=sem, vmem_limit_bytes=VMEM_LIMIT)


def _ada_kernel(c_ref, w_ref, b_ref, o_ref):
    c = c_ref[...]
    o_ref[...] = _dot(c * _sigmoid(c), w_ref[...]) + b_ref[...]


def _ada(c_all, w_ada, b_ada):
    n = c_all.shape[0]
    width = w_ada.shape[1]
    bn = 512
    return pl.pallas_call(
        _ada_kernel,
        grid=(width // bn,),
        in_specs=[pl.BlockSpec((n, D_MODEL), lambda j: (0, 0)),
                  pl.BlockSpec((D_MODEL, bn), lambda j: (0, j)),
                  pl.BlockSpec((1, bn), lambda j: (0, j))],
        out_specs=pl.BlockSpec((n, bn), lambda j: (0, j)),
        out_shape=jax.ShapeDtypeStruct((n, width), F32),
        compiler_params=_params(("arbitrary",)),
        name="ada_mod",
    )(c_all, w_ada, b_ada.reshape(1, width))


IN_BLOCK = 1024
N_ALIGNED = 6


def _cast_kernel(w_ref, o_ref):
    o_ref[...] = w_ref[0].astype(o_ref.dtype)


def _realign_kernel(a_ref, b_ref, o_ref, bg_ref):
    s = 2 * DN_HEADS
    a = a_ref[0]
    o_ref[...] = jnp.concatenate([a[:, s:], b_ref[0, :, :s]], axis=1).astype(o_ref.dtype)

    @pl.when(pl.program_id(0) == 0)
    def _():
        lane = lax.broadcasted_iota(jnp.int32, (D_MODEL, LANES), 1)
        bg_ref[...] = jnp.where(lane < s, a[:, :LANES], 0.0).astype(bg_ref.dtype)


def _prep_in_weights(w_in):
    w_a = pl.pallas_call(
        _cast_kernel,
        grid=(N_ALIGNED,),
        in_specs=[pl.BlockSpec((1, D_MODEL, IN_BLOCK), lambda j: (0, 0, j))],
        out_specs=pl.BlockSpec((D_MODEL, IN_BLOCK), lambda j: (0, j)),
        out_shape=jax.ShapeDtypeStruct((D_MODEL, N_ALIGNED * IN_BLOCK), BF16),
        compiler_params=_params(("arbitrary",)),
        name="prep_w_in",
    )(w_in)
    w_b, w_bg = pl.pallas_call(
        _realign_kernel,
        grid=(2,),
        in_specs=[pl.BlockSpec((1, D_MODEL, IN_BLOCK), lambda j: (0, 0, N_ALIGNED + j)),
                  pl.BlockSpec((1, D_MODEL, IN_BLOCK), lambda j: (0, 0, N_ALIGNED + j + 1))],
        out_specs=[pl.BlockSpec((D_MODEL, IN_BLOCK), lambda j: (0, j)),
                   pl.BlockSpec((D_MODEL, LANES), lambda j: (0, 0))],
        out_shape=[jax.ShapeDtypeStruct((D_MODEL, 2 * IN_BLOCK), BF16),
                   jax.ShapeDtypeStruct((D_MODEL, LANES), BF16)],
        compiler_params=_params(("arbitrary",)),
        name="prep_w_gates",
    )(w_in, w_in)
    return w_a, w_b, w_bg


def _inproj_kernel(x_ref, mod_ref, n1_ref, wa_ref, wb_ref, wbg_ref, lnw_ref, lnb_ref, adt_ref,
                   u_ref, vn_ref, qkv_ref, sz_ref, ga_ref, gb_ref, bg_ref):
    tm = u_ref.shape[0]
    hm = tm // IN_SPLIT
    bb, tt, _ = x_ref.shape
    HR = range(IN_SPLIT)
    rows = [slice(h * hm, (h + 1) * hm) for h in HR]
    if bb == 1:
        xsl = [(slice(None), slice(h * (tt // IN_SPLIT), (h + 1) * (tt // IN_SPLIT))) for h in HR]
        msl = [slice(None)] * IN_SPLIT
    else:
        xsl = [(slice(h * (bb // IN_SPLIT), (h + 1) * (bb // IN_SPLIT)), slice(None)) for h in HR]
        msl = [sl[0] for sl in xsl]
    hb = []
    for h in HR:
        x = x_ref[xsl[h][0], xsl[h][1], :]
        y = x * lax.rsqrt(jnp.mean(x * x, axis=-1, keepdims=True) + EPS) * n1_ref[...]
        mod = mod_ref[msl[h]]
        hb.append((y * (1.0 + mod[:, 1:2, :]) + mod[:, 0:1, :]).reshape(hm, D_MODEL).astype(BF16))
    adt = adt_ref[...]
    lane = lax.broadcasted_iota(jnp.int32, (hm, LANES), 1)
    for h in HR:
        zbg = jnp.dot(hb[h], wbg_ref[...], preferred_element_type=F32)
        g = -jnp.exp(adt[0:1, :]) * _softplus(zbg + adt[1:2, :])
        bg_ref[rows[h], :] = jnp.where(lane < DN_HEADS, _sigmoid(zbg), g)

    n_a = wa_ref.shape[1] // IN_BLOCK
    blk = lambda h, j: jnp.dot(hb[h], wa_ref[:, j * IN_BLOCK:(j + 1) * IN_BLOCK] if j < n_a else
                               wb_ref[:, (j - n_a) * IN_BLOCK:(j - n_a + 1) * IN_BLOCK],
                               preferred_element_type=F32)
    for h in HR:
        u_ref[rows[h], :] = _gelu(blk(h, 0)).astype(u_ref.dtype)
    for h in HR:
        a = _gelu(blk(h, 1))
        ac = a - jnp.mean(a, axis=-1, keepdims=True)
        var = jnp.mean(ac * ac, axis=-1, keepdims=True)
        vn_ref[rows[h], :] = (ac * lax.rsqrt(var + EPS) * lnw_ref[...] + lnb_ref[...]).astype(vn_ref.dtype)
    for j in range(3):
        for h in HR:
            qkv_ref[rows[h], j * IN_BLOCK:(j + 1) * IN_BLOCK] = blk(h, 2 + j)
    for h in HR:
        z = blk(h, 5)
        sz_ref[rows[h], :] = (z * _sigmoid(z)).astype(sz_ref.dtype)
    for h in HR:
        ga_ref[rows[h], :] = _sigmoid(blk(h, 6)).astype(ga_ref.dtype)
    for h in HR:
        gb_ref[rows[h], :] = _sigmoid(blk(h, 7)).astype(gb_ref.dtype)


def _inproj(x, mod, norm1, w_a, w_b, w_bg, ln_w, ln_b, adt, bb, tt, vn_dtype):
    B, T, _ = x.shape
    tm = bb * tt
    tpb = T // tt
    n_tiles = (B // bb) * tpb
    M = B * T
    const = lambda i: (0, 0)
    rows = lambda i: (i, 0)
    resident = dict(pipeline_mode=pl.Buffered(1))
    in_specs = [pl.BlockSpec((bb, tt, D_MODEL), lambda i: (i // tpb, i % tpb, 0)),
                pl.BlockSpec((bb, N_MOD, D_MODEL), lambda i: (i // tpb, 0, 0)),
                pl.BlockSpec((1, D_MODEL), const),
                pl.BlockSpec(w_a.shape, const, **resident),
                pl.BlockSpec(w_b.shape, const, **resident),
                pl.BlockSpec((D_MODEL, LANES), const, **resident),
                pl.BlockSpec((1, A_WIDTH), const),
                pl.BlockSpec((1, A_WIDTH), const),
                pl.BlockSpec((2, LANES), const)]
    args = [x, mod, norm1.reshape(1, D_MODEL), w_a, w_b, w_bg, ln_w.reshape(1, A_WIDTH), ln_b.reshape(1, A_WIDTH),
            adt]
    out_specs = [pl.BlockSpec((tm, A_WIDTH), rows),
                 pl.BlockSpec((tm, A_WIDTH), rows),
                 pl.BlockSpec((tm, DN_CONV_CH), rows),
                 pl.BlockSpec((tm, DN_V), rows),
                 pl.BlockSpec((tm, D_MODEL), rows),
                 pl.BlockSpec((tm, D_MODEL), rows),
                 pl.BlockSpec((tm, LANES), rows)]
    out_shape = [jax.ShapeDtypeStruct((M, A_WIDTH), BF16),
                 jax.ShapeDtypeStruct((M, A_WIDTH), vn_dtype),
                 jax.ShapeDtypeStruct((M, DN_CONV_CH), F32),
                 jax.ShapeDtypeStruct((M, DN_V), BF16),
                 jax.ShapeDtypeStruct((M, D_MODEL), BF16),
                 jax.ShapeDtypeStruct((M, D_MODEL), BF16),
                 jax.ShapeDtypeStruct((M, LANES), F32)]
    return pl.pallas_call(
        _inproj_kernel,
        grid=(n_tiles,),
        in_specs=in_specs,
        out_specs=out_specs,
        out_shape=out_shape,
        compiler_params=_params(("arbitrary",)),
        name="in_proj",
    )(*args)


def _inv_unit_lower_minus_eye(lmats, nilpotent):
    ns = [-l for l in lmats]
    ps = list(lmats)
    k = 2
    while k < nilpotent:
        early = k <= _PASSES["inv_kmax"]
        ps = [_mm("invp" if early else "inv_hi", p, p) for p in ps]
        ns = [n + p + _mm("invn" if early else "inv_hi", n, p) for n, p in zip(ns, ps)]
        k *= 2
    return ns


def _head_sumsq(x):
    pair = 2 * DN_KDIM
    r = lax.broadcasted_iota(jnp.int32, (pair, pair), 0) >= DN_KDIM
    c = lax.broadcasted_iota(jnp.int32, (pair, pair), 1) >= DN_KDIM
    ones2 = jnp.where(r == c, 1.0, 0.0).astype(BF16)
    sq = (x * x).astype(BF16)
    return jnp.concatenate([jnp.dot(sq[:, p * pair:(p + 1) * pair], ones2, preferred_element_type=F32)
                            for p in range(x.shape[1] // pair)], axis=1)


def _deltanet_kernel(q_ref, k_ref, v_ref, z_ref, bg_ref, cinit_ref, s0_ref, cw_ref, dnn_ref,
                     bo_ref, sn_ref, xp_s, s_s, *, nprob, nseq, C):
    R = nseq * C
    c = pl.program_id(1)
    nc = pl.num_programs(1)

    @pl.when(c == 0)
    def _():
        xp_s[:, SUBLANES - (CONV_W - 1):SUBLANES, :] = cinit_ref[...]
        s_s[...] = s0_ref[...]

    @pl.when(c > 0)
    def _():
        xp_s[:, 0:SUBLANES, :] = xp_s[:, C:C + SUBLANES, :]

    for p in range(nprob):
        for s in range(nseq):
            i = p * nseq + s
            xp_s[i, SUBLANES:SUBLANES + C, 0:DN_QK] = q_ref[p, s * C:(s + 1) * C, :]
            xp_s[i, SUBLANES:SUBLANES + C, DN_QK:2 * DN_QK] = k_ref[p, s * C:(s + 1) * C, :]
            xp_s[i, SUBLANES:SUBLANES + C, 2 * DN_QK:DN_CONV_CH] = v_ref[p, s * C:(s + 1) * C, :]

    cw = cw_ref[...]
    base = SUBLANES - (CONV_W - 1)
    acc = None
    for j in range(CONV_W):
        term = xp_s[:, base + j:base + j + C, :] * cw[j:j + 1, :]
        acc = term if acc is None else acc + term
    qkv = acc.reshape(nprob * R, DN_CONV_CH)
    qkv = qkv * _sigmoid(qkv)
    q_all = qkv[:, 0:DN_QK]
    k_all = qkv[:, DN_QK:2 * DN_QK]
    v_all = qkv[:, 2 * DN_QK:DN_CONV_CH]
    qn_all = q_all * lax.rsqrt(_head_sumsq(q_all) + EPS) * (DN_KDIM ** -0.5)
    kn_all = k_all * lax.rsqrt(_head_sumsq(k_all) + EPS)

    row = lax.broadcasted_iota(jnp.int32, (R, R), 0)
    col = lax.broadcasted_iota(jnp.int32, (R, R), 1)
    if nseq > 1:
        shift = C.bit_length() - 1
        same = lax.shift_right_logical(row, shift) == lax.shift_right_logical(col, shift)
        tril = same & (col <= row)
        strict = same & (col < row)
    else:
        tril = col <= row
        strict = col < row
    tril_b = jnp.where(tril, 1.0, 0.0).astype(BF16)
    bg = [bg_ref[p] for p in range(nprob)]
    gc = [_dot_exact_lhs(tril_b, bg[p]) for p in range(nprob)]
    gct = [g.T for g in gc]
    dnn = dnn_ref[...]

    IT = [(p, h) for p in range(nprob) for h in range(DN_HEADS)]
    NI = range(len(IT))
    hsl = [slice(h * DN_KDIM, (h + 1) * DN_KDIM) for h in range(DN_HEADS)]
    rows = [slice(p * R, (p + 1) * R) for p in range(nprob)]
    qn = [qn_all[rows[p], hsl[h]] for p, h in IT]
    kn = [kn_all[rows[p], hsl[h]] for p, h in IT]
    vh = [v_all[rows[p], hsl[h]] for p, h in IT]
    beta = [bg[p][:, h:h + 1] for p, h in IT]
    gcol = [gc[p][:, DN_HEADS + h:DN_HEADS + h + 1] for p, h in IT]
    grow = [gct[p][DN_HEADS + h:DN_HEADS + h + 1, :] for p, h in IT]
    decay = [jnp.exp(jnp.minimum(gcol[i] - grow[i], 0.0)) for i in NI]
    eg = [jnp.exp(gcol[i]) for i in NI]
    kb = [kn[i] * beta[i] for i in NI]
    vb = [vh[i] * beta[i] for i in NI]
    a2 = [_mm("kk", jnp.concatenate([kb[i], qn[i]], axis=0), kn[i], _NT) for i in NI]
    lmat = [jnp.where(strict, a2[i][:R] * decay[i], 0.0) for i in NI]
    attn = [jnp.where(tril, a2[i][R:] * decay[i], 0.0) for i in NI]
    tn = _inv_unit_lower_minus_eye(lmat, C)
    rhs = [jnp.concatenate([vb[i], kb[i] * eg[i]], axis=1) for i in NI]
    sol = [rhs[i] + _mm("sol", tn[i], rhs[i]) for i in NI]
    u = [x[:, :DN_VDIM] for x in sol]
    w = [x[:, DN_VDIM:] for x in sol]
    qg = [qn[i] * eg[i] for i in NI]
    vnew = [[None] * nseq for _ in NI]
    qs = [[None] * nseq for _ in NI]
    for s in range(nseq):
        rs = slice(s * C, (s + 1) * C)
        st = [s_s[p * nseq + s, h] for p, h in IT]
        ws = [_mm("ws", jnp.concatenate([w[i][rs], qg[i][rs]], axis=0), st[i]) for i in NI]
        for i in NI:
            vnew[i][s] = u[i][rs] - ws[i][:C]
            qs[i][s] = ws[i][C:]
        glast = [gcol[i][(s + 1) * C - 1:(s + 1) * C, :] for i in NI]
        kd = [kn[i][rs] * jnp.exp(glast[i] - gcol[i][rs]) for i in NI]
        upd = [_mm("state", kd[i], vnew[i][s], _TN) for i in NI]
        for i, (p, h) in enumerate(IT):
            s_s[p * nseq + s, h] = st[i] * jnp.exp(glast[i]) + upd[i]
    cat = lambda parts: parts[0] if nseq == 1 else jnp.concatenate(parts, axis=0)
    o = [cat(qs[i]) + _mm("attn", attn[i], cat(vnew[i])) for i in NI]
    o_all = jnp.concatenate([jnp.concatenate(o[p * DN_HEADS:(p + 1) * DN_HEADS], axis=1) for p in range(nprob)],
                            axis=0)
    dnn_all = jnp.concatenate([dnn] * DN_HEADS, axis=1)
    on = o_all * lax.rsqrt(_head_sumsq(o_all) * (1.0 / DN_VDIM) + EPS) * dnn_all
    for p in range(nprob):
        bo_ref[p] = (on[rows[p]] * z_ref[p].astype(F32)).astype(bo_ref.dtype)

    @pl.when(c == nc - 1)
    def _():
        sn_ref[...] = s_s[...]


def _deltanet(zqkv, sz, bg, conv_init, s0, conv_w, dn_norm, nprob, nseq, C, nc):
    M = zqkv.shape[0]
    R = nseq * C
    G = M // (R * nc)
    view = lambda a: a.reshape(G, R * nc, a.shape[-1])
    rmap = lambda col: (lambda g, c: (g, c, col))
    nsq = nprob * nseq
    state_spec = pl.BlockSpec((nsq, DN_HEADS, DN_KDIM, DN_VDIM), lambda g, c: (g, 0, 0, 0))
    bo, s_new = pl.pallas_call(
        functools.partial(_deltanet_kernel, nprob=nprob, nseq=nseq, C=C),
        grid=(G // nprob, nc),
        in_specs=[pl.BlockSpec((nprob, R, DN_QK), rmap(0)),
                  pl.BlockSpec((nprob, R, DN_QK), rmap(1)),
                  pl.BlockSpec((nprob, R, DN_V), rmap(2)),
                  pl.BlockSpec((nprob, R, DN_V), rmap(0)),
                  pl.BlockSpec((nprob, R, LANES), rmap(0)),
                  pl.BlockSpec((nsq, CONV_W - 1, DN_CONV_CH), lambda g, c: (g, 0, 0)),
                  state_spec,
                  pl.BlockSpec((CONV_W, DN_CONV_CH), lambda g, c: (0, 0)),
                  pl.BlockSpec((1, DN_VDIM), lambda g, c: (0, 0))],
        out_specs=[pl.BlockSpec((nprob, R, DN_V), rmap(0)), state_spec],
        out_shape=[jax.ShapeDtypeStruct((G, R * nc, DN_V), BF16),
                   jax.ShapeDtypeStruct(s0.shape, F32)],
        scratch_shapes=[pltpu.VMEM((nsq, C + SUBLANES, DN_CONV_CH), F32),
                        pltpu.VMEM((nsq, DN_HEADS, DN_KDIM, DN_VDIM), F32)],
        compiler_params=_params(("arbitrary", "arbitrary")),
        name="deltanet",
    )(view(zqkv), view(zqkv), view(zqkv), view(sz), view(bg), conv_init, s0, conv_w, dn_norm.reshape(1, DN_VDIM))
    return bo.reshape(M, DN_V), s_new


def _outproj_kernel(u_ref, vn_ref, ga_ref, gb_ref, bo_ref, x_ref, mod_ref, ws_ref, bs_ref, wpa_ref, wpb_ref,
                    wo_ref, n2_ref, wr_ref, br_ref, before_ref, x1_ref, h2_ref, rt_ref, cnt_ref, cnt_s):
    @pl.when(pl.program_id(0) == 0)
    def _():
        cnt_s[...] = jnp.zeros_like(cnt_s)

    tm = u_ref.shape[0]
    hm = tm // OUT_SPLIT
    bb, tt, _ = x_ref.shape
    HR = range(OUT_SPLIT)
    rows = [slice(h * hm, (h + 1) * hm) for h in HR]
    if bb == 1:
        xsl = [(slice(None), slice(h * (tt // OUT_SPLIT), (h + 1) * (tt // OUT_SPLIT))) for h in HR]
        msl = [slice(None)] * OUT_SPLIT
    else:
        xsl = [(slice(h * (bb // OUT_SPLIT), (h + 1) * (bb // OUT_SPLIT)), slice(None)) for h in HR]
        msl = [s[0] for s in xsl]

    def spatial(h):
        parts = []
        for r in range(hm // GMLP_CHUNK):
            rs = slice(h * hm + r * GMLP_CHUNK, h * hm + (r + 1) * GMLP_CHUNK)
            vn = vn_ref[rs, :].astype(BF16)
            cols = [jnp.dot(ws_ref[g], vn[:, g * A_GDIM:(g + 1) * A_GDIM], preferred_element_type=F32)
                    for g in range(A_GROUPS)]
            parts.append(jnp.concatenate(cols, axis=1) + bs_ref[...])
        return parts[0] if len(parts) == 1 else jnp.concatenate(parts, axis=0)

    sv = [spatial(h) for h in HR]
    a_out = [u_ref[rows[h], :].astype(F32) * sv[h] for h in HR]
    pa = [_dot(a_out[h], wpa_ref[...]) for h in HR]
    pb = [_dot(bo_ref[rows[h], :], wpb_ref[...]) for h in HR]
    m = [ga_ref[rows[h], :].astype(F32) * pa[h] + gb_ref[rows[h], :].astype(F32) * pb[h] for h in HR]
    mix = [_dot(m[h], wo_ref[...]) for h in HR]
    mod = [mod_ref[msl[h]] for h in HR]
    x = [x_ref[xsl[h][0], xsl[h][1], :] for h in HR]
    x1 = [x[h] + mod[h][:, 2:3, :] * mix[h].reshape(x[h].shape) for h in HR]
    for h in HR:
        x1_ref[xsl[h][0], xsl[h][1], :] = x1[h]
    y = [x1[h] * lax.rsqrt(jnp.mean(x1[h] * x1[h], axis=-1, keepdims=True) + EPS) * n2_ref[...] for h in HR]
    h2 = [(y[h] * (1.0 + mod[h][:, 4:5, :]) + mod[h][:, 3:4, :]).reshape(hm, D_MODEL) for h in HR]
    for h in HR:
        for j in range(SUBLANES):
            h2_ref[pl.ds(h * hm * SUBLANES + j, hm, stride=SUBLANES), :] = h2[h][:, j * LANES:(j + 1) * LANES]

    wr = wr_ref[...]
    lane = lax.broadcasted_iota(jnp.int32, (hm, LANES), 1)
    lanef = lane.astype(F32)
    neg = jnp.float32(-jnp.inf)
    big = jnp.float32(1e9)

    def route(h2h):
        hh, hl = _hilo(h2h)
        p = jnp.dot(hh, wr, preferred_element_type=F32)
        logits = (p[:, :LANES] + (p[:, LANES:] + jnp.dot(hl, wr[:, :LANES], preferred_element_type=F32))
                  + br_ref[...])
        gl = jnp.where(lane < N_GROUPS, logits, neg)
        gmax = jnp.max(gl, axis=-1, keepdims=True)
        gsel = jnp.min(jnp.where(gl == gmax, lanef, big), axis=-1, keepdims=True)
        gp = 1.0 / jnp.sum(jnp.exp(gl - gmax), axis=-1, keepdims=True)
        lo = ROUTER_LANE0 + EXP_PER_GROUP * gsel
        in_grp = (lanef >= lo) & (lanef < lo + EXP_PER_GROUP)
        el = jnp.where(in_grp, logits, neg)
        m1 = jnp.max(el, axis=-1, keepdims=True)
        i1 = jnp.min(jnp.where(el == m1, lanef, big), axis=-1, keepdims=True)
        el2 = jnp.where(lanef == i1, neg, el)
        m2 = jnp.max(el2, axis=-1, keepdims=True)
        i2 = jnp.min(jnp.where(el2 == m2, lanef, big), axis=-1, keepdims=True)
        ex = jnp.exp(m2 - m1)
        return i1 - ROUTER_LANE0, i2 - ROUTER_LANE0, gp / (1.0 + ex), gp * ex / (1.0 + ex)

    routed = [route(h2[h]) for h in HR]

    seen0 = cnt_s[...]
    for h in HR:
        e1, e2, w1, w2 = routed[h]
        oh1 = jnp.where(lanef == e1, 1.0, 0.0)
        oh2 = jnp.where(lanef == e2, 1.0, 0.0)
        oh = oh1 + oh2
        seen = jnp.dot(before_ref[...], oh.astype(BF16), preferred_element_type=F32) + seen0
        r1 = jnp.sum(oh1 * seen, axis=-1, keepdims=True)
        r2 = jnp.sum(oh2 * seen, axis=-1, keepdims=True)
        seen0 = seen0 + jnp.sum(oh, axis=0, keepdims=True)
        rec = jnp.zeros((hm, LANES), F32)
        for k, val in enumerate((e1, e2, r1, r2, w1, w2)):
            rec = jnp.where(lane == k, val, rec)
        rt_ref[rows[h], :] = rec
    cnt_s[...] = seen0
    cnt_ref[...] = seen0


def _outproj(u, vn, ga, gb, bo, x, mod, ws_eff, bs_full, w_pa, w_pb, w_o, norm2, w_r, b_r, bb, tt):
    B, T, _ = x.shape
    tm = bb * tt
    tpb = T // tt
    n_tiles = (B // bb) * tpb
    M = B * T
    cmap = lambda col: (lambda i: (i, col))
    full2 = lambda i: (0, 0)
    return pl.pallas_call(
        _outproj_kernel,
        grid=(n_tiles,),
        in_specs=[pl.BlockSpec((tm, A_WIDTH), cmap(0)),
                  pl.BlockSpec((tm, A_WIDTH), cmap(0)),
                  pl.BlockSpec((tm, D_MODEL), cmap(0)),
                  pl.BlockSpec((tm, D_MODEL), cmap(0)),
                  pl.BlockSpec((tm, DN_V), cmap(0)),
                  pl.BlockSpec((bb, tt, D_MODEL), lambda i: (i // tpb, i % tpb, 0)),
                  pl.BlockSpec((bb, N_MOD, D_MODEL), lambda i: (i // tpb, 0, 0)),
                  pl.BlockSpec((A_GROUPS, GMLP_CHUNK, GMLP_CHUNK), lambda i: (0, 0, 0)),
                  pl.BlockSpec((GMLP_CHUNK, A_WIDTH), full2),
                  pl.BlockSpec((A_WIDTH, D_MODEL), full2),
                  pl.BlockSpec((DN_V, D_MODEL), full2),
                  pl.BlockSpec((D_MODEL, D_MODEL), full2),
                  pl.BlockSpec((1, D_MODEL), full2),
                  pl.BlockSpec((D_MODEL, 2 * LANES), full2),
                  pl.BlockSpec((1, LANES), full2),
                  pl.BlockSpec((tm // OUT_SPLIT, tm // OUT_SPLIT), full2)],
        out_specs=[pl.BlockSpec((bb, tt, D_MODEL), lambda i: (i // tpb, i % tpb, 0)),
                   pl.BlockSpec((tm * SUBLANES, LANES), cmap(0)),
                   pl.BlockSpec((tm, LANES), cmap(0)),
                   pl.BlockSpec((1, LANES), full2)],
        out_shape=[jax.ShapeDtypeStruct((B, T, D_MODEL), F32),
                   jax.ShapeDtypeStruct((M * SUBLANES, LANES), F32),
                   jax.ShapeDtypeStruct((M, LANES), F32),
                   jax.ShapeDtypeStruct((1, LANES), F32)],
        scratch_shapes=[pltpu.VMEM((1, LANES), F32)],
        compiler_params=_params(("arbitrary",)),
        name="out_proj",
    )(u, vn, ga, gb, bo, x, mod, ws_eff, bs_full, w_pa, w_pb, w_o, norm2.reshape(1, D_MODEL), w_r, b_r,
      jnp.tril(jnp.ones((tm // OUT_SPLIT, tm // OUT_SPLIT), BF16), -1))


def _tile_copy(src_ref, src_off, dst_ref, dst_off, sem):
    return pltpu.make_async_copy(src_ref.at[pl.ds(pl.multiple_of(src_off, SUBLANES), SUBLANES), :],
                                 dst_ref.at[pl.ds(pl.multiple_of(dst_off, SUBLANES), SUBLANES), :], sem)


def _to_matrix(ref, n):
    return jnp.concatenate([ref[pl.ds(j, n, stride=SUBLANES), :] for j in range(SUBLANES)], axis=1)


def _zero_fill(pad_lo_ref, pad_n_ref, na_ref, xs_ref, zero_s, zsem):
    blk = EXPERT_TILE * SUBLANES
    n_tiles = xs_ref.shape[0] // blk
    zero_s[...] = jnp.zeros_like(zero_s)
    run = lambda off, rows: pltpu.make_async_copy(
        zero_s.at[pl.ds(0, rows * SUBLANES), :],
        xs_ref.at[pl.ds(pl.multiple_of(off * SUBLANES, SUBLANES), rows * SUBLANES), :], zsem)
    sizes = [1 << b for b in reversed(range(EXPERT_TILE.bit_length() - 1))]
    for wait in (False, True):
        for e in range(N_EXPERTS):
            off = pad_lo_ref[e]
            for rows in sizes:
                bit = pad_n_ref[e] & rows

                @pl.when(bit != 0)
                def _(off=off, rows=rows):
                    run(0, rows).wait() if wait else run(off, rows).start()

                off = off + bit

        def idle(i, carry):
            run(0, EXPERT_TILE).wait() if wait else run(i * EXPERT_TILE, EXPERT_TILE).start()
            return carry

        lax.fori_loop(na_ref[0], n_tiles, idle, 0)


def _dispatch_kernel(pad_lo_ref, pad_n_ref, na_ref, pos_ref, *refs, first_tile):
    n_paths = len(first_tile) - 1
    h_refs = refs[:n_paths]
    xs_ref, zero_s, sem, zsem = refs[n_paths:]
    i = pl.program_id(0)

    @pl.when(i == 0)
    def _():
        _zero_fill(pad_lo_ref, pad_n_ref, na_ref, xs_ref, zero_s, zsem)

    tm = h_refs[0].shape[0] // SUBLANES
    for p, h_ref in enumerate(h_refs):
        @pl.when((i >= first_tile[p]) & (i < first_tile[p + 1]))
        def _(h_ref=h_ref):
            def issue(t, carry):
                for k in range(2):
                    _tile_copy(h_ref, t * SUBLANES, xs_ref, pos_ref[0, k, t], sem).start(priority=k)
                return carry

            lax.fori_loop(0, tm, issue, 0, unroll=8)

            def drain(t, carry):
                for k in range(2):
                    _tile_copy(h_ref, 0, xs_ref, 0, sem).wait()
                return carry

            lax.fori_loop(0, tm, drain, 0, unroll=8)


def _dispatch(h2s, pos, tm, n_rows, pad_lo, pad_n, n_active):
    blk = tm * SUBLANES
    tiles = [h.shape[0] // blk for h in h2s]
    first_tile = [sum(tiles[:p]) for p in range(len(tiles) + 1)]
    hmap = lambda p: (lambda i, *_: (jnp.clip(i - first_tile[p], 0, tiles[p] - 1), 0))
    grid_spec = pltpu.PrefetchScalarGridSpec(
        num_scalar_prefetch=3,
        grid=(first_tile[-1],),
        in_specs=[pl.BlockSpec((1, 2, tm), lambda i, *_: (i, 0, 0), memory_space=pltpu.SMEM)]
                 + [pl.BlockSpec((blk, LANES), hmap(p)) for p in range(len(tiles))],
        out_specs=pl.BlockSpec(memory_space=pl.ANY),
        scratch_shapes=[pltpu.VMEM((EXPERT_TILE * SUBLANES, LANES), F32),
                        pltpu.SemaphoreType.DMA(()), pltpu.SemaphoreType.DMA(())],
    )
    return pl.pallas_call(
        functools.partial(_dispatch_kernel, first_tile=tuple(first_tile)),
        grid_spec=grid_spec,
        out_shape=jax.ShapeDtypeStruct((n_rows * SUBLANES, LANES), F32),
        compiler_params=_params(("arbitrary",)),
        name="moe_dispatch",
    )(pad_lo, pad_n, n_active, pos, *h2s)


def _expert_kernel(te_ref, na_ref, x_ref, *refs):
    del te_ref
    w_refs, o_ref = refs[:-1], refs[-1]
    blk = EXPERT_TILE * SUBLANES
    live = EXPERTS_PER_STEP * pl.program_id(0) < na_ref[0]

    @pl.when(live)
    def _():
        TR = range(EXPERTS_PER_STEP)
        wg, wu, wd = ([w_refs[3 * t + k][0] for t in TR] for k in range(3))
        x = [_to_matrix(x_ref.at[pl.ds(t * blk, blk), :], EXPERT_TILE).astype(BF16) for t in TR]
        a = [_dot(x[t], wg[t]) for t in TR]
        u = [_dot(x[t], wu[t]) for t in TR]
        o = [_dot(a[t] * _sigmoid(a[t]) * u[t], wd[t]) for t in TR]
        for t in TR:
            for j in range(SUBLANES):
                o_ref[pl.ds(t * blk + j, EXPERT_TILE, stride=SUBLANES), :] = o[t][:, j * LANES:(j + 1) * LANES]

    @pl.when(jnp.logical_not(live))
    def _():
        o_ref[...] = jnp.zeros_like(o_ref)


def _experts(xs, tile_expert, n_active, wg, wu, wd):
    blk = EXPERT_TILE * SUBLANES * EXPERTS_PER_STEP
    n_steps = xs.shape[0] // blk
    last = lambda na: jnp.maximum(na[0] - 1, 0)
    xmap = lambda i, te, na: (jnp.minimum(i, last(na) // EXPERTS_PER_STEP), 0)
    wmap = lambda t: (lambda i, te, na: (te[jnp.minimum(EXPERTS_PER_STEP * i + t, last(na))], 0, 0))
    w_specs = []
    for t in range(EXPERTS_PER_STEP):
        w_specs += [pl.BlockSpec((1, D_MODEL, D_EXPERT), wmap(t)),
                    pl.BlockSpec((1, D_MODEL, D_EXPERT), wmap(t)),
                    pl.BlockSpec((1, D_EXPERT, D_MODEL), wmap(t))]
    grid_spec = pltpu.PrefetchScalarGridSpec(
        num_scalar_prefetch=2,
        grid=(n_steps,),
        in_specs=[pl.BlockSpec((blk, LANES), xmap)] + w_specs,
        out_specs=pl.BlockSpec((blk, LANES), lambda i, te, na: (i, 0)),
    )
    return pl.pallas_call(
        _expert_kernel,
        grid_spec=grid_spec,
        out_shape=jax.ShapeDtypeStruct(xs.shape, F32),
        compiler_params=_params(("arbitrary",)),
        name="moe_experts",
    )(tile_expert, n_active, xs, *([wg, wu, wd] * EXPERTS_PER_STEP))


def _combine_kernel(pos_ref, pos_next_ref, rt_ref, x1_ref, mod_ref, fn_ref, ys_ref, y_ref, g_s, sems):
    tm = rt_ref.shape[0]
    i = pl.program_id(0)
    n = pl.num_programs(0)
    slot = i % 2

    def gather(p_ref, s):
        def issue(t, carry):
            for k in range(2):
                _tile_copy(ys_ref, p_ref[0, k, t], g_s.at[s, k], t * SUBLANES, sems.at[s]).start(priority=k)
            return carry

        lax.fori_loop(0, tm, issue, 0, unroll=8)

    @pl.when(i == 0)
    def _():
        gather(pos_ref, slot)

    @pl.when(i + 1 < n)
    def _():
        gather(pos_next_ref, 1 - slot)

    def drain(t, carry):
        for k in range(2):
            _tile_copy(ys_ref, 0, g_s.at[slot, k], 0, sems.at[slot]).wait()
        return carry

    lax.fori_loop(0, tm, drain, 0, unroll=8)

    rt = rt_ref[...]
    moe = rt[:, 4:5] * _to_matrix(g_s.at[slot, 0], tm) + rt[:, 5:6] * _to_matrix(g_s.at[slot, 1], tm)
    mod = mod_ref[...]
    x1 = x1_ref[...]
    x2 = x1 + mod[:, 5:6, :] * moe.reshape(x1.shape)
    y_ref[...] = x2 * lax.rsqrt(jnp.mean(x2 * x2, axis=-1, keepdims=True) + EPS) * fn_ref[...]


def _combine(ys, pos, route, x1, mod, final_norm, bb, tt):
    B, T, _ = x1.shape
    tm = bb * tt
    tpb = T // tt
    n_tiles = (B // bb) * tpb
    xmap = lambda i: (i // tpb, i % tpb, 0)
    return pl.pallas_call(
        _combine_kernel,
        grid=(n_tiles,),
        in_specs=[pl.BlockSpec((1, 2, tm), lambda i: (i, 0, 0), memory_space=pltpu.SMEM),
                  pl.BlockSpec((1, 2, tm), lambda i: (jnp.minimum(i + 1, n_tiles - 1), 0, 0),
                               memory_space=pltpu.SMEM),
                  pl.BlockSpec((tm, LANES), lambda i: (i, 0)),
                  pl.BlockSpec((bb, tt, D_MODEL), xmap),
                  pl.BlockSpec((bb, N_MOD, D_MODEL), lambda i: (i // tpb, 0, 0)),
                  pl.BlockSpec((1, D_MODEL), lambda i: (0, 0)),
                  pl.BlockSpec(memory_space=pl.ANY)],
        out_specs=pl.BlockSpec((bb, tt, D_MODEL), xmap),
        out_shape=jax.ShapeDtypeStruct((B, T, D_MODEL), F32),
        scratch_shapes=[pltpu.VMEM((2, 2, tm * SUBLANES, LANES), F32), pltpu.SemaphoreType.DMA((2,))],
        compiler_params=_params(("arbitrary",)),
        name="moe_combine",
    )(pos, pos, route, x1, mod, final_norm.reshape(1, D_MODEL), ys)


def _moe(paths, wg, wu, wd, final_norm):
    n_assign = sum(2 * p["route"].shape[0] for p in paths)
    max_tiles = (n_assign + N_EXPERTS * (EXPERT_TILE - 1)) // EXPERT_TILE
    max_tiles += -max_tiles % EXPERTS_PER_STEP
    cnts = [p["counts"][0, :N_EXPERTS].astype(jnp.int32) for p in paths]
    cnt = sum(cnts)
    nt = (cnt + EXPERT_TILE - 1) // EXPERT_TILE
    ends = jnp.cumsum(nt)
    base = ((ends - nt) * EXPERT_TILE).astype(jnp.int32)
    n_active = ends[-1:].astype(jnp.int32)
    tile_expert = jnp.minimum(jnp.sum(jnp.arange(max_tiles)[:, None] >= ends[None, :], axis=1),
                              N_EXPERTS - 1).astype(jnp.int32)
    tm = paths[0]["bb"] * paths[0]["tt"]
    assert all(p["bb"] * p["tt"] == tm for p in paths)
    start = base
    for p, c in zip(paths, cnts):
        eid = p["route"][:, 0:2].astype(jnp.int32)
        rank = p["route"][:, 2:4].astype(jnp.int32)
        first_row = jnp.sum(jnp.where(eid[..., None] == jnp.arange(N_EXPERTS), start, 0), axis=-1)
        pos = (first_row + rank) * SUBLANES
        p["pos"] = jnp.transpose(pos.reshape(-1, tm, 2), (0, 2, 1))
        start = start + c
    xs = _dispatch([p["h2"] for p in paths], jnp.concatenate([p["pos"] for p in paths], axis=0), tm,
                   max_tiles * EXPERT_TILE, base + cnt, nt * EXPERT_TILE - cnt, n_active)
    ys = _experts(xs, tile_expert, n_active, wg, wu, wd)
    return [_combine(ys, p["pos"], p["route"], p["x1"], p["mod"], final_norm, p["bb"], p["tt"]) for p in paths]


def _mixers(x, mod, conv_init, s0, prm, *, bb, tt, dn_nprob, dn_nseq, dn_chunk, vn_dtype):
    B, T, _ = x.shape
    u, vn, zqkv, sz, ga, gb, bg = _inproj(x, mod, prm["norm1"], prm["w_a"], prm["w_b"], prm["w_bg"], prm["ln_v_w"],
                                          prm["ln_v_b"], prm["adt"], bb, tt, vn_dtype)
    nc = T // dn_chunk
    bo, s_new = _deltanet(zqkv, sz, bg, conv_init, s0, prm["conv_w"], prm["dn_norm"], dn_nprob, dn_nseq, dn_chunk, nc)
    L = min(GMLP_CHUNK, T)
    reps = GMLP_CHUNK // L
    wm = prm["w_s_tril"][:, :L, :L]
    if reps > 1:
        blk = (jnp.arange(GMLP_CHUNK)[:, None] // L) == (jnp.arange(GMLP_CHUNK)[None, :] // L)
        wm = jnp.where(blk[None], jnp.tile(wm, (1, reps, reps)), 0.0)
    ws_eff = wm.astype(BF16)
    bs_rows = jnp.tile(jnp.transpose(prm["b_s"][:, :L]), (reps, 1))
    bs_full = jnp.repeat(bs_rows, A_GDIM, axis=1)
    x1, h2, route, counts = _outproj(u, vn, ga, gb, bo, x, mod, ws_eff, bs_full, prm["w_pa"], prm["w_pb"],
                                     prm["w_o"], prm["norm2"], prm["w_r"], prm["b_r"], bb, tt)
    tail = min(T, CONV_W - 1)
    zqkv_tail = zqkv.reshape(B, T, DN_CONV_CH)[:, T - tail:, :]
    conv_new = jnp.concatenate([conv_init, zqkv_tail], axis=1)[:, -(CONV_W - 1):, :]
    path = dict(h2=h2, route=route, counts=counts, x1=x1, mod=mod, bb=bb, tt=tt)
    return path, conv_new, s_new, vn.reshape(B, T, A_WIDTH)


def kernel(x_prompt, x_sample, c_prompt, c_sample, state_conv, state_delta, w_ada, b_ada, norm1, norm2, w_in, conv_w, a_log, dt_bias, dn_norm, ln_v_w, ln_v_b, w_s, b_s, w_pa, w_pb, w_o, w_rg, b_rg, w_re, b_re, w_e_gate, w_e_up, w_e_down, final_norm):
    depth = w_ada.shape[0]
    assert depth == 1
    bp, tp, _ = x_prompt.shape
    bs_, ts, _ = x_sample.shape
    l = 0
    assert w_in.shape[-1] == (N_ALIGNED + 2) * IN_BLOCK + 2 * DN_HEADS
    w_a, w_b, w_bg = _prep_in_weights(w_in[l:l + 1])
    adt = jnp.stack([jnp.pad(a_log[l], (DN_HEADS, LANES - 2 * DN_HEADS)),
                     jnp.pad(dt_bias[l], (DN_HEADS, LANES - 2 * DN_HEADS))])
    w_r = jnp.pad(jnp.concatenate([w_rg[l], w_re[l]], axis=1), ((0, 0), (0, LANES - N_GROUPS - N_EXPERTS)))
    w_r_hi = w_r.astype(BF16)
    w_r = jnp.concatenate([w_r_hi, (w_r - w_r_hi.astype(F32)).astype(BF16)], axis=1)
    b_r = jnp.pad(jnp.concatenate([b_rg[l], b_re[l]]), (0, LANES - N_GROUPS - N_EXPERTS)).reshape(1, LANES)
    tri = jnp.tril(jnp.ones((GMLP_CHUNK, GMLP_CHUNK), F32))
    prm = dict(
        norm1=norm1[l], norm2=norm2[l], w_a=w_a, w_b=w_b, w_bg=w_bg, adt=adt, ln_v_w=ln_v_w[l], ln_v_b=ln_v_b[l],
        conv_w=conv_w[l], dn_norm=dn_norm[l], w_s_tril=w_s[l] * tri, b_s=b_s[l],
        w_pa=w_pa[l].astype(BF16), w_pb=w_pb[l].astype(BF16), w_o=w_o[l].astype(BF16), w_r=w_r, b_r=b_r,
        w_e_gate=w_e_gate[l].reshape(N_EXPERTS, D_MODEL, D_EXPERT),
        w_e_up=w_e_up[l].reshape(N_EXPERTS, D_MODEL, D_EXPERT),
        w_e_down=w_e_down[l].reshape(N_EXPERTS, D_EXPERT, D_MODEL),
    )

    c_all = jnp.concatenate([c_prompt, c_sample], axis=0)
    mod = _ada(c_all, w_ada[l], b_ada[l]).reshape(bp + bs_, N_MOD, D_MODEL)
    mod_p, mod_s = mod[:bp], mod[bp:]

    cinit_p = jnp.zeros((bp, CONV_W - 1, DN_CONV_CH), F32)
    cinit_s = state_conv[l]
    s0_p = jnp.zeros((bp, DN_HEADS, DN_KDIM, DN_VDIM), F32)

    path_p, conv_p, delta_p, _ = _mixers(x_prompt, mod_p, cinit_p, s0_p, prm,
                                         bb=1, tt=min(ROW_TILE, tp), dn_nprob=4 if bp % 4 == 0 else 1, dn_nseq=1,
                                         dn_chunk=min(DN_CHUNK, tp), vn_dtype=BF16)
    path_s, conv_s, delta_s, vn_s = _mixers(x_sample, mod_s, cinit_s, state_delta[l], prm,
                                            bb=min(bs_, ROW_TILE // ts), tt=ts,
                                            dn_nprob=2 if bs_ % (2 * DN_CHUNK // ts) == 0 else 1,
                                            dn_nseq=min(bs_, DN_CHUNK // ts), dn_chunk=ts, vn_dtype=F32)
    yp, ys = _moe([path_p, path_s], prm["w_e_gate"], prm["w_e_up"], prm["w_e_down"], final_norm)
    return (yp, ys, conv_p[None], conv_s[None], delta_p[None], delta_s[None], vn_s[None])
```

```python
import functools

import jax
import jax.numpy as jnp
from jax import lax
from jax.experimental import pallas as pl
from jax.experimental.pallas import tpu as pltpu

F32 = jnp.float32
BF16 = jnp.bfloat16

D_MODEL = 1024
A_WIDTH = 1024
A_GROUPS = 4
A_GDIM = A_WIDTH // A_GROUPS
GMLP_CHUNK = 128
DN_HEADS = 8
DN_KDIM = 128
DN_VDIM = 128
DN_QK = DN_HEADS * DN_KDIM
DN_V = DN_HEADS * DN_VDIM
DN_CONV_CH = 2 * DN_QK + DN_V
CONV_W = 4
DN_CHUNK = 64
N_GROUPS = 4
EXP_PER_GROUP = 8
N_EXPERTS = N_GROUPS * EXP_PER_GROUP
D_EXPERT = 256
N_MOD = 6
EPS = 1e-6

LANES = 128
SUBLANES = 8
ROUTER_LANE0 = N_GROUPS
VMEM_LIMIT = 56 * 1024 * 1024
ROW_TILE = 512
IN_SPLIT = 2
OUT_SPLIT = 2
EXPERT_TILE = 512
EXPERTS_PER_STEP = 1


def _sigmoid(x):
    return 1.0 / (1.0 + jnp.exp(-x))


def _gelu(x):
    return 0.5 * x * (1.0 + lax.erf(x * 0.7071067811865476))


def _softplus(x):
    return jnp.maximum(x, 0.0) + jnp.log(1.0 + jnp.exp(-jnp.abs(x)))


def _dot(a, b):
    return jnp.dot(a.astype(BF16), b.astype(BF16), preferred_element_type=F32)


def _split3(x):
    hi = x.astype(BF16)
    r1 = x - hi.astype(F32)
    mid = r1.astype(BF16)
    lo = (r1 - mid.astype(F32)).astype(BF16)
    return hi, mid, lo


def _dot_exact_lhs(a_bf16, x):
    hi, mid, lo = _split3(x)
    f = lambda p: jnp.dot(a_bf16, p, preferred_element_type=F32)
    return f(hi) + f(mid) + f(lo)


_PASSES = {"invp": 3, "invn": 3, "inv_hi": 1, "inv_kmax": 8, "sol": 1, "kk": 1, "ws": 1, "attn": 1, "state": 1}


def _hilo(x):
    hi = x.astype(BF16)
    lo = (x - hi.astype(F32)).astype(BF16)
    return hi, lo


def _mm(site, a, b, dims=(((1,), (0,)), ((), ()))):
    f = lambda p, q: lax.dot_general(p, q, dims, preferred_element_type=F32)
    if _PASSES[site] == 1:
        return f(a.astype(BF16), b.astype(BF16))
    ah, al = _hilo(a)
    bh, bl = _hilo(b)
    return f(ah, bh) + (f(ah, bl) + f(al, bh))


_NT = (((1,), (1,)), ((), ()))
_TN = (((0,), (0,)), ((), ()))

def _params(sem):
    return pltpu.CompilerParams(dimension_semantics=sem, vmem_limit_bytes=VMEM_LIMIT)


def _ada_kernel(c_ref, w_ref, b_ref, o_ref):
    c = c_ref[...]
    o_ref[...] = _dot(c * _sigmoid(c), w_ref[...]) + b_ref[...]


def _ada(c_all, w_ada, b_ada):
    n = c_all.shape[0]
    width = w_ada.shape[1]
    bn = 512
    return pl.pallas_call(
        _ada_kernel,
        grid=(width // bn,),
        in_specs=[pl.BlockSpec((n, D_MODEL), lambda j: (0, 0)),
                  pl.BlockSpec((D_MODEL, bn), lambda j: (0, j)),
                  pl.BlockSpec((1, bn), lambda j: (0, j))],
        out_specs=pl.BlockSpec((n, bn), lambda j: (0, j)),
        out_shape=jax.ShapeDtypeStruct((n, width), F32),
        compiler_params=_params(("arbitrary",)),
        name="ada_mod",
    )(c_all, w_ada, b_ada.reshape(1, width))


IN_BLOCK = 1024
N_ALIGNED = 6


def _realign_kernel(a_ref, b_ref, o_ref, bg_ref):
    s = 2 * DN_HEADS
    a = a_ref[0]
    o_ref[...] = jnp.concatenate([a[:, s:], b_ref[0, :, :s]], axis=1).astype(o_ref.dtype)

    @pl.when(pl.program_id(0) == 0)
    def _():
        lane = lax.broadcasted_iota(jnp.int32, (D_MODEL, LANES), 1)
        bg_ref[...] = jnp.where(lane < s, a[:, :LANES], 0.0).astype(bg_ref.dtype)


def _prep_in_weights(w_in):
    w_b, w_bg = pl.pallas_call(
        _realign_kernel,
        grid=(2,),
        in_specs=[pl.BlockSpec((1, D_MODEL, IN_BLOCK), lambda j: (0, 0, N_ALIGNED + j)),
                  pl.BlockSpec((1, D_MODEL, IN_BLOCK), lambda j: (0, 0, N_ALIGNED + j + 1))],
        out_specs=[pl.BlockSpec((D_MODEL, IN_BLOCK), lambda j: (0, j)),
                   pl.BlockSpec((D_MODEL, LANES), lambda j: (0, 0))],
        out_shape=[jax.ShapeDtypeStruct((D_MODEL, 2 * IN_BLOCK), BF16),
                   jax.ShapeDtypeStruct((D_MODEL, LANES), BF16)],
        compiler_params=_params(("arbitrary",)),
        name="prep_w_gates",
    )(w_in, w_in)
    return w_b, w_bg


def _inproj_kernel(x_ref, mod_ref, n1_ref, wa_ref, wb_ref, wbg_ref, lnw_ref, lnb_ref, adt_ref,
                   u_ref, vn_ref, qkv_ref, sz_ref, ga_ref, gb_ref, bg_ref):
    tm = u_ref.shape[0]
    hm = tm // IN_SPLIT
    bb, tt, _ = x_ref.shape
    HR = range(IN_SPLIT)
    rows = [slice(h * hm, (h + 1) * hm) for h in HR]
    if bb == 1:
        xsl = [(slice(None), slice(h * (tt // IN_SPLIT), (h + 1) * (tt // IN_SPLIT))) for h in HR]
        msl = [slice(None)] * IN_SPLIT
    else:
        xsl = [(slice(h * (bb // IN_SPLIT), (h + 1) * (bb // IN_SPLIT)), slice(None)) for h in HR]
        msl = [sl[0] for sl in xsl]
    hb = []
    for h in HR:
        x = x_ref[xsl[h][0], xsl[h][1], :]
        y = x * lax.rsqrt(jnp.mean(x * x, axis=-1, keepdims=True) + EPS) * n1_ref[...]
        mod = mod_ref[msl[h]]
        hb.append((y * (1.0 + mod[:, 1:2, :]) + mod[:, 0:1, :]).reshape(hm, D_MODEL).astype(BF16))
    adt = adt_ref[...]
    lane = lax.broadcasted_iota(jnp.int32, (hm, LANES), 1)
    for h in HR:
        zbg = jnp.dot(hb[h], wbg_ref[...], preferred_element_type=F32)
        g = -jnp.exp(adt[0:1, :]) * _softplus(zbg + adt[1:2, :])
        bg_ref[rows[h], :] = jnp.where(lane < DN_HEADS, _sigmoid(zbg), g)

    n_a = wa_ref.shape[2] // IN_BLOCK
    blk = lambda h, j: jnp.dot(hb[h], wa_ref[0, :, j * IN_BLOCK:(j + 1) * IN_BLOCK] if j < n_a else
                               wb_ref[:, (j - n_a) * IN_BLOCK:(j - n_a + 1) * IN_BLOCK],
                               preferred_element_type=F32)
    for h in HR:
        u_ref[rows[h], :] = _gelu(blk(h, 0)).astype(u_ref.dtype)
    for h in HR:
        a = _gelu(blk(h, 1))
        ac = a - jnp.mean(a, axis=-1, keepdims=True)
        var = jnp.mean(ac * ac, axis=-1, keepdims=True)
        vn_ref[rows[h], :] = (ac * lax.rsqrt(var + EPS) * lnw_ref[...] + lnb_ref[...]).astype(vn_ref.dtype)
    for j in range(3):
        for h in HR:
            qkv_ref[rows[h], j * IN_BLOCK:(j + 1) * IN_BLOCK] = blk(h, 2 + j)
    for h in HR:
        z = blk(h, 5)
        sz_ref[rows[h], :] = (z * _sigmoid(z)).astype(sz_ref.dtype)
    for h in HR:
        ga_ref[rows[h], :] = _sigmoid(blk(h, 6)).astype(ga_ref.dtype)
    for h in HR:
        gb_ref[rows[h], :] = _sigmoid(blk(h, 7)).astype(gb_ref.dtype)


def _inproj(x, mod, norm1, w_a, w_b, w_bg, ln_w, ln_b, adt, bb, tt, vn_dtype):
    B, T, _ = x.shape
    tm = bb * tt
    tpb = T // tt
    n_tiles = (B // bb) * tpb
    M = B * T
    const = lambda i: (0, 0)
    rows = lambda i: (i, 0)
    resident = dict(pipeline_mode=pl.Buffered(1))
    in_specs = [pl.BlockSpec((bb, tt, D_MODEL), lambda i: (i // tpb, i % tpb, 0)),
                pl.BlockSpec((bb, N_MOD, D_MODEL), lambda i: (i // tpb, 0, 0)),
                pl.BlockSpec((1, D_MODEL), const),
                pl.BlockSpec((1, D_MODEL, N_ALIGNED * IN_BLOCK), lambda i: (0, 0, 0), **resident),
                pl.BlockSpec(w_b.shape, const, **resident),
                pl.BlockSpec((D_MODEL, LANES), const, **resident),
                pl.BlockSpec((1, A_WIDTH), const),
                pl.BlockSpec((1, A_WIDTH), const),
                pl.BlockSpec((2, LANES), const)]
    args = [x, mod, norm1.reshape(1, D_MODEL), w_a, w_b, w_bg, ln_w.reshape(1, A_WIDTH), ln_b.reshape(1, A_WIDTH),
            adt]
    out_specs = [pl.BlockSpec((tm, A_WIDTH), rows),
                 pl.BlockSpec((tm, A_WIDTH), rows),
                 pl.BlockSpec((tm, DN_CONV_CH), rows),
                 pl.BlockSpec((tm, DN_V), rows),
                 pl.BlockSpec((tm, D_MODEL), rows),
                 pl.BlockSpec((tm, D_MODEL), rows),
                 pl.BlockSpec((tm, LANES), rows)]
    out_shape = [jax.ShapeDtypeStruct((M, A_WIDTH), BF16),
                 jax.ShapeDtypeStruct((M, A_WIDTH), vn_dtype),
                 jax.ShapeDtypeStruct((M, DN_CONV_CH), F32),
                 jax.ShapeDtypeStruct((M, DN_V), BF16),
                 jax.ShapeDtypeStruct((M, D_MODEL), BF16),
                 jax.ShapeDtypeStruct((M, D_MODEL), BF16),
                 jax.ShapeDtypeStruct((M, LANES), F32)]
    return pl.pallas_call(
        _inproj_kernel,
        grid=(n_tiles,),
        in_specs=in_specs,
        out_specs=out_specs,
        out_shape=out_shape,
        compiler_params=_params(("arbitrary",)),
        name="in_proj",
    )(*args)


def _inv_unit_lower_minus_eye(lmats, nilpotent):
    ns = [-l for l in lmats]
    ps = list(lmats)
    k = 2
    while k < nilpotent:
        early = k <= _PASSES["inv_kmax"]
        ps = [_mm("invp" if early else "inv_hi", p, p) for p in ps]
        ns = [n + p + _mm("invn" if early else "inv_hi", n, p) for n, p in zip(ns, ps)]
        k *= 2
    return ns


def _head_sumsq(x):
    pair = 2 * DN_KDIM
    r = lax.broadcasted_iota(jnp.int32, (pair, pair), 0) >= DN_KDIM
    c = lax.broadcasted_iota(jnp.int32, (pair, pair), 1) >= DN_KDIM
    ones2 = jnp.where(r == c, 1.0, 0.0).astype(BF16)
    sq = (x * x).astype(BF16)
    return jnp.concatenate([jnp.dot(sq[:, p * pair:(p + 1) * pair], ones2, preferred_element_type=F32)
                            for p in range(x.shape[1] // pair)], axis=1)


def _deltanet_kernel(q_ref, k_ref, v_ref, z_ref, bg_ref, cinit_ref, s0_ref, cw_ref, dnn_ref,
                     bo_ref, sn_ref, xp_s, s_s, *, nprob, nseq, C):
    R = nseq * C
    c = pl.program_id(1)
    nc = pl.num_programs(1)

    @pl.when(c == 0)
    def _():
        xp_s[:, SUBLANES - (CONV_W - 1):SUBLANES, :] = cinit_ref[...]
        s_s[...] = s0_ref[...]

    @pl.when(c > 0)
    def _():
        xp_s[:, 0:SUBLANES, :] = xp_s[:, C:C + SUBLANES, :]

    for p in range(nprob):
        for s in range(nseq):
            i = p * nseq + s
            xp_s[i, SUBLANES:SUBLANES + C, 0:DN_QK] = q_ref[p, s * C:(s + 1) * C, :]
            xp_s[i, SUBLANES:SUBLANES + C, DN_QK:2 * DN_QK] = k_ref[p, s * C:(s + 1) * C, :]
            xp_s[i, SUBLANES:SUBLANES + C, 2 * DN_QK:DN_CONV_CH] = v_ref[p, s * C:(s + 1) * C, :]

    cw = cw_ref[...]
    base = SUBLANES - (CONV_W - 1)
    acc = None
    for j in range(CONV_W):
        term = xp_s[:, base + j:base + j + C, :] * cw[j:j + 1, :]
        acc = term if acc is None else acc + term
    qkv = acc.reshape(nprob * R, DN_CONV_CH)
    qkv = qkv * _sigmoid(qkv)
    q_all = qkv[:, 0:DN_QK]
    k_all = qkv[:, DN_QK:2 * DN_QK]
    v_all = qkv[:, 2 * DN_QK:DN_CONV_CH]
    qn_all = q_all * lax.rsqrt(_head_sumsq(q_all) + EPS) * (DN_KDIM ** -0.5)
    kn_all = k_all * lax.rsqrt(_head_sumsq(k_all) + EPS)

    row = lax.broadcasted_iota(jnp.int32, (R, R), 0)
    col = lax.broadcasted_iota(jnp.int32, (R, R), 1)
    if nseq > 1:
        shift = C.bit_length() - 1
        same = lax.shift_right_logical(row, shift) == lax.shift_right_logical(col, shift)
        tril = same & (col <= row)
        strict = same & (col < row)
    else:
        tril = col <= row
        strict = col < row
    tril_b = jnp.where(tril, 1.0, 0.0).astype(BF16)
    bg = [bg_ref[p] for p in range(nprob)]
    gc = [_dot_exact_lhs(tril_b, bg[p]) for p in range(nprob)]
    gct = [g.T for g in gc]
    dnn = dnn_ref[...]

    IT = [(p, h) for p in range(nprob) for h in range(DN_HEADS)]
    NI = range(len(IT))
    hsl = [slice(h * DN_KDIM, (h + 1) * DN_KDIM) for h in range(DN_HEADS)]
    rows = [slice(p * R, (p + 1) * R) for p in range(nprob)]
    qn = [qn_all[rows[p], hsl[h]] for p, h in IT]
    kn = [kn_all[rows[p], hsl[h]] for p, h in IT]
    vh = [v_all[rows[p], hsl[h]] for p, h in IT]
    beta = [bg[p][:, h:h + 1] for p, h in IT]
    gcol = [gc[p][:, DN_HEADS + h:DN_HEADS + h + 1] for p, h in IT]
    grow = [gct[p][DN_HEADS + h:DN_HEADS + h + 1, :] for p, h in IT]
    decay = [jnp.exp(jnp.minimum(gcol[i] - grow[i], 0.0)) for i in NI]
    eg = [jnp.exp(gcol[i]) for i in NI]
    kb = [kn[i] * beta[i] for i in NI]
    vb = [vh[i] * beta[i] for i in NI]
    a2 = [_mm("kk", jnp.concatenate([kb[i], qn[i]], axis=0), kn[i], _NT) for i in NI]
    lmat = [jnp.where(strict, a2[i][:R] * decay[i], 0.0) for i in NI]
    attn = [jnp.where(tril, a2[i][R:] * decay[i], 0.0) for i in NI]
    tn = _inv_unit_lower_minus_eye(lmat, C)
    rhs = [jnp.concatenate([vb[i], kb[i] * eg[i]], axis=1) for i in NI]
    sol = [rhs[i] + _mm("sol", tn[i], rhs[i]) for i in NI]
    u = [x[:, :DN_VDIM] for x in sol]
    w = [x[:, DN_VDIM:] for x in sol]
    qg = [qn[i] * eg[i] for i in NI]
    vnew = [[None] * nseq for _ in NI]
    qs = [[None] * nseq for _ in NI]
    for s in range(nseq):
        rs = slice(s * C, (s + 1) * C)
        st = [s_s[p * nseq + s, h] for p, h in IT]
        ws = [_mm("ws", jnp.concatenate([w[i][rs], qg[i][rs]], axis=0), st[i]) for i in NI]
        for i in NI:
            vnew[i][s] = u[i][rs] - ws[i][:C]
            qs[i][s] = ws[i][C:]
        glast = [gcol[i][(s + 1) * C - 1:(s + 1) * C, :] for i in NI]
        kd = [kn[i][rs] * jnp.exp(glast[i] - gcol[i][rs]) for i in NI]
        upd = [_mm("state", kd[i], vnew[i][s], _TN) for i in NI]
        for i, (p, h) in enumerate(IT):
            s_s[p * nseq + s, h] = st[i] * jnp.exp(glast[i]) + upd[i]
    cat = lambda parts: parts[0] if nseq == 1 else jnp.concatenate(parts, axis=0)
    o = [cat(qs[i]) + _mm("attn", attn[i], cat(vnew[i])) for i in NI]
    o_all = jnp.concatenate([jnp.concatenate(o[p * DN_HEADS:(p + 1) * DN_HEADS], axis=1) for p in range(nprob)],
                            axis=0)
    dnn_all = jnp.concatenate([dnn] * DN_HEADS, axis=1)
    on = o_all * lax.rsqrt(_head_sumsq(o_all) * (1.0 / DN_VDIM) + EPS) * dnn_all
    for p in range(nprob):
        bo_ref[p] = (on[rows[p]] * z_ref[p].astype(F32)).astype(bo_ref.dtype)

    @pl.when(c == nc - 1)
    def _():
        sn_ref[...] = s_s[...]


def _deltanet(zqkv, sz, bg, conv_init, s0, conv_w, dn_norm, nprob, nseq, C, nc):
    M = zqkv.shape[0]
    R = nseq * C
    G = M // (R * nc)
    view = lambda a: a.reshape(G, R * nc, a.shape[-1])
    rmap = lambda col: (lambda g, c: (g, c, col))
    nsq = nprob * nseq
    state_spec = pl.BlockSpec((nsq, DN_HEADS, DN_KDIM, DN_VDIM), lambda g, c: (g, 0, 0, 0))
    bo, s_new = pl.pallas_call(
        functools.partial(_deltanet_kernel, nprob=nprob, nseq=nseq, C=C),
        grid=(G // nprob, nc),
        in_specs=[pl.BlockSpec((nprob, R, DN_QK), rmap(0)),
                  pl.BlockSpec((nprob, R, DN_QK), rmap(1)),
                  pl.BlockSpec((nprob, R, DN_V), rmap(2)),
                  pl.BlockSpec((nprob, R, DN_V), rmap(0)),
                  pl.BlockSpec((nprob, R, LANES), rmap(0)),
                  pl.BlockSpec((nsq, CONV_W - 1, DN_CONV_CH), lambda g, c: (g, 0, 0)),
                  state_spec,
                  pl.BlockSpec((CONV_W, DN_CONV_CH), lambda g, c: (0, 0)),
                  pl.BlockSpec((1, DN_VDIM), lambda g, c: (0, 0))],
        out_specs=[pl.BlockSpec((nprob, R, DN_V), rmap(0)), state_spec],
        out_shape=[jax.ShapeDtypeStruct((G, R * nc, DN_V), BF16),
                   jax.ShapeDtypeStruct(s0.shape, F32)],
        scratch_shapes=[pltpu.VMEM((nsq, C + SUBLANES, DN_CONV_CH), F32),
                        pltpu.VMEM((nsq, DN_HEADS, DN_KDIM, DN_VDIM), F32)],
        compiler_params=_params(("arbitrary", "arbitrary")),
        name="deltanet",
    )(view(zqkv), view(zqkv), view(zqkv), view(sz), view(bg), conv_init, s0, conv_w, dn_norm.reshape(1, DN_VDIM))
    return bo.reshape(M, DN_V), s_new


def _outproj_kernel(u_ref, vn_ref, ga_ref, gb_ref, bo_ref, x_ref, mod_ref, ws_ref, bs_ref, wpa_ref, wpb_ref,
                    wo_ref, n2_ref, wr_ref, br_ref, before_ref, x1_ref, h2_ref, rt_ref, cnt_ref, cnt_s):
    @pl.when(pl.program_id(0) == 0)
    def _():
        cnt_s[...] = jnp.zeros_like(cnt_s)

    tm = u_ref.shape[0]
    hm = tm // OUT_SPLIT
    bb, tt, _ = x_ref.shape
    HR = range(OUT_SPLIT)
    rows = [slice(h * hm, (h + 1) * hm) for h in HR]
    if bb == 1:
        xsl = [(slice(None), slice(h * (tt // OUT_SPLIT), (h + 1) * (tt // OUT_SPLIT))) for h in HR]
        msl = [slice(None)] * OUT_SPLIT
    else:
        xsl = [(slice(h * (bb // OUT_SPLIT), (h + 1) * (bb // OUT_SPLIT)), slice(None)) for h in HR]
        msl = [s[0] for s in xsl]

    def spatial(h):
        parts = []
        for r in range(hm // GMLP_CHUNK):
            rs = slice(h * hm + r * GMLP_CHUNK, h * hm + (r + 1) * GMLP_CHUNK)
            vn = vn_ref[rs, :].astype(BF16)
            cols = [jnp.dot(ws_ref[g], vn[:, g * A_GDIM:(g + 1) * A_GDIM], preferred_element_type=F32)
                    for g in range(A_GROUPS)]
            parts.append(jnp.concatenate(cols, axis=1) + bs_ref[...])
        return parts[0] if len(parts) == 1 else jnp.concatenate(parts, axis=0)

    sv = [spatial(h) for h in HR]
    a_out = [u_ref[rows[h], :].astype(F32) * sv[h] for h in HR]
    pa = [_dot(a_out[h], wpa_ref[...]) for h in HR]
    pb = [_dot(bo_ref[rows[h], :], wpb_ref[...]) for h in HR]
    m = [ga_ref[rows[h], :].astype(F32) * pa[h] + gb_ref[rows[h], :].astype(F32) * pb[h] for h in HR]
    mix = [_dot(m[h], wo_ref[...]) for h in HR]
    mod = [mod_ref[msl[h]] for h in HR]
    x = [x_ref[xsl[h][0], xsl[h][1], :] for h in HR]
    x1 = [x[h] + mod[h][:, 2:3, :] * mix[h].reshape(x[h].shape) for h in HR]
    for h in HR:
        x1_ref[xsl[h][0], xsl[h][1], :] = x1[h]
    y = [x1[h] * lax.rsqrt(jnp.mean(x1[h] * x1[h], axis=-1, keepdims=True) + EPS) * n2_ref[...] for h in HR]
    h2 = [(y[h] * (1.0 + mod[h][:, 4:5, :]) + mod[h][:, 3:4, :]).reshape(hm, D_MODEL) for h in HR]
    for h in HR:
        for j in range(SUBLANES):
            h2_ref[pl.ds(h * hm * SUBLANES + j, hm, stride=SUBLANES), :] = h2[h][:, j * LANES:(j + 1) * LANES]

    wr = wr_ref[...]
    lane = lax.broadcasted_iota(jnp.int32, (hm, LANES), 1)
    lanef = lane.astype(F32)
    neg = jnp.float32(-jnp.inf)
    big = jnp.float32(1e9)

    def route(h2h):
        hh, hl = _hilo(h2h)
        p = jnp.dot(hh, wr, preferred_element_type=F32)
        logits = (p[:, :LANES] + (p[:, LANES:] + jnp.dot(hl, wr[:, :LANES], preferred_element_type=F32))
                  + br_ref[...])
        gl = jnp.where(lane < N_GROUPS, logits, neg)
        gmax = jnp.max(gl, axis=-1, keepdims=True)
        gsel = jnp.min(jnp.where(gl == gmax, lanef, big), axis=-1, keepdims=True)
        gp = 1.0 / jnp.sum(jnp.exp(gl - gmax), axis=-1, keepdims=True)
        lo = ROUTER_LANE0 + EXP_PER_GROUP * gsel
        in_grp = (lanef >= lo) & (lanef < lo + EXP_PER_GROUP)
        el = jnp.where(in_grp, logits, neg)
        m1 = jnp.max(el, axis=-1, keepdims=True)
        i1 = jnp.min(jnp.where(el == m1, lanef, big), axis=-1, keepdims=True)
        el2 = jnp.where(lanef == i1, neg, el)
        m2 = jnp.max(el2, axis=-1, keepdims=True)
        i2 = jnp.min(jnp.where(el2 == m2, lanef, big), axis=-1, keepdims=True)
        ex = jnp.exp(m2 - m1)
        return i1 - ROUTER_LANE0, i2 - ROUTER_LANE0, gp / (1.0 + ex), gp * ex / (1.0 + ex)

    routed = [route(h2[h]) for h in HR]

    seen0 = cnt_s[...]
    for h in HR:
        e1, e2, w1, w2 = routed[h]
        oh1 = jnp.where(lanef == e1, 1.0, 0.0)
        oh2 = jnp.where(lanef == e2, 1.0, 0.0)
        oh = oh1 + oh2
        seen = jnp.dot(before_ref[...], oh.astype(BF16), preferred_element_type=F32) + seen0
        r1 = jnp.sum(oh1 * seen, axis=-1, keepdims=True)
        r2 = jnp.sum(oh2 * seen, axis=-1, keepdims=True)
        seen0 = seen0 + jnp.sum(oh, axis=0, keepdims=True)
        rec = jnp.zeros((hm, LANES), F32)
        for k, val in enumerate((e1, e2, r1, r2, w1, w2)):
            rec = jnp.where(lane == k, val, rec)
        rt_ref[rows[h], :] = rec
    cnt_s[...] = seen0
    cnt_ref[...] = seen0


def _outproj(u, vn, ga, gb, bo, x, mod, ws_eff, bs_full, w_pa, w_pb, w_o, norm2, w_r, b_r, bb, tt):
    B, T, _ = x.shape
    tm = bb * tt
    tpb = T // tt
    n_tiles = (B // bb) * tpb
    M = B * T
    cmap = lambda col: (lambda i: (i, col))
    full2 = lambda i: (0, 0)
    return pl.pallas_call(
        _outproj_kernel,
        grid=(n_tiles,),
        in_specs=[pl.BlockSpec((tm, A_WIDTH), cmap(0)),
                  pl.BlockSpec((tm, A_WIDTH), cmap(0)),
                  pl.BlockSpec((tm, D_MODEL), cmap(0)),
                  pl.BlockSpec((tm, D_MODEL), cmap(0)),
                  pl.BlockSpec((tm, DN_V), cmap(0)),
                  pl.BlockSpec((bb, tt, D_MODEL), lambda i: (i // tpb, i % tpb, 0)),
                  pl.BlockSpec((bb, N_MOD, D_MODEL), lambda i: (i // tpb, 0, 0)),
                  pl.BlockSpec((A_GROUPS, GMLP_CHUNK, GMLP_CHUNK), lambda i: (0, 0, 0)),
                  pl.BlockSpec((GMLP_CHUNK, A_WIDTH), full2),
                  pl.BlockSpec((A_WIDTH, D_MODEL), full2),
                  pl.BlockSpec((DN_V, D_MODEL), full2),
                  pl.BlockSpec((D_MODEL, D_MODEL), full2),
                  pl.BlockSpec((1, D_MODEL), full2),
                  pl.BlockSpec((D_MODEL, 2 * LANES), full2),
                  pl.BlockSpec((1, LANES), full2),
                  pl.BlockSpec((tm // OUT_SPLIT, tm // OUT_SPLIT), full2)],
        out_specs=[pl.BlockSpec((bb, tt, D_MODEL), lambda i: (i // tpb, i % tpb, 0)),
                   pl.BlockSpec((tm * SUBLANES, LANES), cmap(0)),
                   pl.BlockSpec((tm, LANES), cmap(0)),
                   pl.BlockSpec((1, LANES), full2)],
        out_shape=[jax.ShapeDtypeStruct((B, T, D_MODEL), F32),
                   jax.ShapeDtypeStruct((M * SUBLANES, LANES), F32),
                   jax.ShapeDtypeStruct((M, LANES), F32),
                   jax.ShapeDtypeStruct((1, LANES), F32)],
        scratch_shapes=[pltpu.VMEM((1, LANES), F32)],
        compiler_params=_params(("arbitrary",)),
        name="out_proj",
    )(u, vn, ga, gb, bo, x, mod, ws_eff, bs_full, w_pa, w_pb, w_o, norm2.reshape(1, D_MODEL), w_r, b_r,
      jnp.tril(jnp.ones((tm // OUT_SPLIT, tm // OUT_SPLIT), BF16), -1))


def _tile_copy(src_ref, src_off, dst_ref, dst_off, sem):
    return pltpu.make_async_copy(src_ref.at[pl.ds(pl.multiple_of(src_off, SUBLANES), SUBLANES), :],
                                 dst_ref.at[pl.ds(pl.multiple_of(dst_off, SUBLANES), SUBLANES), :], sem)


def _to_matrix(ref, n):
    return jnp.concatenate([ref[pl.ds(j, n, stride=SUBLANES), :] for j in range(SUBLANES)], axis=1)


def _zero_fill(pad_lo_ref, pad_n_ref, na_ref, xs_ref, zero_s, zsem):
    blk = EXPERT_TILE * SUBLANES
    n_tiles = xs_ref.shape[0] // blk
    zero_s[...] = jnp.zeros_like(zero_s)
    run = lambda off, rows: pltpu.make_async_copy(
        zero_s.at[pl.ds(0, rows * SUBLANES), :],
        xs_ref.at[pl.ds(pl.multiple_of(off * SUBLANES, SUBLANES), rows * SUBLANES), :], zsem)
    sizes = [1 << b for b in reversed(range(EXPERT_TILE.bit_length() - 1))]
    for wait in (False, True):
        for e in range(N_EXPERTS):
            off = pad_lo_ref[e]
            for rows in sizes:
                bit = pad_n_ref[e] & rows

                @pl.when(bit != 0)
                def _(off=off, rows=rows):
                    run(0, rows).wait() if wait else run(off, rows).start()

                off = off + bit

        def idle(i, carry):
            run(0, EXPERT_TILE).wait() if wait else run(i * EXPERT_TILE, EXPERT_TILE).start()
            return carry

        lax.fori_loop(na_ref[0], n_tiles, idle, 0)


def _dispatch_kernel(pad_lo_ref, pad_n_ref, na_ref, pos_ref, *refs, first_tile):
    n_paths = len(first_tile) - 1
    h_refs = refs[:n_paths]
    xs_ref, zero_s, sem, zsem = refs[n_paths:]
    i = pl.program_id(0)

    @pl.when(i == 0)
    def _():
        _zero_fill(pad_lo_ref, pad_n_ref, na_ref, xs_ref, zero_s, zsem)

    tm = h_refs[0].shape[0] // SUBLANES
    for p, h_ref in enumerate(h_refs):
        @pl.when((i >= first_tile[p]) & (i < first_tile[p + 1]))
        def _(h_ref=h_ref):
            def issue(t, carry):
                for k in range(2):
                    _tile_copy(h_ref, t * SUBLANES, xs_ref, pos_ref[0, k, t], sem).start(priority=k)
                return carry

            lax.fori_loop(0, tm, issue, 0, unroll=8)

            def drain(t, carry):
                for k in range(2):
                    _tile_copy(h_ref, 0, xs_ref, 0, sem).wait()
                return carry

            lax.fori_loop(0, tm, drain, 0, unroll=8)


def _dispatch(h2s, pos, tm, n_rows, pad_lo, pad_n, n_active):
    blk = tm * SUBLANES
    tiles = [h.shape[0] // blk for h in h2s]
    first_tile = [sum(tiles[:p]) for p in range(len(tiles) + 1)]
    hmap = lambda p: (lambda i, *_: (jnp.clip(i - first_tile[p], 0, tiles[p] - 1), 0))
    grid_spec = pltpu.PrefetchScalarGridSpec(
        num_scalar_prefetch=3,
        grid=(first_tile[-1],),
        in_specs=[pl.BlockSpec((1, 2, tm), lambda i, *_: (i, 0, 0), memory_space=pltpu.SMEM)]
                 + [pl.BlockSpec((blk, LANES), hmap(p)) for p in range(len(tiles))],
        out_specs=pl.BlockSpec(memory_space=pl.ANY),
        scratch_shapes=[pltpu.VMEM((EXPERT_TILE * SUBLANES, LANES), F32),
                        pltpu.SemaphoreType.DMA(()), pltpu.SemaphoreType.DMA(())],
    )
    return pl.pallas_call(
        functools.partial(_dispatch_kernel, first_tile=tuple(first_tile)),
        grid_spec=grid_spec,
        out_shape=jax.ShapeDtypeStruct((n_rows * SUBLANES, LANES), F32),
        compiler_params=_params(("arbitrary",)),
        name="moe_dispatch",
    )(pad_lo, pad_n, n_active, pos, *h2s)


def _expert_kernel(te_ref, na_ref, x_ref, *refs):
    del te_ref
    w_refs, o_ref = refs[:-1], refs[-1]
    blk = EXPERT_TILE * SUBLANES
    live = EXPERTS_PER_STEP * pl.program_id(0) < na_ref[0]

    @pl.when(live)
    def _():
        TR = range(EXPERTS_PER_STEP)
        wg, wu, wd = ([w_refs[3 * t + k][0] for t in TR] for k in range(3))
        x = [_to_matrix(x_ref.at[pl.ds(t * blk, blk), :], EXPERT_TILE).astype(BF16) for t in TR]
        a = [_dot(x[t], wg[t]) for t in TR]
        u = [_dot(x[t], wu[t]) for t in TR]
        o = [_dot(a[t] * _sigmoid(a[t]) * u[t], wd[t]) for t in TR]
        for t in TR:
            for j in range(SUBLANES):
                o_ref[pl.ds(t * blk + j, EXPERT_TILE, stride=SUBLANES), :] = o[t][:, j * LANES:(j + 1) * LANES]

    @pl.when(jnp.logical_not(live))
    def _():
        o_ref[...] = jnp.zeros_like(o_ref)


def _experts(xs, tile_expert, n_active, wg, wu, wd):
    blk = EXPERT_TILE * SUBLANES * EXPERTS_PER_STEP
    n_steps = xs.shape[0] // blk
    last = lambda na: jnp.maximum(na[0] - 1, 0)
    xmap = lambda i, te, na: (jnp.minimum(i, last(na) // EXPERTS_PER_STEP), 0)
    wmap = lambda t: (lambda i, te, na: (te[jnp.minimum(EXPERTS_PER_STEP * i + t, last(na))], 0, 0))
    w_specs = []
    for t in range(EXPERTS_PER_STEP):
        w_specs += [pl.BlockSpec((1, D_MODEL, D_EXPERT), wmap(t)),
                    pl.BlockSpec((1, D_MODEL, D_EXPERT), wmap(t)),
                    pl.BlockSpec((1, D_EXPERT, D_MODEL), wmap(t))]
    grid_spec = pltpu.PrefetchScalarGridSpec(
        num_scalar_prefetch=2,
        grid=(n_steps,),
        in_specs=[pl.BlockSpec((blk, LANES), xmap)] + w_specs,
        out_specs=pl.BlockSpec((blk, LANES), lambda i, te, na: (i, 0)),
    )
    return pl.pallas_call(
        _expert_kernel,
        grid_spec=grid_spec,
        out_shape=jax.ShapeDtypeStruct(xs.shape, F32),
        compiler_params=_params(("arbitrary",)),
        name="moe_experts",
    )(tile_expert, n_active, xs, *([wg, wu, wd] * EXPERTS_PER_STEP))


def _combine_kernel(pos_ref, pos_next_ref, rt_ref, x1_ref, mod_ref, fn_ref, ys_ref, y_ref, g_s, sems):
    tm = rt_ref.shape[0]
    i = pl.program_id(0)
    n = pl.num_programs(0)
    slot = i % 2

    def gather(p_ref, s):
        def issue(t, carry):
            for k in range(2):
                _tile_copy(ys_ref, p_ref[0, k, t], g_s.at[s, k], t * SUBLANES, sems.at[s]).start(priority=k)
            return carry

        lax.fori_loop(0, tm, issue, 0, unroll=8)

    @pl.when(i == 0)
    def _():
        gather(pos_ref, slot)

    @pl.when(i + 1 < n)
    def _():
        gather(pos_next_ref, 1 - slot)

    def drain(t, carry):
        for k in range(2):
            _tile_copy(ys_ref, 0, g_s.at[slot, k], 0, sems.at[slot]).wait()
        return carry

    lax.fori_loop(0, tm, drain, 0, unroll=8)

    rt = rt_ref[...]
    moe = rt[:, 4:5] * _to_matrix(g_s.at[slot, 0], tm) + rt[:, 5:6] * _to_matrix(g_s.at[slot, 1], tm)
    mod = mod_ref[...]
    x1 = x1_ref[...]
    x2 = x1 + mod[:, 5:6, :] * moe.reshape(x1.shape)
    y_ref[...] = x2 * lax.rsqrt(jnp.mean(x2 * x2, axis=-1, keepdims=True) + EPS) * fn_ref[...]


def _combine(ys, pos, route, x1, mod, final_norm, bb, tt):
    B, T, _ = x1.shape
    tm = bb * tt
    tpb = T // tt
    n_tiles = (B // bb) * tpb
    xmap = lambda i: (i // tpb, i % tpb, 0)
    return pl.pallas_call(
        _combine_kernel,
        grid=(n_tiles,),
        in_specs=[pl.BlockSpec((1, 2, tm), lambda i: (i, 0, 0), memory_space=pltpu.SMEM),
                  pl.BlockSpec((1, 2, tm), lambda i: (jnp.minimum(i + 1, n_tiles - 1), 0, 0),
                               memory_space=pltpu.SMEM),
                  pl.BlockSpec((tm, LANES), lambda i: (i, 0)),
                  pl.BlockSpec((bb, tt, D_MODEL), xmap),
                  pl.BlockSpec((bb, N_MOD, D_MODEL), lambda i: (i // tpb, 0, 0)),
                  pl.BlockSpec((1, D_MODEL), lambda i: (0, 0)),
                  pl.BlockSpec(memory_space=pl.ANY)],
        out_specs=pl.BlockSpec((bb, tt, D_MODEL), xmap),
        out_shape=jax.ShapeDtypeStruct((B, T, D_MODEL), F32),
        scratch_shapes=[pltpu.VMEM((2, 2, tm * SUBLANES, LANES), F32), pltpu.SemaphoreType.DMA((2,))],
        compiler_params=_params(("arbitrary",)),
        name="moe_combine",
    )(pos, pos, route, x1, mod, final_norm.reshape(1, D_MODEL), ys)


def _moe(paths, wg, wu, wd, final_norm):
    n_assign = sum(2 * p["route"].shape[0] for p in paths)
    max_tiles = (n_assign + N_EXPERTS * (EXPERT_TILE - 1)) // EXPERT_TILE
    max_tiles += -max_tiles % EXPERTS_PER_STEP
    cnts = [p["counts"][0, :N_EXPERTS].astype(jnp.int32) for p in paths]
    cnt = sum(cnts)
    nt = (cnt + EXPERT_TILE - 1) // EXPERT_TILE
    ends = jnp.cumsum(nt)
    base = ((ends - nt) * EXPERT_TILE).astype(jnp.int32)
    n_active = ends[-1:].astype(jnp.int32)
    tile_expert = jnp.minimum(jnp.sum(jnp.arange(max_tiles)[:, None] >= ends[None, :], axis=1),
                              N_EXPERTS - 1).astype(jnp.int32)
    tm = paths[0]["bb"] * paths[0]["tt"]
    assert all(p["bb"] * p["tt"] == tm for p in paths)
    start = base
    for p, c in zip(paths, cnts):
        eid = p["route"][:, 0:2].astype(jnp.int32)
        rank = p["route"][:, 2:4].astype(jnp.int32)
        first_row = jnp.sum(jnp.where(eid[..., None] == jnp.arange(N_EXPERTS), start, 0), axis=-1)
        pos = (first_row + rank) * SUBLANES
        p["pos"] = jnp.transpose(pos.reshape(-1, tm, 2), (0, 2, 1))
        start = start + c
    xs = _dispatch([p["h2"] for p in paths], jnp.concatenate([p["pos"] for p in paths], axis=0), tm,
                   max_tiles * EXPERT_TILE, base + cnt, nt * EXPERT_TILE - cnt, n_active)
    ys = _experts(xs, tile_expert, n_active, wg, wu, wd)
    return [_combine(ys, p["pos"], p["route"], p["x1"], p["mod"], final_norm, p["bb"], p["tt"]) for p in paths]


def _mixers(x, mod, conv_init, s0, prm, *, bb, tt, dn_nprob, dn_nseq, dn_chunk, vn_dtype):
    B, T, _ = x.shape
    u, vn, zqkv, sz, ga, gb, bg = _inproj(x, mod, prm["norm1"], prm["w_a"], prm["w_b"], prm["w_bg"], prm["ln_v_w"],
                                          prm["ln_v_b"], prm["adt"], bb, tt, vn_dtype)
    nc = T // dn_chunk
    bo, s_new = _deltanet(zqkv, sz, bg, conv_init, s0, prm["conv_w"], prm["dn_norm"], dn_nprob, dn_nseq, dn_chunk, nc)
    L = min(GMLP_CHUNK, T)
    reps = GMLP_CHUNK // L
    wm = prm["w_s_tril"][:, :L, :L]
    if reps > 1:
        blk = (jnp.arange(GMLP_CHUNK)[:, None] // L) == (jnp.arange(GMLP_CHUNK)[None, :] // L)
        wm = jnp.where(blk[None], jnp.tile(wm, (1, reps, reps)), 0.0)
    ws_eff = wm.astype(BF16)
    bs_rows = jnp.tile(jnp.transpose(prm["b_s"][:, :L]), (reps, 1))
    bs_full = jnp.repeat(bs_rows, A_GDIM, axis=1)
    x1, h2, route, counts = _outproj(u, vn, ga, gb, bo, x, mod, ws_eff, bs_full, prm["w_pa"], prm["w_pb"],
                                     prm["w_o"], prm["norm2"], prm["w_r"], prm["b_r"], bb, tt)
    tail = min(T, CONV_W - 1)
    zqkv_tail = zqkv.reshape(B, T, DN_CONV_CH)[:, T - tail:, :]
    conv_new = jnp.concatenate([conv_init, zqkv_tail], axis=1)[:, -(CONV_W - 1):, :]
    path = dict(h2=h2, route=route, counts=counts, x1=x1, mod=mod, bb=bb, tt=tt)
    return path, conv_new, s_new, vn.reshape(B, T, A_WIDTH)


def kernel(x_prompt, x_sample, c_prompt, c_sample, state_conv, state_delta, w_ada, b_ada, norm1, norm2, w_in, conv_w, a_log, dt_bias, dn_norm, ln_v_w, ln_v_b, w_s, b_s, w_pa, w_pb, w_o, w_rg, b_rg, w_re, b_re, w_e_gate, w_e_up, w_e_down, final_norm):
    depth = w_ada.shape[0]
    assert depth == 1
    bp, tp, _ = x_prompt.shape
    bs_, ts, _ = x_sample.shape
    l = 0
    assert w_in.shape[-1] == (N_ALIGNED + 2) * IN_BLOCK + 2 * DN_HEADS
    w_a = w_in[l:l + 1].astype(BF16)
    w_b, w_bg = _prep_in_weights(w_a)
    adt = jnp.stack([jnp.pad(a_log[l], (DN_HEADS, LANES - 2 * DN_HEADS)),
                     jnp.pad(dt_bias[l], (DN_HEADS, LANES - 2 * DN_HEADS))])
    w_r = jnp.pad(jnp.concatenate([w_rg[l], w_re[l]], axis=1), ((0, 0), (0, LANES - N_GROUPS - N_EXPERTS)))
    w_r_hi = w_r.astype(BF16)
    w_r = jnp.concatenate([w_r_hi, (w_r - w_r_hi.astype(F32)).astype(BF16)], axis=1)
    b_r = jnp.pad(jnp.concatenate([b_rg[l], b_re[l]]), (0, LANES - N_GROUPS - N_EXPERTS)).reshape(1, LANES)
    tri = jnp.tril(jnp.ones((GMLP_CHUNK, GMLP_CHUNK), F32))
    prm = dict(
        norm1=norm1[l], norm2=norm2[l], w_a=w_a, w_b=w_b, w_bg=w_bg, adt=adt, ln_v_w=ln_v_w[l], ln_v_b=ln_v_b[l],
        conv_w=conv_w[l], dn_norm=dn_norm[l], w_s_tril=w_s[l] * tri, b_s=b_s[l],
        w_pa=w_pa[l].astype(BF16), w_pb=w_pb[l].astype(BF16), w_o=w_o[l].astype(BF16), w_r=w_r, b_r=b_r,
        w_e_gate=w_e_gate[l].reshape(N_EXPERTS, D_MODEL, D_EXPERT),
        w_e_up=w_e_up[l].reshape(N_EXPERTS, D_MODEL, D_EXPERT),
        w_e_down=w_e_down[l].reshape(N_EXPERTS, D_EXPERT, D_MODEL),
    )

    c_all = jnp.concatenate([c_prompt, c_sample], axis=0)
    mod = _ada(c_all, w_ada[l], b_ada[l]).reshape(bp + bs_, N_MOD, D_MODEL)
    mod_p, mod_s = mod[:bp], mod[bp:]

    cinit_p = jnp.zeros((bp, CONV_W - 1, DN_CONV_CH), F32)
    cinit_s = state_conv[l]
    s0_p = jnp.zeros((bp, DN_HEADS, DN_KDIM, DN_VDIM), F32)

    path_p, conv_p, delta_p, _ = _mixers(x_prompt, mod_p, cinit_p, s0_p, prm,
                                         bb=1, tt=min(ROW_TILE, tp), dn_nprob=4 if bp % 4 == 0 else 1, dn_nseq=1,
                                         dn_chunk=min(DN_CHUNK, tp), vn_dtype=BF16)
    path_s, conv_s, delta_s, vn_s = _mixers(x_sample, mod_s, cinit_s, state_delta[l], prm,
                                            bb=min(bs_, ROW_TILE // ts), tt=ts,
                                            dn_nprob=2 if bs_ % (2 * DN_CHUNK // ts) == 0 else 1,
                                            dn_nseq=min(bs_, DN_CHUNK // ts), dn_chunk=ts, vn_dtype=F32)
    yp, ys = _moe([path_p, path_s], prm["w_e_gate"], prm["w_e_up"], prm["w_e_down"], final_norm)
    return (yp, ys, conv_p[None], conv_s[None], delta_p[None], delta_s[None], vn_s[None])
```

```python
import functools

import jax
import jax.numpy as jnp
from jax import lax
from jax.experimental import pallas as pl
from jax.experimental.pallas import tpu as pltpu

F32 = jnp.float32
BF16 = jnp.bfloat16

D_MODEL = 1024
A_WIDTH = 1024
A_GROUPS = 4
A_GDIM = A_WIDTH // A_GROUPS
GMLP_CHUNK = 128
DN_HEADS = 8
DN_KDIM = 128
DN_VDIM = 128
DN_QK = DN_HEADS * DN_KDIM
DN_V = DN_HEADS * DN_VDIM
DN_CONV_CH = 2 * DN_QK + DN_V
CONV_W = 4
DN_CHUNK = 64
N_GROUPS = 4
EXP_PER_GROUP = 8
N_EXPERTS = N_GROUPS * EXP_PER_GROUP
D_EXPERT = 256
N_MOD = 6
EPS = 1e-6

LANES = 128
SUBLANES = 8
ROUTER_LANE0 = N_GROUPS
VMEM_LIMIT = 56 * 1024 * 1024
ROW_TILE = 512
IN_SPLIT = 2
OUT_SPLIT = 2
EXPERT_TILE = 1024
EXPERTS_PER_STEP = 1


def _sigmoid(x):
    return 1.0 / (1.0 + jnp.exp(-x))


def _gelu(x):
    return 0.5 * x * (1.0 + lax.erf(x * 0.7071067811865476))


def _softplus(x):
    return jnp.maximum(x, 0.0) + jnp.log(1.0 + jnp.exp(-jnp.abs(x)))


def _dot(a, b):
    return jnp.dot(a.astype(BF16), b.astype(BF16), preferred_element_type=F32)


def _split3(x):
    hi = x.astype(BF16)
    r1 = x - hi.astype(F32)
    mid = r1.astype(BF16)
    lo = (r1 - mid.astype(F32)).astype(BF16)
    return hi, mid, lo


def _dot_exact_lhs(a_bf16, x):
    hi, mid, lo = _split3(x)
    f = lambda p: jnp.dot(a_bf16, p, preferred_element_type=F32)
    return f(hi) + f(mid) + f(lo)


_PASSES = {"invp": 3, "invn": 3, "inv_hi": 1, "inv_kmax": 8, "sol": 1, "kk": 1, "ws": 1, "attn": 1, "state": 1}


def _hilo(x):
    hi = x.astype(BF16)
    lo = (x - hi.astype(F32)).astype(BF16)
    return hi, lo


def _mm(site, a, b, dims=(((1,), (0,)), ((), ()))):
    f = lambda p, q: lax.dot_general(p, q, dims, preferred_element_type=F32)
    if _PASSES[site] == 1:
        return f(a.astype(BF16), b.astype(BF16))
    ah, al = _hilo(a)
    bh, bl = _hilo(b)
    return f(ah, bh) + (f(ah, bl) + f(al, bh))


_NT = (((1,), (1,)), ((), ()))
_TN = (((0,), (0,)), ((), ()))

def _params(sem):
    return pltpu.CompilerParams(dimension_semantics=sem, vmem_limit_bytes=VMEM_LIMIT)


def _ada_kernel(c_ref, w_ref, b_ref, o_ref):
    c = c_ref[...]
    o_ref[...] = _dot(c * _sigmoid(c), w_ref[...]) + b_ref[...]


def _ada(c_all, w_ada, b_ada):
    n = c_all.shape[0]
    width = w_ada.shape[1]
    bn = 512
    return pl.pallas_call(
        _ada_kernel,
        grid=(width // bn,),
        in_specs=[pl.BlockSpec((n, D_MODEL), lambda j: (0, 0)),
                  pl.BlockSpec((D_MODEL, bn), lambda j: (0, j)),
                  pl.BlockSpec((1, bn), lambda j: (0, j))],
        out_specs=pl.BlockSpec((n, bn), lambda j: (0, j)),
        out_shape=jax.ShapeDtypeStruct((n, width), F32),
        compiler_params=_params(("arbitrary",)),
        name="ada_mod",
    )(c_all, w_ada, b_ada.reshape(1, width))


IN_BLOCK = 1024
N_ALIGNED = 6


def _realign_kernel(a_ref, b_ref, o_ref, bg_ref):
    s = 2 * DN_HEADS
    a = a_ref[0]
    o_ref[...] = jnp.concatenate([a[:, s:], b_ref[0, :, :s]], axis=1).astype(o_ref.dtype)

    @pl.when(pl.program_id(0) == 0)
    def _():
        lane = lax.broadcasted_iota(jnp.int32, (D_MODEL, LANES), 1)
        bg_ref[...] = jnp.where(lane < s, a[:, :LANES], 0.0).astype(bg_ref.dtype)


def _prep_in_weights(w_in):
    w_b, w_bg = pl.pallas_call(
        _realign_kernel,
        grid=(2,),
        in_specs=[pl.BlockSpec((1, D_MODEL, IN_BLOCK), lambda j: (0, 0, N_ALIGNED + j)),
                  pl.BlockSpec((1, D_MODEL, IN_BLOCK), lambda j: (0, 0, N_ALIGNED + j + 1))],
        out_specs=[pl.BlockSpec((D_MODEL, IN_BLOCK), lambda j: (0, j)),
                   pl.BlockSpec((D_MODEL, LANES), lambda j: (0, 0))],
        out_shape=[jax.ShapeDtypeStruct((D_MODEL, 2 * IN_BLOCK), BF16),
                   jax.ShapeDtypeStruct((D_MODEL, LANES), BF16)],
        compiler_params=_params(("arbitrary",)),
        name="prep_w_gates",
    )(w_in, w_in)
    return w_b, w_bg


def _inproj_kernel(x_ref, mod_ref, n1_ref, wa_ref, wb_ref, wbg_ref, lnw_ref, lnb_ref, adt_ref,
                   u_ref, vn_ref, qkv_ref, sz_ref, ga_ref, gb_ref, bg_ref):
    tm = u_ref.shape[0]
    hm = tm // IN_SPLIT
    bb, tt, _ = x_ref.shape
    HR = range(IN_SPLIT)
    rows = [slice(h * hm, (h + 1) * hm) for h in HR]
    if bb == 1:
        xsl = [(slice(None), slice(h * (tt // IN_SPLIT), (h + 1) * (tt // IN_SPLIT))) for h in HR]
        msl = [slice(None)] * IN_SPLIT
    else:
        xsl = [(slice(h * (bb // IN_SPLIT), (h + 1) * (bb // IN_SPLIT)), slice(None)) for h in HR]
        msl = [sl[0] for sl in xsl]
    hb = []
    for h in HR:
        x = x_ref[xsl[h][0], xsl[h][1], :]
        y = x * lax.rsqrt(jnp.mean(x * x, axis=-1, keepdims=True) + EPS) * n1_ref[...]
        mod = mod_ref[msl[h]]
        hb.append((y * (1.0 + mod[:, 1:2, :]) + mod[:, 0:1, :]).reshape(hm, D_MODEL).astype(BF16))
    adt = adt_ref[...]
    lane = lax.broadcasted_iota(jnp.int32, (hm, LANES), 1)
    for h in HR:
        zbg = jnp.dot(hb[h], wbg_ref[...], preferred_element_type=F32)
        g = -jnp.exp(adt[0:1, :]) * _softplus(zbg + adt[1:2, :])
        bg_ref[rows[h], :] = jnp.where(lane < DN_HEADS, _sigmoid(zbg), g)

    n_a = wa_ref.shape[2] // IN_BLOCK
    blk = lambda h, j: jnp.dot(hb[h], wa_ref[0, :, j * IN_BLOCK:(j + 1) * IN_BLOCK] if j < n_a else
                               wb_ref[:, (j - n_a) * IN_BLOCK:(j - n_a + 1) * IN_BLOCK],
                               preferred_element_type=F32)
    for h in HR:
        u_ref[rows[h], :] = _gelu(blk(h, 0)).astype(u_ref.dtype)
    for h in HR:
        a = _gelu(blk(h, 1))
        ac = a - jnp.mean(a, axis=-1, keepdims=True)
        var = jnp.mean(ac * ac, axis=-1, keepdims=True)
        vn_ref[rows[h], :] = (ac * lax.rsqrt(var + EPS) * lnw_ref[...] + lnb_ref[...]).astype(vn_ref.dtype)
    for j in range(3):
        for h in HR:
            qkv_ref[rows[h], j * IN_BLOCK:(j + 1) * IN_BLOCK] = blk(h, 2 + j)
    for h in HR:
        z = blk(h, 5)
        sz_ref[rows[h], :] = (z * _sigmoid(z)).astype(sz_ref.dtype)
    for h in HR:
        ga_ref[rows[h], :] = _sigmoid(blk(h, 6)).astype(ga_ref.dtype)
    for h in HR:
        gb_ref[rows[h], :] = _sigmoid(blk(h, 7)).astype(gb_ref.dtype)


def _inproj(x, mod, norm1, w_a, w_b, w_bg, ln_w, ln_b, adt, bb, tt, vn_dtype):
    B, T, _ = x.shape
    tm = bb * tt
    tpb = T // tt
    n_tiles = (B // bb) * tpb
    M = B * T
    const = lambda i: (0, 0)
    rows = lambda i: (i, 0)
    resident = dict(pipeline_mode=pl.Buffered(1))
    in_specs = [pl.BlockSpec((bb, tt, D_MODEL), lambda i: (i // tpb, i % tpb, 0)),
                pl.BlockSpec((bb, N_MOD, D_MODEL), lambda i: (i // tpb, 0, 0)),
                pl.BlockSpec((1, D_MODEL), const),
                pl.BlockSpec((1, D_MODEL, N_ALIGNED * IN_BLOCK), lambda i: (0, 0, 0), **resident),
                pl.BlockSpec(w_b.shape, const, **resident),
                pl.BlockSpec((D_MODEL, LANES), const, **resident),
                pl.BlockSpec((1, A_WIDTH), const),
                pl.BlockSpec((1, A_WIDTH), const),
                pl.BlockSpec((2, LANES), const)]
    args = [x, mod, norm1.reshape(1, D_MODEL), w_a, w_b, w_bg, ln_w.reshape(1, A_WIDTH), ln_b.reshape(1, A_WIDTH),
            adt]
    out_specs = [pl.BlockSpec((tm, A_WIDTH), rows),
                 pl.BlockSpec((tm, A_WIDTH), rows),
                 pl.BlockSpec((tm, DN_CONV_CH), rows),
                 pl.BlockSpec((tm, DN_V), rows),
                 pl.BlockSpec((tm, D_MODEL), rows),
                 pl.BlockSpec((tm, D_MODEL), rows),
                 pl.BlockSpec((tm, LANES), rows)]
    out_shape = [jax.ShapeDtypeStruct((M, A_WIDTH), BF16),
                 jax.ShapeDtypeStruct((M, A_WIDTH), vn_dtype),
                 jax.ShapeDtypeStruct((M, DN_CONV_CH), F32),
                 jax.ShapeDtypeStruct((M, DN_V), BF16),
                 jax.ShapeDtypeStruct((M, D_MODEL), BF16),
                 jax.ShapeDtypeStruct((M, D_MODEL), BF16),
                 jax.ShapeDtypeStruct((M, LANES), F32)]
    return pl.pallas_call(
        _inproj_kernel,
        grid=(n_tiles,),
        in_specs=in_specs,
        out_specs=out_specs,
        out_shape=out_shape,
        compiler_params=_params(("arbitrary",)),
        name="in_proj",
    )(*args)


def _inv_unit_lower_minus_eye(lmats, nilpotent):
    ns = [-l for l in lmats]
    ps = list(lmats)
    k = 2
    while k < nilpotent:
        early = k <= _PASSES["inv_kmax"]
        ps = [_mm("invp" if early else "inv_hi", p, p) for p in ps]
        ns = [n + p + _mm("invn" if early else "inv_hi", n, p) for n, p in zip(ns, ps)]
        k *= 2
    return ns


def _head_sumsq(x):
    pair = 2 * DN_KDIM
    r = lax.broadcasted_iota(jnp.int32, (pair, pair), 0) >= DN_KDIM
    c = lax.broadcasted_iota(jnp.int32, (pair, pair), 1) >= DN_KDIM
    ones2 = jnp.where(r == c, 1.0, 0.0).astype(BF16)
    sq = (x * x).astype(BF16)
    return jnp.concatenate([jnp.dot(sq[:, p * pair:(p + 1) * pair], ones2, preferred_element_type=F32)
                            for p in range(x.shape[1] // pair)], axis=1)


def _deltanet_kernel(q_ref, k_ref, v_ref, z_ref, bg_ref, cinit_ref, s0_ref, cw_ref, dnn_ref,
                     bo_ref, sn_ref, xp_s, s_s, *, nprob, nseq, C):
    R = nseq * C
    c = pl.program_id(1)
    nc = pl.num_programs(1)

    @pl.when(c == 0)
    def _():
        xp_s[:, SUBLANES - (CONV_W - 1):SUBLANES, :] = cinit_ref[...]
        s_s[...] = s0_ref[...]

    @pl.when(c > 0)
    def _():
        xp_s[:, 0:SUBLANES, :] = xp_s[:, C:C + SUBLANES, :]

    for p in range(nprob):
        for s in range(nseq):
            i = p * nseq + s
            xp_s[i, SUBLANES:SUBLANES + C, 0:DN_QK] = q_ref[p, s * C:(s + 1) * C, :]
            xp_s[i, SUBLANES:SUBLANES + C, DN_QK:2 * DN_QK] = k_ref[p, s * C:(s + 1) * C, :]
            xp_s[i, SUBLANES:SUBLANES + C, 2 * DN_QK:DN_CONV_CH] = v_ref[p, s * C:(s + 1) * C, :]

    cw = cw_ref[...]
    base = SUBLANES - (CONV_W - 1)
    acc = None
    for j in range(CONV_W):
        term = xp_s[:, base + j:base + j + C, :] * cw[j:j + 1, :]
        acc = term if acc is None else acc + term
    qkv = acc.reshape(nprob * R, DN_CONV_CH)
    qkv = qkv * _sigmoid(qkv)
    q_all = qkv[:, 0:DN_QK]
    k_all = qkv[:, DN_QK:2 * DN_QK]
    v_all = qkv[:, 2 * DN_QK:DN_CONV_CH]
    qn_all = q_all * lax.rsqrt(_head_sumsq(q_all) + EPS) * (DN_KDIM ** -0.5)
    kn_all = k_all * lax.rsqrt(_head_sumsq(k_all) + EPS)

    row = lax.broadcasted_iota(jnp.int32, (R, R), 0)
    col = lax.broadcasted_iota(jnp.int32, (R, R), 1)
    if nseq > 1:
        shift = C.bit_length() - 1
        same = lax.shift_right_logical(row, shift) == lax.shift_right_logical(col, shift)
        tril = same & (col <= row)
        strict = same & (col < row)
    else:
        tril = col <= row
        strict = col < row
    tril_b = jnp.where(tril, 1.0, 0.0).astype(BF16)
    bg = [bg_ref[p] for p in range(nprob)]
    gc = [_dot_exact_lhs(tril_b, bg[p]) for p in range(nprob)]
    gct = [g.T for g in gc]
    dnn = dnn_ref[...]

    IT = [(p, h) for p in range(nprob) for h in range(DN_HEADS)]
    NI = range(len(IT))
    hsl = [slice(h * DN_KDIM, (h + 1) * DN_KDIM) for h in range(DN_HEADS)]
    rows = [slice(p * R, (p + 1) * R) for p in range(nprob)]
    qn = [qn_all[rows[p], hsl[h]] for p, h in IT]
    kn = [kn_all[rows[p], hsl[h]] for p, h in IT]
    vh = [v_all[rows[p], hsl[h]] for p, h in IT]
    beta = [bg[p][:, h:h + 1] for p, h in IT]
    gcol = [gc[p][:, DN_HEADS + h:DN_HEADS + h + 1] for p, h in IT]
    grow = [gct[p][DN_HEADS + h:DN_HEADS + h + 1, :] for p, h in IT]
    decay = [jnp.exp(jnp.minimum(gcol[i] - grow[i], 0.0)) for i in NI]
    eg = [jnp.exp(gcol[i]) for i in NI]
    kb = [kn[i] * beta[i] for i in NI]
    vb = [vh[i] * beta[i] for i in NI]
    a2 = [_mm("kk", jnp.concatenate([kb[i], qn[i]], axis=0), kn[i], _NT) for i in NI]
    lmat = [jnp.where(strict, a2[i][:R] * decay[i], 0.0) for i in NI]
    attn = [jnp.where(tril, a2[i][R:] * decay[i], 0.0) for i in NI]
    tn = _inv_unit_lower_minus_eye(lmat, C)
    rhs = [jnp.concatenate([vb[i], kb[i] * eg[i]], axis=1) for i in NI]
    sol = [rhs[i] + _mm("sol", tn[i], rhs[i]) for i in NI]
    u = [x[:, :DN_VDIM] for x in sol]
    w = [x[:, DN_VDIM:] for x in sol]
    qg = [qn[i] * eg[i] for i in NI]
    vnew = [[None] * nseq for _ in NI]
    qs = [[None] * nseq for _ in NI]
    for s in range(nseq):
        rs = slice(s * C, (s + 1) * C)
        st = [s_s[p * nseq + s, h] for p, h in IT]
        ws = [_mm("ws", jnp.concatenate([w[i][rs], qg[i][rs]], axis=0), st[i]) for i in NI]
        for i in NI:
            vnew[i][s] = u[i][rs] - ws[i][:C]
            qs[i][s] = ws[i][C:]
        glast = [gcol[i][(s + 1) * C - 1:(s + 1) * C, :] for i in NI]
        kd = [kn[i][rs] * jnp.exp(glast[i] - gcol[i][rs]) for i in NI]
        upd = [_mm("state", kd[i], vnew[i][s], _TN) for i in NI]
        for i, (p, h) in enumerate(IT):
            s_s[p * nseq + s, h] = st[i] * jnp.exp(glast[i]) + upd[i]
    cat = lambda parts: parts[0] if nseq == 1 else jnp.concatenate(parts, axis=0)
    o = [cat(qs[i]) + _mm("attn", attn[i], cat(vnew[i])) for i in NI]
    o_all = jnp.concatenate([jnp.concatenate(o[p * DN_HEADS:(p + 1) * DN_HEADS], axis=1) for p in range(nprob)],
                            axis=0)
    dnn_all = jnp.concatenate([dnn] * DN_HEADS, axis=1)
    on = o_all * lax.rsqrt(_head_sumsq(o_all) * (1.0 / DN_VDIM) + EPS) * dnn_all
    for p in range(nprob):
        bo_ref[p] = (on[rows[p]] * z_ref[p].astype(F32)).astype(bo_ref.dtype)

    @pl.when(c == nc - 1)
    def _():
        sn_ref[...] = s_s[...]


def _deltanet(zqkv, sz, bg, conv_init, s0, conv_w, dn_norm, nprob, nseq, C, nc):
    M = zqkv.shape[0]
    R = nseq * C
    G = M // (R * nc)
    view = lambda a: a.reshape(G, R * nc, a.shape[-1])
    rmap = lambda col: (lambda g, c: (g, c, col))
    nsq = nprob * nseq
    state_spec = pl.BlockSpec((nsq, DN_HEADS, DN_KDIM, DN_VDIM), lambda g, c: (g, 0, 0, 0))
    bo, s_new = pl.pallas_call(
        functools.partial(_deltanet_kernel, nprob=nprob, nseq=nseq, C=C),
        grid=(G // nprob, nc),
        in_specs=[pl.BlockSpec((nprob, R, DN_QK), rmap(0)),
                  pl.BlockSpec((nprob, R, DN_QK), rmap(1)),
                  pl.BlockSpec((nprob, R, DN_V), rmap(2)),
                  pl.BlockSpec((nprob, R, DN_V), rmap(0)),
                  pl.BlockSpec((nprob, R, LANES), rmap(0)),
                  pl.BlockSpec((nsq, CONV_W - 1, DN_CONV_CH), lambda g, c: (g, 0, 0)),
                  state_spec,
                  pl.BlockSpec((CONV_W, DN_CONV_CH), lambda g, c: (0, 0)),
                  pl.BlockSpec((1, DN_VDIM), lambda g, c: (0, 0))],
        out_specs=[pl.BlockSpec((nprob, R, DN_V), rmap(0)), state_spec],
        out_shape=[jax.ShapeDtypeStruct((G, R * nc, DN_V), BF16),
                   jax.ShapeDtypeStruct(s0.shape, F32)],
        scratch_shapes=[pltpu.VMEM((nsq, C + SUBLANES, DN_CONV_CH), F32),
                        pltpu.VMEM((nsq, DN_HEADS, DN_KDIM, DN_VDIM), F32)],
        compiler_params=_params(("arbitrary", "arbitrary")),
        name="deltanet",
    )(view(zqkv), view(zqkv), view(zqkv), view(sz), view(bg), conv_init, s0, conv_w, dn_norm.reshape(1, DN_VDIM))
    return bo.reshape(M, DN_V), s_new


def _outproj_kernel(u_ref, vn_ref, ga_ref, gb_ref, bo_ref, x_ref, mod_ref, ws_ref, bs_ref, wpa_ref, wpb_ref,
                    wo_ref, n2_ref, wr_ref, br_ref, before_ref, x1_ref, h2_ref, rt_ref, cnt_ref, cnt_s):
    @pl.when(pl.program_id(0) == 0)
    def _():
        cnt_s[...] = jnp.zeros_like(cnt_s)

    tm = u_ref.shape[0]
    hm = tm // OUT_SPLIT
    bb, tt, _ = x_ref.shape
    HR = range(OUT_SPLIT)
    rows = [slice(h * hm, (h + 1) * hm) for h in HR]
    if bb == 1:
        xsl = [(slice(None), slice(h * (tt // OUT_SPLIT), (h + 1) * (tt // OUT_SPLIT))) for h in HR]
        msl = [slice(None)] * OUT_SPLIT
    else:
        xsl = [(slice(h * (bb // OUT_SPLIT), (h + 1) * (bb // OUT_SPLIT)), slice(None)) for h in HR]
        msl = [s[0] for s in xsl]

    def spatial(h):
        parts = []
        for r in range(hm // GMLP_CHUNK):
            rs = slice(h * hm + r * GMLP_CHUNK, h * hm + (r + 1) * GMLP_CHUNK)
            vn = vn_ref[rs, :].astype(BF16)
            cols = [jnp.dot(ws_ref[g], vn[:, g * A_GDIM:(g + 1) * A_GDIM], preferred_element_type=F32)
                    for g in range(A_GROUPS)]
            parts.append(jnp.concatenate(cols, axis=1) + bs_ref[...])
        return parts[0] if len(parts) == 1 else jnp.concatenate(parts, axis=0)

    sv = [spatial(h) for h in HR]
    a_out = [u_ref[rows[h], :].astype(F32) * sv[h] for h in HR]
    pa = [_dot(a_out[h], wpa_ref[...]) for h in HR]
    pb = [_dot(bo_ref[rows[h], :], wpb_ref[...]) for h in HR]
    m = [ga_ref[rows[h], :].astype(F32) * pa[h] + gb_ref[rows[h], :].astype(F32) * pb[h] for h in HR]
    mix = [_dot(m[h], wo_ref[...]) for h in HR]
    mod = [mod_ref[msl[h]] for h in HR]
    x = [x_ref[xsl[h][0], xsl[h][1], :] for h in HR]
    x1 = [x[h] + mod[h][:, 2:3, :] * mix[h].reshape(x[h].shape) for h in HR]
    for h in HR:
        x1_ref[xsl[h][0], xsl[h][1], :] = x1[h]
    y = [x1[h] * lax.rsqrt(jnp.mean(x1[h] * x1[h], axis=-1, keepdims=True) + EPS) * n2_ref[...] for h in HR]
    h2 = [(y[h] * (1.0 + mod[h][:, 4:5, :]) + mod[h][:, 3:4, :]).reshape(hm, D_MODEL) for h in HR]
    for h in HR:
        for j in range(SUBLANES):
            h2_ref[pl.ds(h * hm * SUBLANES + j, hm, stride=SUBLANES), :] = h2[h][:, j * LANES:(j + 1) * LANES]

    wr = wr_ref[...]
    lane = lax.broadcasted_iota(jnp.int32, (hm, LANES), 1)
    lanef = lane.astype(F32)
    neg = jnp.float32(-jnp.inf)
    big = jnp.float32(1e9)

    def route(h2h):
        hh, hl = _hilo(h2h)
        p = jnp.dot(hh, wr, preferred_element_type=F32)
        logits = (p[:, :LANES] + (p[:, LANES:] + jnp.dot(hl, wr[:, :LANES], preferred_element_type=F32))
                  + br_ref[...])
        gl = jnp.where(lane < N_GROUPS, logits, neg)
        gmax = jnp.max(gl, axis=-1, keepdims=True)
        gsel = jnp.min(jnp.where(gl == gmax, lanef, big), axis=-1, keepdims=True)
        gp = 1.0 / jnp.sum(jnp.exp(gl - gmax), axis=-1, keepdims=True)
        lo = ROUTER_LANE0 + EXP_PER_GROUP * gsel
        in_grp = (lanef >= lo) & (lanef < lo + EXP_PER_GROUP)
        el = jnp.where(in_grp, logits, neg)
        m1 = jnp.max(el, axis=-1, keepdims=True)
        i1 = jnp.min(jnp.where(el == m1, lanef, big), axis=-1, keepdims=True)
        el2 = jnp.where(lanef == i1, neg, el)
        m2 = jnp.max(el2, axis=-1, keepdims=True)
        i2 = jnp.min(jnp.where(el2 == m2, lanef, big), axis=-1, keepdims=True)
        ex = jnp.exp(m2 - m1)
        return i1 - ROUTER_LANE0, i2 - ROUTER_LANE0, gp / (1.0 + ex), gp * ex / (1.0 + ex)

    routed = [route(h2[h]) for h in HR]

    seen0 = cnt_s[...]
    for h in HR:
        e1, e2, w1, w2 = routed[h]
        oh1 = jnp.where(lanef == e1, 1.0, 0.0)
        oh2 = jnp.where(lanef == e2, 1.0, 0.0)
        oh = oh1 + oh2
        seen = jnp.dot(before_ref[...], oh.astype(BF16), preferred_element_type=F32) + seen0
        r1 = jnp.sum(oh1 * seen, axis=-1, keepdims=True)
        r2 = jnp.sum(oh2 * seen, axis=-1, keepdims=True)
        seen0 = seen0 + jnp.sum(oh, axis=0, keepdims=True)
        rec = jnp.zeros((hm, LANES), F32)
        for k, val in enumerate((e1, e2, r1, r2, w1, w2)):
            rec = jnp.where(lane == k, val, rec)
        rt_ref[rows[h], :] = rec
    cnt_s[...] = seen0
    cnt_ref[...] = seen0


def _outproj(u, vn, ga, gb, bo, x, mod, ws_eff, bs_full, w_pa, w_pb, w_o, norm2, w_r, b_r, bb, tt):
    B, T, _ = x.shape
    tm = bb * tt
    tpb = T // tt
    n_tiles = (B // bb) * tpb
    M = B * T
    cmap = lambda col: (lambda i: (i, col))
    full2 = lambda i: (0, 0)
    return pl.pallas_call(
        _outproj_kernel,
        grid=(n_tiles,),
        in_specs=[pl.BlockSpec((tm, A_WIDTH), cmap(0)),
                  pl.BlockSpec((tm, A_WIDTH), cmap(0)),
                  pl.BlockSpec((tm, D_MODEL), cmap(0)),
                  pl.BlockSpec((tm, D_MODEL), cmap(0)),
                  pl.BlockSpec((tm, DN_V), cmap(0)),
                  pl.BlockSpec((bb, tt, D_MODEL), lambda i: (i // tpb, i % tpb, 0)),
                  pl.BlockSpec((bb, N_MOD, D_MODEL), lambda i: (i // tpb, 0, 0)),
                  pl.BlockSpec((A_GROUPS, GMLP_CHUNK, GMLP_CHUNK), lambda i: (0, 0, 0)),
                  pl.BlockSpec((GMLP_CHUNK, A_WIDTH), full2),
                  pl.BlockSpec((A_WIDTH, D_MODEL), full2),
                  pl.BlockSpec((DN_V, D_MODEL), full2),
                  pl.BlockSpec((D_MODEL, D_MODEL), full2),
                  pl.BlockSpec((1, D_MODEL), full2),
                  pl.BlockSpec((D_MODEL, 2 * LANES), full2),
                  pl.BlockSpec((1, LANES), full2),
                  pl.BlockSpec((tm // OUT_SPLIT, tm // OUT_SPLIT), full2)],
        out_specs=[pl.BlockSpec((bb, tt, D_MODEL), lambda i: (i // tpb, i % tpb, 0)),
                   pl.BlockSpec((tm * SUBLANES, LANES), cmap(0)),
                   pl.BlockSpec((tm, LANES), cmap(0)),
                   pl.BlockSpec((1, LANES), full2)],
        out_shape=[jax.ShapeDtypeStruct((B, T, D_MODEL), F32),
                   jax.ShapeDtypeStruct((M * SUBLANES, LANES), F32),
                   jax.ShapeDtypeStruct((M, LANES), F32),
                   jax.ShapeDtypeStruct((1, LANES), F32)],
        scratch_shapes=[pltpu.VMEM((1, LANES), F32)],
        compiler_params=_params(("arbitrary",)),
        name="out_proj",
    )(u, vn, ga, gb, bo, x, mod, ws_eff, bs_full, w_pa, w_pb, w_o, norm2.reshape(1, D_MODEL), w_r, b_r,
      jnp.tril(jnp.ones((tm // OUT_SPLIT, tm // OUT_SPLIT), BF16), -1))


def _tile_copy(src_ref, src_off, dst_ref, dst_off, sem):
    return pltpu.make_async_copy(src_ref.at[pl.ds(pl.multiple_of(src_off, SUBLANES), SUBLANES), :],
                                 dst_ref.at[pl.ds(pl.multiple_of(dst_off, SUBLANES), SUBLANES), :], sem)


def _to_matrix(ref, n):
    return jnp.concatenate([ref[pl.ds(j, n, stride=SUBLANES), :] for j in range(SUBLANES)], axis=1)


def _zero_fill(pad_lo_ref, pad_n_ref, na_ref, xs_ref, zero_s, zsem):
    blk = EXPERT_TILE * SUBLANES
    n_tiles = xs_ref.shape[0] // blk
    zero_s[...] = jnp.zeros_like(zero_s)
    run = lambda off, rows: pltpu.make_async_copy(
        zero_s.at[pl.ds(0, rows * SUBLANES), :],
        xs_ref.at[pl.ds(pl.multiple_of(off * SUBLANES, SUBLANES), rows * SUBLANES), :], zsem)
    sizes = [1 << b for b in reversed(range(EXPERT_TILE.bit_length() - 1))]
    for wait in (False, True):
        for e in range(N_EXPERTS):
            off = pad_lo_ref[e]
            for rows in sizes:
                bit = pad_n_ref[e] & rows

                @pl.when(bit != 0)
                def _(off=off, rows=rows):
                    run(0, rows).wait() if wait else run(off, rows).start()

                off = off + bit

        def idle(i, carry):
            run(0, EXPERT_TILE).wait() if wait else run(i * EXPERT_TILE, EXPERT_TILE).start()
            return carry

        lax.fori_loop(na_ref[0], n_tiles, idle, 0)


def _dispatch_kernel(pad_lo_ref, pad_n_ref, na_ref, pos_ref, *refs, first_tile):
    n_paths = len(first_tile) - 1
    h_refs = refs[:n_paths]
    xs_ref, zero_s, sem, zsem = refs[n_paths:]
    i = pl.program_id(0)

    @pl.when(i == 0)
    def _():
        _zero_fill(pad_lo_ref, pad_n_ref, na_ref, xs_ref, zero_s, zsem)

    tm = h_refs[0].shape[0] // SUBLANES
    for p, h_ref in enumerate(h_refs):
        @pl.when((i >= first_tile[p]) & (i < first_tile[p + 1]))
        def _(h_ref=h_ref):
            def issue(t, carry):
                for k in range(2):
                    _tile_copy(h_ref, t * SUBLANES, xs_ref, pos_ref[0, k, t], sem).start(priority=k)
                return carry

            lax.fori_loop(0, tm, issue, 0, unroll=8)

            def drain(t, carry):
                for k in range(2):
                    _tile_copy(h_ref, 0, xs_ref, 0, sem).wait()
                return carry

            lax.fori_loop(0, tm, drain, 0, unroll=8)


def _dispatch(h2s, pos, tm, n_rows, pad_lo, pad_n, n_active):
    blk = tm * SUBLANES
    tiles = [h.shape[0] // blk for h in h2s]
    first_tile = [sum(tiles[:p]) for p in range(len(tiles) + 1)]
    hmap = lambda p: (lambda i, *_: (jnp.clip(i - first_tile[p], 0, tiles[p] - 1), 0))
    grid_spec = pltpu.PrefetchScalarGridSpec(
        num_scalar_prefetch=3,
        grid=(first_tile[-1],),
        in_specs=[pl.BlockSpec((1, 2, tm), lambda i, *_: (i, 0, 0), memory_space=pltpu.SMEM)]
                 + [pl.BlockSpec((blk, LANES), hmap(p)) for p in range(len(tiles))],
        out_specs=pl.BlockSpec(memory_space=pl.ANY),
        scratch_shapes=[pltpu.VMEM((EXPERT_TILE * SUBLANES, LANES), F32),
                        pltpu.SemaphoreType.DMA(()), pltpu.SemaphoreType.DMA(())],
    )
    return pl.pallas_call(
        functools.partial(_dispatch_kernel, first_tile=tuple(first_tile)),
        grid_spec=grid_spec,
        out_shape=jax.ShapeDtypeStruct((n_rows * SUBLANES, LANES), F32),
        compiler_params=_params(("arbitrary",)),
        name="moe_dispatch",
    )(pad_lo, pad_n, n_active, pos, *h2s)


def _expert_kernel(te_ref, na_ref, x_ref, *refs):
    del te_ref
    w_refs, o_ref = refs[:-1], refs[-1]
    blk = EXPERT_TILE * SUBLANES
    live = EXPERTS_PER_STEP * pl.program_id(0) < na_ref[0]

    @pl.when(live)
    def _():
        TR = range(EXPERTS_PER_STEP)
        wg, wu, wd = ([w_refs[3 * t + k][0] for t in TR] for k in range(3))
        x = [_to_matrix(x_ref.at[pl.ds(t * blk, blk), :], EXPERT_TILE).astype(BF16) for t in TR]
        a = [_dot(x[t], wg[t]) for t in TR]
        u = [_dot(x[t], wu[t]) for t in TR]
        o = [_dot(a[t] * _sigmoid(a[t]) * u[t], wd[t]) for t in TR]
        for t in TR:
            for j in range(SUBLANES):
                o_ref[pl.ds(t * blk + j, EXPERT_TILE, stride=SUBLANES), :] = o[t][:, j * LANES:(j + 1) * LANES]

    @pl.when(jnp.logical_not(live))
    def _():
        o_ref[...] = jnp.zeros_like(o_ref)


def _experts(xs, tile_expert, n_active, wg, wu, wd):
    blk = EXPERT_TILE * SUBLANES * EXPERTS_PER_STEP
    n_steps = xs.shape[0] // blk
    last = lambda na: jnp.maximum(na[0] - 1, 0)
    xmap = lambda i, te, na: (jnp.minimum(i, last(na) // EXPERTS_PER_STEP), 0)
    wmap = lambda t: (lambda i, te, na: (te[jnp.minimum(EXPERTS_PER_STEP * i + t, last(na))], 0, 0))
    w_specs = []
    for t in range(EXPERTS_PER_STEP):
        w_specs += [pl.BlockSpec((1, D_MODEL, D_EXPERT), wmap(t)),
                    pl.BlockSpec((1, D_MODEL, D_EXPERT), wmap(t)),
                    pl.BlockSpec((1, D_EXPERT, D_MODEL), wmap(t))]
    grid_spec = pltpu.PrefetchScalarGridSpec(
        num_scalar_prefetch=2,
        grid=(n_steps,),
        in_specs=[pl.BlockSpec((blk, LANES), xmap)] + w_specs,
        out_specs=pl.BlockSpec((blk, LANES), lambda i, te, na: (i, 0)),
    )
    return pl.pallas_call(
        _expert_kernel,
        grid_spec=grid_spec,
        out_shape=jax.ShapeDtypeStruct(xs.shape, F32),
        compiler_params=_params(("arbitrary",)),
        name="moe_experts",
    )(tile_expert, n_active, xs, *([wg, wu, wd] * EXPERTS_PER_STEP))


def _combine_kernel(pos_ref, pos_next_ref, rt_ref, x1_ref, mod_ref, fn_ref, ys_ref, y_ref, g_s, sems):
    tm = rt_ref.shape[0]
    i = pl.program_id(0)
    n = pl.num_programs(0)
    slot = i % 2

    def gather(p_ref, s):
        def issue(t, carry):
            for k in range(2):
                _tile_copy(ys_ref, p_ref[0, k, t], g_s.at[s, k], t * SUBLANES, sems.at[s]).start(priority=k)
            return carry

        lax.fori_loop(0, tm, issue, 0, unroll=8)

    @pl.when(i == 0)
    def _():
        gather(pos_ref, slot)

    @pl.when(i + 1 < n)
    def _():
        gather(pos_next_ref, 1 - slot)

    def drain(t, carry):
        for k in range(2):
            _tile_copy(ys_ref, 0, g_s.at[slot, k], 0, sems.at[slot]).wait()
        return carry

    lax.fori_loop(0, tm, drain, 0, unroll=8)

    rt = rt_ref[...]
    moe = rt[:, 4:5] * _to_matrix(g_s.at[slot, 0], tm) + rt[:, 5:6] * _to_matrix(g_s.at[slot, 1], tm)
    mod = mod_ref[...]
    x1 = x1_ref[...]
    x2 = x1 + mod[:, 5:6, :] * moe.reshape(x1.shape)
    y_ref[...] = x2 * lax.rsqrt(jnp.mean(x2 * x2, axis=-1, keepdims=True) + EPS) * fn_ref[...]


def _combine(ys, pos, route, x1, mod, final_norm, bb, tt):
    B, T, _ = x1.shape
    tm = bb * tt
    tpb = T // tt
    n_tiles = (B // bb) * tpb
    xmap = lambda i: (i // tpb, i % tpb, 0)
    return pl.pallas_call(
        _combine_kernel,
        grid=(n_tiles,),
        in_specs=[pl.BlockSpec((1, 2, tm), lambda i: (i, 0, 0), memory_space=pltpu.SMEM),
                  pl.BlockSpec((1, 2, tm), lambda i: (jnp.minimum(i + 1, n_tiles - 1), 0, 0),
                               memory_space=pltpu.SMEM),
                  pl.BlockSpec((tm, LANES), lambda i: (i, 0)),
                  pl.BlockSpec((bb, tt, D_MODEL), xmap),
                  pl.BlockSpec((bb, N_MOD, D_MODEL), lambda i: (i // tpb, 0, 0)),
                  pl.BlockSpec((1, D_MODEL), lambda i: (0, 0)),
                  pl.BlockSpec(memory_space=pl.ANY)],
        out_specs=pl.BlockSpec((bb, tt, D_MODEL), xmap),
        out_shape=jax.ShapeDtypeStruct((B, T, D_MODEL), F32),
        scratch_shapes=[pltpu.VMEM((2, 2, tm * SUBLANES, LANES), F32), pltpu.SemaphoreType.DMA((2,))],
        compiler_params=_params(("arbitrary",)),
        name="moe_combine",
    )(pos, pos, route, x1, mod, final_norm.reshape(1, D_MODEL), ys)


def _moe(paths, wg, wu, wd, final_norm):
    n_assign = sum(2 * p["route"].shape[0] for p in paths)
    max_tiles = (n_assign + N_EXPERTS * (EXPERT_TILE - 1)) // EXPERT_TILE
    max_tiles += -max_tiles % EXPERTS_PER_STEP
    cnts = [p["counts"][0, :N_EXPERTS].astype(jnp.int32) for p in paths]
    cnt = sum(cnts)
    nt = (cnt + EXPERT_TILE - 1) // EXPERT_TILE
    ends = jnp.cumsum(nt)
    base = ((ends - nt) * EXPERT_TILE).astype(jnp.int32)
    n_active = ends[-1:].astype(jnp.int32)
    tile_expert = jnp.minimum(jnp.sum(jnp.arange(max_tiles)[:, None] >= ends[None, :], axis=1),
                              N_EXPERTS - 1).astype(jnp.int32)
    tm = paths[0]["bb"] * paths[0]["tt"]
    assert all(p["bb"] * p["tt"] == tm for p in paths)
    start = base
    for p, c in zip(paths, cnts):
        eid = p["route"][:, 0:2].astype(jnp.int32)
        rank = p["route"][:, 2:4].astype(jnp.int32)
        first_row = jnp.sum(jnp.where(eid[..., None] == jnp.arange(N_EXPERTS), start, 0), axis=-1)
        pos = (first_row + rank) * SUBLANES
        p["pos"] = jnp.transpose(pos.reshape(-1, tm, 2), (0, 2, 1))
        start = start + c
    xs = _dispatch([p["h2"] for p in paths], jnp.concatenate([p["pos"] for p in paths], axis=0), tm,
                   max_tiles * EXPERT_TILE, base + cnt, nt * EXPERT_TILE - cnt, n_active)
    ys = _experts(xs, tile_expert, n_active, wg, wu, wd)
    return [_combine(ys, p["pos"], p["route"], p["x1"], p["mod"], final_norm, p["bb"], p["tt"]) for p in paths]


def _mixers(x, mod, conv_init, s0, prm, *, bb, tt, dn_nprob, dn_nseq, dn_chunk, vn_dtype):
    B, T, _ = x.shape
    u, vn, zqkv, sz, ga, gb, bg = _inproj(x, mod, prm["norm1"], prm["w_a"], prm["w_b"], prm["w_bg"], prm["ln_v_w"],
                                          prm["ln_v_b"], prm["adt"], bb, tt, vn_dtype)
    nc = T // dn_chunk
    bo, s_new = _deltanet(zqkv, sz, bg, conv_init, s0, prm["conv_w"], prm["dn_norm"], dn_nprob, dn_nseq, dn_chunk, nc)
    L = min(GMLP_CHUNK, T)
    reps = GMLP_CHUNK // L
    wm = prm["w_s_tril"][:, :L, :L]
    if reps > 1:
        blk = (jnp.arange(GMLP_CHUNK)[:, None] // L) == (jnp.arange(GMLP_CHUNK)[None, :] // L)
        wm = jnp.where(blk[None], jnp.tile(wm, (1, reps, reps)), 0.0)
    ws_eff = wm.astype(BF16)
    bs_rows = jnp.tile(jnp.transpose(prm["b_s"][:, :L]), (reps, 1))
    bs_full = jnp.repeat(bs_rows, A_GDIM, axis=1)
    x1, h2, route, counts = _outproj(u, vn, ga, gb, bo, x, mod, ws_eff, bs_full, prm["w_pa"], prm["w_pb"],
                                     prm["w_o"], prm["norm2"], prm["w_r"], prm["b_r"], bb, tt)
    tail = min(T, CONV_W - 1)
    zqkv_tail = zqkv.reshape(B, T, DN_CONV_CH)[:, T - tail:, :]
    conv_new = jnp.concatenate([conv_init, zqkv_tail], axis=1)[:, -(CONV_W - 1):, :]
    path = dict(h2=h2, route=route, counts=counts, x1=x1, mod=mod, bb=bb, tt=tt)
    return path, conv_new, s_new, vn.reshape(B, T, A_WIDTH)


def kernel(x_prompt, x_sample, c_prompt, c_sample, state_conv, state_delta, w_ada, b_ada, norm1, norm2, w_in, conv_w, a_log, dt_bias, dn_norm, ln_v_w, ln_v_b, w_s, b_s, w_pa, w_pb, w_o, w_rg, b_rg, w_re, b_re, w_e_gate, w_e_up, w_e_down, final_norm):
    depth = w_ada.shape[0]
    assert depth == 1
    bp, tp, _ = x_prompt.shape
    bs_, ts, _ = x_sample.shape
    l = 0
    assert w_in.shape[-1] == (N_ALIGNED + 2) * IN_BLOCK + 2 * DN_HEADS
    w_a = w_in[l:l + 1].astype(BF16)
    w_b, w_bg = _prep_in_weights(w_a)
    adt = jnp.stack([jnp.pad(a_log[l], (DN_HEADS, LANES - 2 * DN_HEADS)),
                     jnp.pad(dt_bias[l], (DN_HEADS, LANES - 2 * DN_HEADS))])
    w_r = jnp.pad(jnp.concatenate([w_rg[l], w_re[l]], axis=1), ((0, 0), (0, LANES - N_GROUPS - N_EXPERTS)))
    w_r_hi = w_r.astype(BF16)
    w_r = jnp.concatenate([w_r_hi, (w_r - w_r_hi.astype(F32)).astype(BF16)], axis=1)
    b_r = jnp.pad(jnp.concatenate([b_rg[l], b_re[l]]), (0, LANES - N_GROUPS - N_EXPERTS)).reshape(1, LANES)
    tri = jnp.tril(jnp.ones((GMLP_CHUNK, GMLP_CHUNK), F32))
    prm = dict(
        norm1=norm1[l], norm2=norm2[l], w_a=w_a, w_b=w_b, w_bg=w_bg, adt=adt, ln_v_w=ln_v_w[l], ln_v_b=ln_v_b[l],
        conv_w=conv_w[l], dn_norm=dn_norm[l], w_s_tril=w_s[l] * tri, b_s=b_s[l],
        w_pa=w_pa[l].astype(BF16), w_pb=w_pb[l].astype(BF16), w_o=w_o[l].astype(BF16), w_r=w_r, b_r=b_r,
        w_e_gate=w_e_gate[l].reshape(N_EXPERTS, D_MODEL, D_EXPERT),
        w_e_up=w_e_up[l].reshape(N_EXPERTS, D_MODEL, D_EXPERT),
        w_e_down=w_e_down[l].reshape(N_EXPERTS, D_EXPERT, D_MODEL),
    )

    c_all = jnp.concatenate([c_prompt, c_sample], axis=0)
    mod = _ada(c_all, w_ada[l], b_ada[l]).reshape(bp + bs_, N_MOD, D_MODEL)
    mod_p, mod_s = mod[:bp], mod[bp:]

    cinit_p = jnp.zeros((bp, CONV_W - 1, DN_CONV_CH), F32)
    cinit_s = state_conv[l]
    s0_p = jnp.zeros((bp, DN_HEADS, DN_KDIM, DN_VDIM), F32)

    path_p, conv_p, delta_p, _ = _mixers(x_prompt, mod_p, cinit_p, s0_p, prm,
                                         bb=1, tt=min(ROW_TILE, tp), dn_nprob=4 if bp % 4 == 0 else 1, dn_nseq=1,
                                         dn_chunk=min(DN_CHUNK, tp), vn_dtype=BF16)
    path_s, conv_s, delta_s, vn_s = _mixers(x_sample, mod_s, cinit_s, state_delta[l], prm,
                                            bb=min(bs_, ROW_TILE // ts), tt=ts,
                                            dn_nprob=2 if bs_ % (2 * DN_CHUNK // ts) == 0 else 1,
                                            dn_nseq=min(bs_, DN_CHUNK // ts), dn_chunk=ts, vn_dtype=F32)
    yp, ys = _moe([path_p, path_s], prm["w_e_gate"], prm["w_e_up"], prm["w_e_down"], final_norm)
    return (yp, ys, conv_p[None], conv_s[None], delta_p[None], delta_s[None], vn_s[None])
```

```python
import functools

import jax
import jax.numpy as jnp
from jax import lax
from jax.experimental import pallas as pl
from jax.experimental.pallas import tpu as pltpu

F32 = jnp.float32
BF16 = jnp.bfloat16

D_MODEL = 1024
A_WIDTH = 1024
A_GROUPS = 4
A_GDIM = A_WIDTH // A_GROUPS
GMLP_CHUNK = 128
DN_HEADS = 8
DN_KDIM = 128
DN_VDIM = 128
DN_QK = DN_HEADS * DN_KDIM
DN_V = DN_HEADS * DN_VDIM
DN_CONV_CH = 2 * DN_QK + DN_V
CONV_W = 4
DN_CHUNK = 64
N_GROUPS = 4
EXP_PER_GROUP = 8
N_EXPERTS = N_GROUPS * EXP_PER_GROUP
D_EXPERT = 256
N_MOD = 6
EPS = 1e-6

LANES = 128
SUBLANES = 8
ROUTER_LANE0 = N_GROUPS
VMEM_LIMIT = 56 * 1024 * 1024
ROW_TILE = 512
IN_SPLIT = 2
OUT_SPLIT = 2
EXPERT_TILE = 512
EXPERTS_PER_STEP = 1


def _sigmoid(x):
    return 1.0 / (1.0 + jnp.exp(-x))


def _gelu(x):
    return 0.5 * x * (1.0 + lax.erf(x * 0.7071067811865476))


def _softplus(x):
    return jnp.maximum(x, 0.0) + jnp.log(1.0 + jnp.exp(-jnp.abs(x)))


def _dot(a, b):
    return jnp.dot(a.astype(BF16), b.astype(BF16), preferred_element_type=F32)


def _split3(x):
    hi = x.astype(BF16)
    r1 = x - hi.astype(F32)
    mid = r1.astype(BF16)
    lo = (r1 - mid.astype(F32)).astype(BF16)
    return hi, mid, lo


def _dot_exact_lhs(a_bf16, x):
    hi, mid, lo = _split3(x)
    f = lambda p: jnp.dot(a_bf16, p, preferred_element_type=F32)
    return f(hi) + f(mid) + f(lo)


_PASSES = {"invp": 3, "invn": 3, "inv_hi": 1, "inv_kmax": 64, "sol": 1, "kk": 1, "ws": 1, "attn": 1, "state": 1}


def _hilo(x):
    hi = x.astype(BF16)
    lo = (x - hi.astype(F32)).astype(BF16)
    return hi, lo


def _mm(site, a, b, dims=(((1,), (0,)), ((), ()))):
    f = lambda p, q: lax.dot_general(p, q, dims, preferred_element_type=F32)
    if _PASSES[site] == 1:
        return f(a.astype(BF16), b.astype(BF16))
    ah, al = _hilo(a)
    bh, bl = _hilo(b)
    return f(ah, bh) + (f(ah, bl) + f(al, bh))


_NT = (((1,), (1,)), ((), ()))
_TN = (((0,), (0,)), ((), ()))

def _params(sem):
    return pltpu.CompilerParams(dimension_semantics=sem, vmem_limit_bytes=VMEM_LIMIT)


def _ada_kernel(c_ref, w_ref, b_ref, o_ref):
    c = c_ref[...]
    o_ref[...] = _dot(c * _sigmoid(c), w_ref[...]) + b_ref[...]


def _ada(c_all, w_ada, b_ada):
    n = c_all.shape[0]
    width = w_ada.shape[1]
    bn = 512
    return pl.pallas_call(
        _ada_kernel,
        grid=(width // bn,),
        in_specs=[pl.BlockSpec((n, D_MODEL), lambda j: (0, 0)),
                  pl.BlockSpec((D_MODEL, bn), lambda j: (0, j)),
                  pl.BlockSpec((1, bn), lambda j: (0, j))],
        out_specs=pl.BlockSpec((n, bn), lambda j: (0, j)),
        out_shape=jax.ShapeDtypeStruct((n, width), F32),
        compiler_params=_params(("arbitrary",)),
        name="ada_mod",
    )(c_all, w_ada, b_ada.reshape(1, width))


IN_BLOCK = 1024
N_ALIGNED = 6


def _realign_kernel(a_ref, b_ref, o_ref, bg_ref):
    s = 2 * DN_HEADS
    a = a_ref[0]
    o_ref[...] = jnp.concatenate([a[:, s:], b_ref[0, :, :s]], axis=1).astype(o_ref.dtype)

    @pl.when(pl.program_id(0) == 0)
    def _():
        lane = lax.broadcasted_iota(jnp.int32, (D_MODEL, LANES), 1)
        bg_ref[...] = jnp.where(lane < s, a[:, :LANES], 0.0).astype(bg_ref.dtype)


def _prep_in_weights(w_in):
    w_b, w_bg = pl.pallas_call(
        _realign_kernel,
        grid=(2,),
        in_specs=[pl.BlockSpec((1, D_MODEL, IN_BLOCK), lambda j: (0, 0, N_ALIGNED + j)),
                  pl.BlockSpec((1, D_MODEL, IN_BLOCK), lambda j: (0, 0, N_ALIGNED + j + 1))],
        out_specs=[pl.BlockSpec((D_MODEL, IN_BLOCK), lambda j: (0, j)),
                   pl.BlockSpec((D_MODEL, LANES), lambda j: (0, 0))],
        out_shape=[jax.ShapeDtypeStruct((D_MODEL, 2 * IN_BLOCK), BF16),
                   jax.ShapeDtypeStruct((D_MODEL, LANES), BF16)],
        compiler_params=_params(("arbitrary",)),
        name="prep_w_gates",
    )(w_in, w_in)
    return w_b, w_bg


def _inproj_kernel(x_ref, mod_ref, n1_ref, wa_ref, wb_ref, wbg_ref, lnw_ref, lnb_ref, adt_ref,
                   u_ref, vn_ref, qkv_ref, sz_ref, ga_ref, gb_ref, bg_ref):
    tm = u_ref.shape[0]
    hm = tm // IN_SPLIT
    bb, tt, _ = x_ref.shape
    HR = range(IN_SPLIT)
    rows = [slice(h * hm, (h + 1) * hm) for h in HR]
    if bb == 1:
        xsl = [(slice(None), slice(h * (tt // IN_SPLIT), (h + 1) * (tt // IN_SPLIT))) for h in HR]
        msl = [slice(None)] * IN_SPLIT
    else:
        xsl = [(slice(h * (bb // IN_SPLIT), (h + 1) * (bb // IN_SPLIT)), slice(None)) for h in HR]
        msl = [sl[0] for sl in xsl]
    hb = []
    for h in HR:
        x = x_ref[xsl[h][0], xsl[h][1], :]
        y = x * lax.rsqrt(jnp.mean(x * x, axis=-1, keepdims=True) + EPS) * n1_ref[...]
        mod = mod_ref[msl[h]]
        hb.append((y * (1.0 + mod[:, 1:2, :]) + mod[:, 0:1, :]).reshape(hm, D_MODEL).astype(BF16))
    adt = adt_ref[...]
    lane = lax.broadcasted_iota(jnp.int32, (hm, LANES), 1)
    for h in HR:
        zbg = jnp.dot(hb[h], wbg_ref[...], preferred_element_type=F32)
        g = -jnp.exp(adt[0:1, :]) * _softplus(zbg + adt[1:2, :])
        bg_ref[rows[h], :] = jnp.where(lane < DN_HEADS, _sigmoid(zbg), g)

    n_a = wa_ref.shape[2] // IN_BLOCK
    blk = lambda h, j: jnp.dot(hb[h], wa_ref[0, :, j * IN_BLOCK:(j + 1) * IN_BLOCK] if j < n_a else
                               wb_ref[:, (j - n_a) * IN_BLOCK:(j - n_a + 1) * IN_BLOCK],
                               preferred_element_type=F32)
    for h in HR:
        u_ref[rows[h], :] = _gelu(blk(h, 0)).astype(u_ref.dtype)
    for h in HR:
        a = _gelu(blk(h, 1))
        ac = a - jnp.mean(a, axis=-1, keepdims=True)
        var = jnp.mean(ac * ac, axis=-1, keepdims=True)
        vn_ref[rows[h], :] = (ac * lax.rsqrt(var + EPS) * lnw_ref[...] + lnb_ref[...]).astype(vn_ref.dtype)
    for j in range(3):
        for h in HR:
            qkv_ref[rows[h], j * IN_BLOCK:(j + 1) * IN_BLOCK] = blk(h, 2 + j)
    for h in HR:
        z = blk(h, 5)
        sz_ref[rows[h], :] = (z * _sigmoid(z)).astype(sz_ref.dtype)
    for h in HR:
        ga_ref[rows[h], :] = _sigmoid(blk(h, 6)).astype(ga_ref.dtype)
    for h in HR:
        gb_ref[rows[h], :] = _sigmoid(blk(h, 7)).astype(gb_ref.dtype)


def _inproj(x, mod, norm1, w_a, w_b, w_bg, ln_w, ln_b, adt, bb, tt, vn_dtype):
    B, T, _ = x.shape
    tm = bb * tt
    tpb = T // tt
    n_tiles = (B // bb) * tpb
    M = B * T
    const = lambda i: (0, 0)
    rows = lambda i: (i, 0)
    resident = dict(pipeline_mode=pl.Buffered(1))
    in_specs = [pl.BlockSpec((bb, tt, D_MODEL), lambda i: (i // tpb, i % tpb, 0)),
                pl.BlockSpec((bb, N_MOD, D_MODEL), lambda i: (i // tpb, 0, 0)),
                pl.BlockSpec((1, D_MODEL), const),
                pl.BlockSpec((1, D_MODEL, N_ALIGNED * IN_BLOCK), lambda i: (0, 0, 0), **resident),
                pl.BlockSpec(w_b.shape, const, **resident),
                pl.BlockSpec((D_MODEL, LANES), const, **resident),
                pl.BlockSpec((1, A_WIDTH), const),
                pl.BlockSpec((1, A_WIDTH), const),
                pl.BlockSpec((2, LANES), const)]
    args = [x, mod, norm1.reshape(1, D_MODEL), w_a, w_b, w_bg, ln_w.reshape(1, A_WIDTH), ln_b.reshape(1, A_WIDTH),
            adt]
    out_specs = [pl.BlockSpec((tm, A_WIDTH), rows),
                 pl.BlockSpec((tm, A_WIDTH), rows),
                 pl.BlockSpec((tm, DN_CONV_CH), rows),
                 pl.BlockSpec((tm, DN_V), rows),
                 pl.BlockSpec((tm, D_MODEL), rows),
                 pl.BlockSpec((tm, D_MODEL), rows),
                 pl.BlockSpec((tm, LANES), rows)]
    out_shape = [jax.ShapeDtypeStruct((M, A_WIDTH), BF16),
                 jax.ShapeDtypeStruct((M, A_WIDTH), vn_dtype),
                 jax.ShapeDtypeStruct((M, DN_CONV_CH), F32),
                 jax.ShapeDtypeStruct((M, DN_V), BF16),
                 jax.ShapeDtypeStruct((M, D_MODEL), BF16),
                 jax.ShapeDtypeStruct((M, D_MODEL), BF16),
                 jax.ShapeDtypeStruct((M, LANES), F32)]
    return pl.pallas_call(
        _inproj_kernel,
        grid=(n_tiles,),
        in_specs=in_specs,
        out_specs=out_specs,
        out_shape=out_shape,
        compiler_params=_params(("arbitrary",)),
        name="in_proj",
    )(*args)


def _inv_unit_lower_minus_eye(lmats, nilpotent):
    ns = [-l for l in lmats]
    ps = list(lmats)
    k = 2
    while k < nilpotent:
        early = k <= _PASSES["inv_kmax"]
        ps = [_mm("invp" if early else "inv_hi", p, p) for p in ps]
        ns = [n + p + _mm("invn" if early else "inv_hi", n, p) for n, p in zip(ns, ps)]
        k *= 2
    return ns


def _head_sumsq(x):
    pair = 2 * DN_KDIM
    r = lax.broadcasted_iota(jnp.int32, (pair, pair), 0) >= DN_KDIM
    c = lax.broadcasted_iota(jnp.int32, (pair, pair), 1) >= DN_KDIM
    ones2 = jnp.where(r == c, 1.0, 0.0).astype(BF16)
    sq = (x * x).astype(BF16)
    return jnp.concatenate([jnp.dot(sq[:, p * pair:(p + 1) * pair], ones2, preferred_element_type=F32)
                            for p in range(x.shape[1] // pair)], axis=1)


def _deltanet_kernel(q_ref, k_ref, v_ref, z_ref, bg_ref, cinit_ref, s0_ref, cw_ref, dnn_ref,
                     bo_ref, sn_ref, xp_s, s_s, *, nprob, nseq, C):
    R = nseq * C
    c = pl.program_id(1)
    nc = pl.num_programs(1)

    @pl.when(c == 0)
    def _():
        xp_s[:, SUBLANES - (CONV_W - 1):SUBLANES, :] = cinit_ref[...]
        s_s[...] = s0_ref[...]

    @pl.when(c > 0)
    def _():
        xp_s[:, 0:SUBLANES, :] = xp_s[:, C:C + SUBLANES, :]

    for p in range(nprob):
        for s in range(nseq):
            i = p * nseq + s
            xp_s[i, SUBLANES:SUBLANES + C, 0:DN_QK] = q_ref[p, s * C:(s + 1) * C, :]
            xp_s[i, SUBLANES:SUBLANES + C, DN_QK:2 * DN_QK] = k_ref[p, s * C:(s + 1) * C, :]
            xp_s[i, SUBLANES:SUBLANES + C, 2 * DN_QK:DN_CONV_CH] = v_ref[p, s * C:(s + 1) * C, :]

    cw = cw_ref[...]
    base = SUBLANES - (CONV_W - 1)
    acc = None
    for j in range(CONV_W):
        term = xp_s[:, base + j:base + j + C, :] * cw[j:j + 1, :]
        acc = term if acc is None else acc + term
    qkv = acc.reshape(nprob * R, DN_CONV_CH)
    qkv = qkv * _sigmoid(qkv)
    q_all = qkv[:, 0:DN_QK]
    k_all = qkv[:, DN_QK:2 * DN_QK]
    v_all = qkv[:, 2 * DN_QK:DN_CONV_CH]
    qn_all = q_all * lax.rsqrt(_head_sumsq(q_all) + EPS) * (DN_KDIM ** -0.5)
    kn_all = k_all * lax.rsqrt(_head_sumsq(k_all) + EPS)

    row = lax.broadcasted_iota(jnp.int32, (R, R), 0)
    col = lax.broadcasted_iota(jnp.int32, (R, R), 1)
    if nseq > 1:
        shift = C.bit_length() - 1
        same = lax.shift_right_logical(row, shift) == lax.shift_right_logical(col, shift)
        tril = same & (col <= row)
        strict = same & (col < row)
    else:
        tril = col <= row
        strict = col < row
    tril_b = jnp.where(tril, 1.0, 0.0).astype(BF16)
    bg = [bg_ref[p] for p in range(nprob)]
    gc = [_dot_exact_lhs(tril_b, bg[p]) for p in range(nprob)]
    gct = [g.T for g in gc]
    dnn = dnn_ref[...]

    IT = [(p, h) for p in range(nprob) for h in range(DN_HEADS)]
    NI = range(len(IT))
    hsl = [slice(h * DN_KDIM, (h + 1) * DN_KDIM) for h in range(DN_HEADS)]
    rows = [slice(p * R, (p + 1) * R) for p in range(nprob)]
    qn = [qn_all[rows[p], hsl[h]] for p, h in IT]
    kn = [kn_all[rows[p], hsl[h]] for p, h in IT]
    vh = [v_all[rows[p], hsl[h]] for p, h in IT]
    beta = [bg[p][:, h:h + 1] for p, h in IT]
    gcol = [gc[p][:, DN_HEADS + h:DN_HEADS + h + 1] for p, h in IT]
    grow = [gct[p][DN_HEADS + h:DN_HEADS + h + 1, :] for p, h in IT]
    decay = [jnp.exp(jnp.minimum(gcol[i] - grow[i], 0.0)) for i in NI]
    eg = [jnp.exp(gcol[i]) for i in NI]
    kb = [kn[i] * beta[i] for i in NI]
    vb = [vh[i] * beta[i] for i in NI]
    a2 = [_mm("kk", jnp.concatenate([kb[i], qn[i]], axis=0), kn[i], _NT) for i in NI]
    lmat = [jnp.where(strict, a2[i][:R] * decay[i], 0.0) for i in NI]
    attn = [jnp.where(tril, a2[i][R:] * decay[i], 0.0) for i in NI]
    tn = _inv_unit_lower_minus_eye(lmat, C)
    rhs = [jnp.concatenate([vb[i], kb[i] * eg[i]], axis=1) for i in NI]
    sol = [rhs[i] + _mm("sol", tn[i], rhs[i]) for i in NI]
    u = [x[:, :DN_VDIM] for x in sol]
    w = [x[:, DN_VDIM:] for x in sol]
    qg = [qn[i] * eg[i] for i in NI]
    vnew = [[None] * nseq for _ in NI]
    qs = [[None] * nseq for _ in NI]
    for s in range(nseq):
        rs = slice(s * C, (s + 1) * C)
        st = [s_s[p * nseq + s, h] for p, h in IT]
        ws = [_mm("ws", jnp.concatenate([w[i][rs], qg[i][rs]], axis=0), st[i]) for i in NI]
        for i in NI:
            vnew[i][s] = u[i][rs] - ws[i][:C]
            qs[i][s] = ws[i][C:]
        glast = [gcol[i][(s + 1) * C - 1:(s + 1) * C, :] for i in NI]
        kd = [kn[i][rs] * jnp.exp(glast[i] - gcol[i][rs]) for i in NI]
        upd = [_mm("state", kd[i], vnew[i][s], _TN) for i in NI]
        for i, (p, h) in enumerate(IT):
            s_s[p * nseq + s, h] = st[i] * jnp.exp(glast[i]) + upd[i]
    cat = lambda parts: parts[0] if nseq == 1 else jnp.concatenate(parts, axis=0)
    o = [cat(qs[i]) + _mm("attn", attn[i], cat(vnew[i])) for i in NI]
    o_all = jnp.concatenate([jnp.concatenate(o[p * DN_HEADS:(p + 1) * DN_HEADS], axis=1) for p in range(nprob)],
                            axis=0)
    dnn_all = jnp.concatenate([dnn] * DN_HEADS, axis=1)
    on = o_all * lax.rsqrt(_head_sumsq(o_all) * (1.0 / DN_VDIM) + EPS) * dnn_all
    for p in range(nprob):
        bo_ref[p] = (on[rows[p]] * z_ref[p].astype(F32)).astype(bo_ref.dtype)

    @pl.when(c == nc - 1)
    def _():
        sn_ref[...] = s_s[...]


def _deltanet(zqkv, sz, bg, conv_init, s0, conv_w, dn_norm, nprob, nseq, C, nc):
    M = zqkv.shape[0]
    R = nseq * C
    G = M // (R * nc)
    view = lambda a: a.reshape(G, R * nc, a.shape[-1])
    rmap = lambda col: (lambda g, c: (g, c, col))
    nsq = nprob * nseq
    state_spec = pl.BlockSpec((nsq, DN_HEADS, DN_KDIM, DN_VDIM), lambda g, c: (g, 0, 0, 0))
    bo, s_new = pl.pallas_call(
        functools.partial(_deltanet_kernel, nprob=nprob, nseq=nseq, C=C),
        grid=(G // nprob, nc),
        in_specs=[pl.BlockSpec((nprob, R, DN_QK), rmap(0)),
                  pl.BlockSpec((nprob, R, DN_QK), rmap(1)),
                  pl.BlockSpec((nprob, R, DN_V), rmap(2)),
                  pl.BlockSpec((nprob, R, DN_V), rmap(0)),
                  pl.BlockSpec((nprob, R, LANES), rmap(0)),
                  pl.BlockSpec((nsq, CONV_W - 1, DN_CONV_CH), lambda g, c: (g, 0, 0)),
                  state_spec,
                  pl.BlockSpec((CONV_W, DN_CONV_CH), lambda g, c: (0, 0)),
                  pl.BlockSpec((1, DN_VDIM), lambda g, c: (0, 0))],
        out_specs=[pl.BlockSpec((nprob, R, DN_V), rmap(0)), state_spec],
        out_shape=[jax.ShapeDtypeStruct((G, R * nc, DN_V), BF16),
                   jax.ShapeDtypeStruct(s0.shape, F32)],
        scratch_shapes=[pltpu.VMEM((nsq, C + SUBLANES, DN_CONV_CH), F32),
                        pltpu.VMEM((nsq, DN_HEADS, DN_KDIM, DN_VDIM), F32)],
        compiler_params=_params(("arbitrary", "arbitrary")),
        name="deltanet",
    )(view(zqkv), view(zqkv), view(zqkv), view(sz), view(bg), conv_init, s0, conv_w, dn_norm.reshape(1, DN_VDIM))
    return bo.reshape(M, DN_V), s_new


def _outproj_kernel(u_ref, vn_ref, ga_ref, gb_ref, bo_ref, x_ref, mod_ref, ws_ref, bs_ref, wpa_ref, wpb_ref,
                    wo_ref, n2_ref, wr_ref, br_ref, before_ref, x1_ref, h2_ref, rt_ref, cnt_ref, cnt_s):
    @pl.when(pl.program_id(0) == 0)
    def _():
        cnt_s[...] = jnp.zeros_like(cnt_s)

    tm = u_ref.shape[0]
    hm = tm // OUT_SPLIT
    bb, tt, _ = x_ref.shape
    HR = range(OUT_SPLIT)
    rows = [slice(h * hm, (h + 1) * hm) for h in HR]
    if bb == 1:
        xsl = [(slice(None), slice(h * (tt // OUT_SPLIT), (h + 1) * (tt // OUT_SPLIT))) for h in HR]
        msl = [slice(None)] * OUT_SPLIT
    else:
        xsl = [(slice(h * (bb // OUT_SPLIT), (h + 1) * (bb // OUT_SPLIT)), slice(None)) for h in HR]
        msl = [s[0] for s in xsl]

    def spatial(h):
        parts = []
        for r in range(hm // GMLP_CHUNK):
            rs = slice(h * hm + r * GMLP_CHUNK, h * hm + (r + 1) * GMLP_CHUNK)
            vn = vn_ref[rs, :].astype(BF16)
            cols = [jnp.dot(ws_ref[g], vn[:, g * A_GDIM:(g + 1) * A_GDIM], preferred_element_type=F32)
                    for g in range(A_GROUPS)]
            parts.append(jnp.concatenate(cols, axis=1) + bs_ref[...])
        return parts[0] if len(parts) == 1 else jnp.concatenate(parts, axis=0)

    sv = [spatial(h) for h in HR]
    a_out = [u_ref[rows[h], :].astype(F32) * sv[h] for h in HR]
    pa = [_dot(a_out[h], wpa_ref[...]) for h in HR]
    pb = [_dot(bo_ref[rows[h], :], wpb_ref[...]) for h in HR]
    m = [ga_ref[rows[h], :].astype(F32) * pa[h] + gb_ref[rows[h], :].astype(F32) * pb[h] for h in HR]
    mix = [_dot(m[h], wo_ref[...]) for h in HR]
    mod = [mod_ref[msl[h]] for h in HR]
    x = [x_ref[xsl[h][0], xsl[h][1], :] for h in HR]
    x1 = [x[h] + mod[h][:, 2:3, :] * mix[h].reshape(x[h].shape) for h in HR]
    for h in HR:
        x1_ref[xsl[h][0], xsl[h][1], :] = x1[h]
    y = [x1[h] * lax.rsqrt(jnp.mean(x1[h] * x1[h], axis=-1, keepdims=True) + EPS) * n2_ref[...] for h in HR]
    h2 = [(y[h] * (1.0 + mod[h][:, 4:5, :]) + mod[h][:, 3:4, :]).reshape(hm, D_MODEL) for h in HR]
    for h in HR:
        for j in range(SUBLANES):
            h2_ref[pl.ds(h * hm * SUBLANES + j, hm, stride=SUBLANES), :] = h2[h][:, j * LANES:(j + 1) * LANES]

    wr = wr_ref[...]
    lane = lax.broadcasted_iota(jnp.int32, (hm, LANES), 1)
    lanef = lane.astype(F32)
    neg = jnp.float32(-jnp.inf)
    big = jnp.float32(1e9)

    def route(h2h):
        hh, hl = _hilo(h2h)
        p = jnp.dot(hh, wr, preferred_element_type=F32)
        logits = (p[:, :LANES] + (p[:, LANES:] + jnp.dot(hl, wr[:, :LANES], preferred_element_type=F32))
                  + br_ref[...])
        gl = jnp.where(lane < N_GROUPS, logits, neg)
        gmax = jnp.max(gl, axis=-1, keepdims=True)
        gsel = jnp.min(jnp.where(gl == gmax, lanef, big), axis=-1, keepdims=True)
        gp = 1.0 / jnp.sum(jnp.exp(gl - gmax), axis=-1, keepdims=True)
        lo = ROUTER_LANE0 + EXP_PER_GROUP * gsel
        in_grp = (lanef >= lo) & (lanef < lo + EXP_PER_GROUP)
        el = jnp.where(in_grp, logits, neg)
        m1 = jnp.max(el, axis=-1, keepdims=True)
        i1 = jnp.min(jnp.where(el == m1, lanef, big), axis=-1, keepdims=True)
        el2 = jnp.where(lanef == i1, neg, el)
        m2 = jnp.max(el2, axis=-1, keepdims=True)
        i2 = jnp.min(jnp.where(el2 == m2, lanef, big), axis=-1, keepdims=True)
        ex = jnp.exp(m2 - m1)
        return i1 - ROUTER_LANE0, i2 - ROUTER_LANE0, gp / (1.0 + ex), gp * ex / (1.0 + ex)

    routed = [route(h2[h]) for h in HR]

    seen0 = cnt_s[...]
    for h in HR:
        e1, e2, w1, w2 = routed[h]
        oh1 = jnp.where(lanef == e1, 1.0, 0.0)
        oh2 = jnp.where(lanef == e2, 1.0, 0.0)
        oh = oh1 + oh2
        seen = jnp.dot(before_ref[...], oh.astype(BF16), preferred_element_type=F32) + seen0
        r1 = jnp.sum(oh1 * seen, axis=-1, keepdims=True)
        r2 = jnp.sum(oh2 * seen, axis=-1, keepdims=True)
        seen0 = seen0 + jnp.sum(oh, axis=0, keepdims=True)
        rec = jnp.zeros((hm, LANES), F32)
        for k, val in enumerate((e1, e2, r1, r2, w1, w2)):
            rec = jnp.where(lane == k, val, rec)
        rt_ref[rows[h], :] = rec
    cnt_s[...] = seen0
    cnt_ref[...] = seen0


def _outproj(u, vn, ga, gb, bo, x, mod, ws_eff, bs_full, w_pa, w_pb, w_o, norm2, w_r, b_r, bb, tt):
    B, T, _ = x.shape
    tm = bb * tt
    tpb = T // tt
    n_tiles = (B // bb) * tpb
    M = B * T
    cmap = lambda col: (lambda i: (i, col))
    full2 = lambda i: (0, 0)
    return pl.pallas_call(
        _outproj_kernel,
        grid=(n_tiles,),
        in_specs=[pl.BlockSpec((tm, A_WIDTH), cmap(0)),
                  pl.BlockSpec((tm, A_WIDTH), cmap(0)),
                  pl.BlockSpec((tm, D_MODEL), cmap(0)),
                  pl.BlockSpec((tm, D_MODEL), cmap(0)),
                  pl.BlockSpec((tm, DN_V), cmap(0)),
                  pl.BlockSpec((bb, tt, D_MODEL), lambda i: (i // tpb, i % tpb, 0)),
                  pl.BlockSpec((bb, N_MOD, D_MODEL), lambda i: (i // tpb, 0, 0)),
                  pl.BlockSpec((A_GROUPS, GMLP_CHUNK, GMLP_CHUNK), lambda i: (0, 0, 0)),
                  pl.BlockSpec((GMLP_CHUNK, A_WIDTH), full2),
                  pl.BlockSpec((A_WIDTH, D_MODEL), full2),
                  pl.BlockSpec((DN_V, D_MODEL), full2),
                  pl.BlockSpec((D_MODEL, D_MODEL), full2),
                  pl.BlockSpec((1, D_MODEL), full2),
                  pl.BlockSpec((D_MODEL, 2 * LANES), full2),
                  pl.BlockSpec((1, LANES), full2),
                  pl.BlockSpec((tm // OUT_SPLIT, tm // OUT_SPLIT), full2)],
        out_specs=[pl.BlockSpec((bb, tt, D_MODEL), lambda i: (i // tpb, i % tpb, 0)),
                   pl.BlockSpec((tm * SUBLANES, LANES), cmap(0)),
                   pl.BlockSpec((tm, LANES), cmap(0)),
                   pl.BlockSpec((1, LANES), full2)],
        out_shape=[jax.ShapeDtypeStruct((B, T, D_MODEL), F32),
                   jax.ShapeDtypeStruct((M * SUBLANES, LANES), F32),
                   jax.ShapeDtypeStruct((M, LANES), F32),
                   jax.ShapeDtypeStruct((1, LANES), F32)],
        scratch_shapes=[pltpu.VMEM((1, LANES), F32)],
        compiler_params=_params(("arbitrary",)),
        name="out_proj",
    )(u, vn, ga, gb, bo, x, mod, ws_eff, bs_full, w_pa, w_pb, w_o, norm2.reshape(1, D_MODEL), w_r, b_r,
      jnp.tril(jnp.ones((tm // OUT_SPLIT, tm // OUT_SPLIT), BF16), -1))


def _tile_copy(src_ref, src_off, dst_ref, dst_off, sem):
    return pltpu.make_async_copy(src_ref.at[pl.ds(pl.multiple_of(src_off, SUBLANES), SUBLANES), :],
                                 dst_ref.at[pl.ds(pl.multiple_of(dst_off, SUBLANES), SUBLANES), :], sem)


def _to_matrix(ref, n):
    return jnp.concatenate([ref[pl.ds(j, n, stride=SUBLANES), :] for j in range(SUBLANES)], axis=1)


def _zero_fill(pad_lo_ref, pad_n_ref, na_ref, xs_ref, zero_s, zsem):
    blk = EXPERT_TILE * SUBLANES
    n_tiles = xs_ref.shape[0] // blk
    zero_s[...] = jnp.zeros_like(zero_s)
    run = lambda off, rows: pltpu.make_async_copy(
        zero_s.at[pl.ds(0, rows * SUBLANES), :],
        xs_ref.at[pl.ds(pl.multiple_of(off * SUBLANES, SUBLANES), rows * SUBLANES), :], zsem)
    sizes = [1 << b for b in reversed(range(EXPERT_TILE.bit_length() - 1))]
    for wait in (False, True):
        for e in range(N_EXPERTS):
            off = pad_lo_ref[e]
            for rows in sizes:
                bit = pad_n_ref[e] & rows

                @pl.when(bit != 0)
                def _(off=off, rows=rows):
                    run(0, rows).wait() if wait else run(off, rows).start()

                off = off + bit

        def idle(i, carry):
            run(0, EXPERT_TILE).wait() if wait else run(i * EXPERT_TILE, EXPERT_TILE).start()
            return carry

        lax.fori_loop(na_ref[0], n_tiles, idle, 0)


def _dispatch_kernel(pad_lo_ref, pad_n_ref, na_ref, pos_ref, *refs, first_tile):
    n_paths = len(first_tile) - 1
    h_refs = refs[:n_paths]
    xs_ref, zero_s, sem, zsem = refs[n_paths:]
    i = pl.program_id(0)

    @pl.when(i == 0)
    def _():
        _zero_fill(pad_lo_ref, pad_n_ref, na_ref, xs_ref, zero_s, zsem)

    tm = h_refs[0].shape[0] // SUBLANES
    for p, h_ref in enumerate(h_refs):
        @pl.when((i >= first_tile[p]) & (i < first_tile[p + 1]))
        def _(h_ref=h_ref):
            def issue(t, carry):
                for k in range(2):
                    _tile_copy(h_ref, t * SUBLANES, xs_ref, pos_ref[0, k, t], sem).start(priority=k)
                return carry

            lax.fori_loop(0, tm, issue, 0, unroll=8)

            def drain(t, carry):
                for k in range(2):
                    _tile_copy(h_ref, 0, xs_ref, 0, sem).wait()
                return carry

            lax.fori_loop(0, tm, drain, 0, unroll=8)


def _dispatch(h2s, pos, tm, n_rows, pad_lo, pad_n, n_active):
    blk = tm * SUBLANES
    tiles = [h.shape[0] // blk for h in h2s]
    first_tile = [sum(tiles[:p]) for p in range(len(tiles) + 1)]
    hmap = lambda p: (lambda i, *_: (jnp.clip(i - first_tile[p], 0, tiles[p] - 1), 0))
    grid_spec = pltpu.PrefetchScalarGridSpec(
        num_scalar_prefetch=3,
        grid=(first_tile[-1],),
        in_specs=[pl.BlockSpec((1, 2, tm), lambda i, *_: (i, 0, 0), memory_space=pltpu.SMEM)]
                 + [pl.BlockSpec((blk, LANES), hmap(p)) for p in range(len(tiles))],
        out_specs=pl.BlockSpec(memory_space=pl.ANY),
        scratch_shapes=[pltpu.VMEM((EXPERT_TILE * SUBLANES, LANES), F32),
                        pltpu.SemaphoreType.DMA(()), pltpu.SemaphoreType.DMA(())],
    )
    return pl.pallas_call(
        functools.partial(_dispatch_kernel, first_tile=tuple(first_tile)),
        grid_spec=grid_spec,
        out_shape=jax.ShapeDtypeStruct((n_rows * SUBLANES, LANES), F32),
        compiler_params=_params(("arbitrary",)),
        name="moe_dispatch",
    )(pad_lo, pad_n, n_active, pos, *h2s)


def _expert_kernel(te_ref, na_ref, x_ref, *refs):
    del te_ref
    w_refs, o_ref = refs[:-1], refs[-1]
    blk = EXPERT_TILE * SUBLANES
    live = EXPERTS_PER_STEP * pl.program_id(0) < na_ref[0]

    @pl.when(live)
    def _():
        TR = range(EXPERTS_PER_STEP)
        wg, wu, wd = ([w_refs[3 * t + k][0] for t in TR] for k in range(3))
        x = [_to_matrix(x_ref.at[pl.ds(t * blk, blk), :], EXPERT_TILE).astype(BF16) for t in TR]
        a = [_dot(x[t], wg[t]) for t in TR]
        u = [_dot(x[t], wu[t]) for t in TR]
        o = [_dot(a[t] * _sigmoid(a[t]) * u[t], wd[t]) for t in TR]
        for t in TR:
            for j in range(SUBLANES):
                o_ref[pl.ds(t * blk + j, EXPERT_TILE, stride=SUBLANES), :] = o[t][:, j * LANES:(j + 1) * LANES]

    @pl.when(jnp.logical_not(live))
    def _():
        o_ref[...] = jnp.zeros_like(o_ref)


def _experts(xs, tile_expert, n_active, wg, wu, wd):
    blk = EXPERT_TILE * SUBLANES * EXPERTS_PER_STEP
    n_steps = xs.shape[0] // blk
    last = lambda na: jnp.maximum(na[0] - 1, 0)
    xmap = lambda i, te, na: (jnp.minimum(i, last(na) // EXPERTS_PER_STEP), 0)
    wmap = lambda t: (lambda i, te, na: (te[jnp.minimum(EXPERTS_PER_STEP * i + t, last(na))], 0, 0))
    w_specs = []
    for t in range(EXPERTS_PER_STEP):
        w_specs += [pl.BlockSpec((1, D_MODEL, D_EXPERT), wmap(t)),
                    pl.BlockSpec((1, D_MODEL, D_EXPERT), wmap(t)),
                    pl.BlockSpec((1, D_EXPERT, D_MODEL), wmap(t))]
    grid_spec = pltpu.PrefetchScalarGridSpec(
        num_scalar_prefetch=2,
        grid=(n_steps,),
        in_specs=[pl.BlockSpec((blk, LANES), xmap)] + w_specs,
        out_specs=pl.BlockSpec((blk, LANES), lambda i, te, na: (i, 0)),
    )
    return pl.pallas_call(
        _expert_kernel,
        grid_spec=grid_spec,
        out_shape=jax.ShapeDtypeStruct(xs.shape, F32),
        compiler_params=_params(("arbitrary",)),
        name="moe_experts",
    )(tile_expert, n_active, xs, *([wg, wu, wd] * EXPERTS_PER_STEP))


def _combine_kernel(pos_ref, pos_next_ref, rt_ref, x1_ref, mod_ref, fn_ref, ys_ref, y_ref, g_s, sems):
    tm = rt_ref.shape[0]
    i = pl.program_id(0)
    n = pl.num_programs(0)
    slot = i % 2

    def gather(p_ref, s):
        def issue(t, carry):
            for k in range(2):
                _tile_copy(ys_ref, p_ref[0, k, t], g_s.at[s, k], t * SUBLANES, sems.at[s]).start(priority=k)
            return carry

        lax.fori_loop(0, tm, issue, 0, unroll=8)

    @pl.when(i == 0)
    def _():
        gather(pos_ref, slot)

    @pl.when(i + 1 < n)
    def _():
        gather(pos_next_ref, 1 - slot)

    def drain(t, carry):
        for k in range(2):
            _tile_copy(ys_ref, 0, g_s.at[slot, k], 0, sems.at[slot]).wait()
        return carry

    lax.fori_loop(0, tm, drain, 0, unroll=8)

    rt = rt_ref[...]
    moe = rt[:, 4:5] * _to_matrix(g_s.at[slot, 0], tm) + rt[:, 5:6] * _to_matrix(g_s.at[slot, 1], tm)
    mod = mod_ref[...]
    x1 = x1_ref[...]
    x2 = x1 + mod[:, 5:6, :] * moe.reshape(x1.shape)
    y_ref[...] = x2 * lax.rsqrt(jnp.mean(x2 * x2, axis=-1, keepdims=True) + EPS) * fn_ref[...]


def _combine(ys, pos, route, x1, mod, final_norm, bb, tt):
    B, T, _ = x1.shape
    tm = bb * tt
    tpb = T // tt
    n_tiles = (B // bb) * tpb
    xmap = lambda i: (i // tpb, i % tpb, 0)
    return pl.pallas_call(
        _combine_kernel,
        grid=(n_tiles,),
        in_specs=[pl.BlockSpec((1, 2, tm), lambda i: (i, 0, 0), memory_space=pltpu.SMEM),
                  pl.BlockSpec((1, 2, tm), lambda i: (jnp.minimum(i + 1, n_tiles - 1), 0, 0),
                               memory_space=pltpu.SMEM),
                  pl.BlockSpec((tm, LANES), lambda i: (i, 0)),
                  pl.BlockSpec((bb, tt, D_MODEL), xmap),
                  pl.BlockSpec((bb, N_MOD, D_MODEL), lambda i: (i // tpb, 0, 0)),
                  pl.BlockSpec((1, D_MODEL), lambda i: (0, 0)),
                  pl.BlockSpec(memory_space=pl.ANY)],
        out_specs=pl.BlockSpec((bb, tt, D_MODEL), xmap),
        out_shape=jax.ShapeDtypeStruct((B, T, D_MODEL), F32),
        scratch_shapes=[pltpu.VMEM((2, 2, tm * SUBLANES, LANES), F32), pltpu.SemaphoreType.DMA((2,))],
        compiler_params=_params(("arbitrary",)),
        name="moe_combine",
    )(pos, pos, route, x1, mod, final_norm.reshape(1, D_MODEL), ys)


def _moe(paths, wg, wu, wd, final_norm):
    n_assign = sum(2 * p["route"].shape[0] for p in paths)
    max_tiles = (n_assign + N_EXPERTS * (EXPERT_TILE - 1)) // EXPERT_TILE
    max_tiles += -max_tiles % EXPERTS_PER_STEP
    cnts = [p["counts"][0, :N_EXPERTS].astype(jnp.int32) for p in paths]
    cnt = sum(cnts)
    nt = (cnt + EXPERT_TILE - 1) // EXPERT_TILE
    ends = jnp.cumsum(nt)
    base = ((ends - nt) * EXPERT_TILE).astype(jnp.int32)
    n_active = ends[-1:].astype(jnp.int32)
    tile_expert = jnp.minimum(jnp.sum(jnp.arange(max_tiles)[:, None] >= ends[None, :], axis=1),
                              N_EXPERTS - 1).astype(jnp.int32)
    tm = paths[0]["bb"] * paths[0]["tt"]
    assert all(p["bb"] * p["tt"] == tm for p in paths)
    start = base
    for p, c in zip(paths, cnts):
        eid = p["route"][:, 0:2].astype(jnp.int32)
        rank = p["route"][:, 2:4].astype(jnp.int32)
        first_row = jnp.sum(jnp.where(eid[..., None] == jnp.arange(N_EXPERTS), start, 0), axis=-1)
        pos = (first_row + rank) * SUBLANES
        p["pos"] = jnp.transpose(pos.reshape(-1, tm, 2), (0, 2, 1))
        start = start + c
    xs = _dispatch([p["h2"] for p in paths], jnp.concatenate([p["pos"] for p in paths], axis=0), tm,
                   max_tiles * EXPERT_TILE, base + cnt, nt * EXPERT_TILE - cnt, n_active)
    ys = _experts(xs, tile_expert, n_active, wg, wu, wd)
    return [_combine(ys, p["pos"], p["route"], p["x1"], p["mod"], final_norm, p["bb"], p["tt"]) for p in paths]


def _mixers(x, mod, conv_init, s0, prm, *, bb, tt, dn_nprob, dn_nseq, dn_chunk, vn_dtype):
    B, T, _ = x.shape
    u, vn, zqkv, sz, ga, gb, bg = _inproj(x, mod, prm["norm1"], prm["w_a"], prm["w_b"], prm["w_bg"], prm["ln_v_w"],
                                          prm["ln_v_b"], prm["adt"], bb, tt, vn_dtype)
    nc = T // dn_chunk
    bo, s_new = _deltanet(zqkv, sz, bg, conv_init, s0, prm["conv_w"], prm["dn_norm"], dn_nprob, dn_nseq, dn_chunk, nc)
    L = min(GMLP_CHUNK, T)
    reps = GMLP_CHUNK // L
    wm = prm["w_s_tril"][:, :L, :L]
    if reps > 1:
        blk = (jnp.arange(GMLP_CHUNK)[:, None] // L) == (jnp.arange(GMLP_CHUNK)[None, :] // L)
        wm = jnp.where(blk[None], jnp.tile(wm, (1, reps, reps)), 0.0)
    ws_eff = wm.astype(BF16)
    bs_rows = jnp.tile(jnp.transpose(prm["b_s"][:, :L]), (reps, 1))
    bs_full = jnp.repeat(bs_rows, A_GDIM, axis=1)
    x1, h2, route, counts = _outproj(u, vn, ga, gb, bo, x, mod, ws_eff, bs_full, prm["w_pa"], prm["w_pb"],
                                     prm["w_o"], prm["norm2"], prm["w_r"], prm["b_r"], bb, tt)
    tail = min(T, CONV_W - 1)
    zqkv_tail = zqkv.reshape(B, T, DN_CONV_CH)[:, T - tail:, :]
    conv_new = jnp.concatenate([conv_init, zqkv_tail], axis=1)[:, -(CONV_W - 1):, :]
    path = dict(h2=h2, route=route, counts=counts, x1=x1, mod=mod, bb=bb, tt=tt)
    return path, conv_new, s_new, vn.reshape(B, T, A_WIDTH)


def kernel(x_prompt, x_sample, c_prompt, c_sample, state_conv, state_delta, w_ada, b_ada, norm1, norm2, w_in, conv_w, a_log, dt_bias, dn_norm, ln_v_w, ln_v_b, w_s, b_s, w_pa, w_pb, w_o, w_rg, b_rg, w_re, b_re, w_e_gate, w_e_up, w_e_down, final_norm):
    depth = w_ada.shape[0]
    assert depth == 1
    bp, tp, _ = x_prompt.shape
    bs_, ts, _ = x_sample.shape
    l = 0
    assert w_in.shape[-1] == (N_ALIGNED + 2) * IN_BLOCK + 2 * DN_HEADS
    w_a = w_in[l:l + 1].astype(BF16)
    w_b, w_bg = _prep_in_weights(w_a)
    adt = jnp.stack([jnp.pad(a_log[l], (DN_HEADS, LANES - 2 * DN_HEADS)),
                     jnp.pad(dt_bias[l], (DN_HEADS, LANES - 2 * DN_HEADS))])
    w_r = jnp.pad(jnp.concatenate([w_rg[l], w_re[l]], axis=1), ((0, 0), (0, LANES - N_GROUPS - N_EXPERTS)))
    w_r_hi = w_r.astype(BF16)
    w_r = jnp.concatenate([w_r_hi, (w_r - w_r_hi.astype(F32)).astype(BF16)], axis=1)
    b_r = jnp.pad(jnp.concatenate([b_rg[l], b_re[l]]), (0, LANES - N_GROUPS - N_EXPERTS)).reshape(1, LANES)
    tri = jnp.tril(jnp.ones((GMLP_CHUNK, GMLP_CHUNK), F32))
    prm = dict(
        norm1=norm1[l], norm2=norm2[l], w_a=w_a, w_b=w_b, w_bg=w_bg, adt=adt, ln_v_w=ln_v_w[l], ln_v_b=ln_v_b[l],
        conv_w=conv_w[l], dn_norm=dn_norm[l], w_s_tril=w_s[l] * tri, b_s=b_s[l],
        w_pa=w_pa[l].astype(BF16), w_pb=w_pb[l].astype(BF16), w_o=w_o[l].astype(BF16), w_r=w_r, b_r=b_r,
        w_e_gate=w_e_gate[l].reshape(N_EXPERTS, D_MODEL, D_EXPERT),
        w_e_up=w_e_up[l].reshape(N_EXPERTS, D_MODEL, D_EXPERT),
        w_e_down=w_e_down[l].reshape(N_EXPERTS, D_EXPERT, D_MODEL),
    )

    c_all = jnp.concatenate([c_prompt, c_sample], axis=0)
    mod = _ada(c_all, w_ada[l], b_ada[l]).reshape(bp + bs_, N_MOD, D_MODEL)
    mod_p, mod_s = mod[:bp], mod[bp:]

    cinit_p = jnp.zeros((bp, CONV_W - 1, DN_CONV_CH), F32)
    cinit_s = state_conv[l]
    s0_p = jnp.zeros((bp, DN_HEADS, DN_KDIM, DN_VDIM), F32)

    path_p, conv_p, delta_p, _ = _mixers(x_prompt, mod_p, cinit_p, s0_p, prm,
                                         bb=1, tt=min(ROW_TILE, tp), dn_nprob=4 if bp % 4 == 0 else 1, dn_nseq=1,
                                         dn_chunk=min(DN_CHUNK, tp), vn_dtype=BF16)
    path_s, conv_s, delta_s, vn_s = _mixers(x_sample, mod_s, cinit_s, state_delta[l], prm,
                                            bb=min(bs_, ROW_TILE // ts), tt=ts,
                                            dn_nprob=2 if bs_ % (2 * DN_CHUNK // ts) == 0 else 1,
                                            dn_nseq=min(bs_, DN_CHUNK // ts), dn_chunk=ts, vn_dtype=F32)
    yp, ys = _moe([path_p, path_s], prm["w_e_gate"], prm["w_e_up"], prm["w_e_down"], final_norm)
    return (yp, ys, conv_p[None], conv_s[None], delta_p[None], delta_s[None], vn_s[None])
```
